```python
import jax, jax.numpy as jnp
from jax import lax
import numpy as np

D_MODEL = 2048
BATCH = 4
SEQ = 4096
DEPTH = 2

GRID_W = 64
NA_HEADS = 8
NA_HEAD_DIM = 128
NA_WIDTH = NA_HEADS * NA_HEAD_DIM
NA_KR_MAX = 8
NA_KC = 16
ML_HEADS = 4
ML_QK_DIM = 128
ML_V_DIM = 256
ML_QK_WIDTH = ML_HEADS * ML_QK_DIM
ML_V_WIDTH = ML_HEADS * ML_V_DIM
ML_CHUNK = 64
CONV_W = 5
N_GATES = 4 * ML_HEADS
D_MIX = NA_WIDTH + ML_V_WIDTH
N_IN = 3 * NA_WIDTH + 2 * ML_QK_WIDTH + 2 * ML_V_WIDTH + N_GATES
IN_SPLITS = (NA_WIDTH, 2 * NA_WIDTH, 3 * NA_WIDTH,
             3 * NA_WIDTH + ML_QK_WIDTH,
             3 * NA_WIDTH + 2 * ML_QK_WIDTH,
             3 * NA_WIDTH + 2 * ML_QK_WIDTH + ML_V_WIDTH,
             3 * NA_WIDTH + 2 * ML_QK_WIDTH + 2 * ML_V_WIDTH)
N_EXPERTS = 16
CAPACITY_FACTOR = 2
D_EXPERT = 1024
EPS = 1e-6

kernel_name = 'hybrid_na_mlstm_ec_encoder'


def rmsnorm(x, g):
    xf = x.astype(jnp.float32)
    y = xf * lax.rsqrt(jnp.mean(xf * xf, axis=-1, keepdims=True) + EPS)
    return (y * g.astype(jnp.float32)).astype(x.dtype)


def neighbourhood_attention(q, k, v, rpb):
    B, S, _ = q.shape
    H, d = NA_HEADS, NA_HEAD_DIM
    rows = S // GRID_W
    kr = min(NA_KR_MAX, rows)
    r = np.arange(rows)
    row_start = np.clip(r - kr // 2, 0, rows - kr)
    key_rows = row_start[:, None] + np.arange(kr)[None, :]
    c = np.arange(GRID_W)
    col_start = np.clip(c - NA_KC // 2, 0, GRID_W - NA_KC)
    col_in = (c[None, :] >= col_start[:, None]) & (c[None, :] < col_start[:, None] + NA_KC)
    dr = key_rows - r[:, None] + (NA_KR_MAX - 1)
    dc = np.clip(c[None, :] - c[:, None] + (NA_KC - 1), 0, 2 * NA_KC - 2)

    qg = q.reshape(B, rows, GRID_W, H, d)
    kg = k.reshape(B, rows, GRID_W, H, d)[:, key_rows]
    vg = v.reshape(B, rows, GRID_W, H, d)[:, key_rows]
    s = jnp.einsum('brqhd,brakhd->bhrqak', qg, kg,
                   preferred_element_type=jnp.float32) * (d ** -0.5)
    bias = rpb[:, dr[:, None, :, None], dc[None, :, None, :]]
    s = s + bias[None].astype(jnp.float32)
    s = jnp.where(col_in[None, None, None, :, None, :], s, -jnp.inf)
    p = jax.nn.softmax(s.reshape(B, H, rows, GRID_W, kr * GRID_W), axis=-1).reshape(s.shape)
    o = jnp.einsum('bhrqak,brakhd->brqhd', p.astype(v.dtype), vg)
    return o.reshape(B, S, H * d)


def mlstm_chunkwise(q, k, v, i_pre, f_pre):
    B, H, S, dk = q.shape
    dv = v.shape[-1]
    L = ML_CHUNK
    nc = S // L
    q = q * (dk ** -0.5)
    log_f = jax.nn.log_sigmoid(f_pre)

    def to_chunks(a):
        return jnp.moveaxis(a.reshape((B, H, nc, L) + a.shape[3:]), 2, 0)

    xs = (to_chunks(q), to_chunks(k), to_chunks(v), to_chunks(i_pre), to_chunks(log_f))
    lower = jnp.tril(jnp.ones((L, L), dtype=bool))

    def step(carry, inp):
        C, n, m = carry
        qb, kb, vb, ib, fb = inp
        b = jnp.cumsum(fb, axis=-1)
        dmat = b[..., :, None] - b[..., None, :] + ib[..., None, :]
        dmat = jnp.where(lower, dmat, -jnp.inf)
        m_inter = b + m[..., None]
        m_j = jnp.maximum(m_inter, jnp.max(dmat, axis=-1))
        w = jnp.exp(dmat - m_j[..., None])
        inter = jnp.exp(m_inter - m_j)
        qk = jnp.einsum('bhjd,bhsd->bhjs', qb, kb) * w
        num = inter[..., None] * jnp.einsum('bhjd,bhde->bhje', qb, C) + jnp.einsum('bhjs,bhse->bhje', qk, vb)
        den = inter * jnp.einsum('bhjd,bhd->bhj', qb, n) + jnp.sum(qk, axis=-1)
        h = num / jnp.maximum(jnp.abs(den), jnp.exp(-m_j))[..., None]
        b_last = b[..., -1]
        g = b_last[..., None] - b + ib
        m_new = jnp.maximum(b_last + m, jnp.max(g, axis=-1))
        decay = jnp.exp(b_last + m - m_new)
        wk = jnp.exp(g - m_new[..., None])
        C_new = decay[..., None, None] * C + jnp.einsum('bhs,bhsd,bhse->bhde', wk, kb, vb)
        n_new = decay[..., None] * n + jnp.einsum('bhs,bhsd->bhd', wk, kb)
        return (C_new, n_new, m_new), h

    init = (jnp.zeros((B, H, dk, dv), jnp.float32),
            jnp.zeros((B, H, dk), jnp.float32),
            jnp.zeros((B, H), jnp.float32))
    _, hs = lax.scan(step, init, xs)
    return jnp.moveaxis(hs, 0, 2).reshape(B, H, S, dv)


def mlstm_mixer(ml_q, ml_k, ml_v, ml_o, gates, b_gates, conv_w, conv_b, norm_g):
    B, S, _ = ml_q.shape
    H = ML_HEADS
    qk = jnp.concatenate([ml_q, ml_k], axis=-1)
    qk = lax.conv_general_dilated(qk, conv_w[:, None, :], (1,), [(CONV_W // 2, CONV_W // 2)],
                                  dimension_numbers=('NWC', 'WIO', 'NWC'),
                                  feature_group_count=2 * ML_QK_WIDTH) + conv_b
    qk = jax.nn.silu(qk)
    q, k = jnp.split(qk, 2, axis=-1)

    def heads(a, dh):
        return jnp.transpose(a.reshape(B, S, H, dh), (0, 2, 1, 3)).astype(jnp.float32)

    q, k, v = heads(q, ML_QK_DIM), heads(k, ML_QK_DIM), heads(ml_v, ML_V_DIM)
    g = gates.astype(jnp.float32) + b_gates.astype(jnp.float32)
    g = jnp.transpose(g.reshape(B, S, 4, H), (2, 0, 3, 1))
    h_fwd = mlstm_chunkwise(q, k, v, g[0], g[1])
    flip = lambda a: jnp.flip(a, axis=2)
    h_bwd = flip(mlstm_chunkwise(flip(q), flip(k), flip(v), flip(g[2]), flip(g[3])))
    h = h_fwd + h_bwd
    h = h * lax.rsqrt(jnp.mean(h * h, axis=-1, keepdims=True) + EPS)
    h = jnp.transpose(h, (0, 2, 1, 3)).reshape(B, S, ML_V_WIDTH) * norm_g.astype(jnp.float32)
    return (h * jax.nn.sigmoid(ml_o.astype(jnp.float32))).astype(ml_v.dtype)


def expert_choice_ffn(h, w_router, w_gate, w_up, w_down):
    B, S, _ = h.shape
    cap = CAPACITY_FACTOR * S // N_EXPERTS
    logits = jnp.einsum('bsd,de->bse', h, w_router, preferred_element_type=jnp.float32)
    aff = jax.nn.softmax(logits, axis=-1)
    gate, tok = lax.top_k(jnp.swapaxes(aff, 1, 2), cap)
    bidx = jnp.arange(B)[:, None, None]
    xin = h[bidx, tok]
    a = jnp.einsum('becd,edf->becf', xin, w_gate)
    u = jnp.einsum('becd,edf->becf', xin, w_up)
    y = jnp.einsum('becf,efd->becd', jax.nn.silu(a) * u, w_down)
    y = y * gate[..., None].astype(y.dtype)
    return jnp.zeros_like(h).at[bidx, tok].add(y)


def setup_inputs(seed: int = 0) -> dict:
    key = jax.random.key(seed)
    ks = jax.random.split(key, 16)
    f32 = jnp.float32

    def nrm(k, shape, scale):
        return jax.random.normal(k, shape, f32) * scale

    fb = jnp.linspace(3.0, 6.0, ML_HEADS, dtype=f32)
    zb = jnp.zeros((ML_HEADS,), f32)
    gate_base = jnp.concatenate([zb, fb, zb, fb])
    return {
        'x': nrm(ks[0], (BATCH, SEQ, D_MODEL), 1.0),
        'norm_mix_g': 1.0 + nrm(ks[1], (DEPTH, D_MODEL), 0.02),
        'w_in': nrm(ks[2], (DEPTH, D_MODEL, N_IN), D_MODEL ** -0.5),
        'b_gates': gate_base[None, :] + nrm(ks[3], (DEPTH, N_GATES), 0.1),
        'conv_w': nrm(ks[4], (DEPTH, CONV_W, 2 * ML_QK_WIDTH), CONV_W ** -0.5),
        'conv_b': nrm(ks[5], (DEPTH, 2 * ML_QK_WIDTH), 0.02),
        'na_rpb': nrm(ks[6], (DEPTH, NA_HEADS, 2 * NA_KR_MAX - 1, 2 * NA_KC - 1), 0.1),
        'na_norm_g': 1.0 + nrm(ks[7], (DEPTH, NA_WIDTH), 0.02),
        'ml_norm_g': 1.0 + nrm(ks[8], (DEPTH, ML_V_WIDTH), 0.02),
        'w_out': nrm(ks[9], (DEPTH, D_MIX, D_MODEL), D_MIX ** -0.5),
        'norm_ffn_g': 1.0 + nrm(ks[10], (DEPTH, D_MODEL), 0.02),
        'w_router': nrm(ks[11], (DEPTH, D_MODEL, N_EXPERTS), D_MODEL ** -0.5),
        'w_gate': nrm(ks[12], (DEPTH, N_EXPERTS, D_MODEL, D_EXPERT), D_MODEL ** -0.5),
        'w_up': nrm(ks[13], (DEPTH, N_EXPERTS, D_MODEL, D_EXPERT), D_MODEL ** -0.5),
        'w_down': nrm(ks[14], (DEPTH, N_EXPERTS, D_EXPERT, D_MODEL), D_EXPERT ** -0.5),
        'final_norm_g': 1.0 + nrm(ks[15], (D_MODEL,), 0.02),
    }


def reference(x, norm_mix_g, w_in, b_gates, conv_w, conv_b, na_rpb, na_norm_g, ml_norm_g,
              w_out, norm_ffn_g, w_router, w_gate, w_up, w_down, final_norm_g):
    for l in range(DEPTH):
        h = rmsnorm(x, norm_mix_g[l])
        proj = jnp.einsum('bsd,dn->bsn', h, w_in[l])
        na_q, na_k, na_v, ml_q, ml_k, ml_v, ml_o, gates = jnp.split(proj, IN_SPLITS, axis=-1)
        na_out = rmsnorm(neighbourhood_attention(na_q, na_k, na_v, na_rpb[l]), na_norm_g[l])
        ml_out = mlstm_mixer(ml_q, ml_k, ml_v, ml_o, gates, b_gates[l], conv_w[l], conv_b[l], ml_norm_g[l])
        mix = jnp.concatenate([na_out, ml_out], axis=-1)
        x = x + jnp.einsum('bsm,md->bsd', mix, w_out[l])
        h = rmsnorm(x, norm_ffn_g[l])
        x = x + expert_choice_ffn(h, w_router[l], w_gate[l], w_up[l], w_down[l])
    return rmsnorm(x, final_norm_g)
```

```python
import functools

import numpy as np
import jax
import jax.numpy as jnp
from jax import lax
from jax.experimental import pallas as pl
from jax.experimental.pallas import tpu as pltpu

F32 = jnp.float32
BF16 = jnp.bfloat16

EPS = 1e-6
GRID_W = 64
NA_HEADS = 8
NA_HEAD_DIM = 128
NA_KR = 8
NA_KC = 16
NA_WIDTH = NA_HEADS * NA_HEAD_DIM
ML_HEADS = 4
ML_QK_DIM = 128
ML_V_DIM = 256
ML_QK_WIDTH = ML_HEADS * ML_QK_DIM
ML_V_WIDTH = ML_HEADS * ML_V_DIM
CONV_W = 5
N_GATES = 4 * ML_HEADS
N_EXPERTS = 16
CAPACITY_FACTOR = 2
N_MAIN = 3 * NA_WIDTH + 2 * ML_QK_WIDTH + 2 * ML_V_WIDTH
LANES = 128
NEG = -1e30
ML_CHUNK = 256
VMEM_LIMIT = 56 * 1024 * 1024


def _cparams(sem):
    return pltpu.CompilerParams(dimension_semantics=sem, vmem_limit_bytes=VMEM_LIMIT)


def _in_proj_kernel(x_ref, g_ref, w_ref, wg_ref, o_ref, og_ref, h_scr):
    @pl.when(pl.program_id(1) == 0)
    def _():
        x = x_ref[...]
        ms = jnp.mean(x * x, axis=-1, keepdims=True)
        h = (x * lax.rsqrt(ms + EPS) * g_ref[...]).astype(BF16)
        h_scr[...] = h
        og_ref[...] = jnp.dot(h, wg_ref[...], preferred_element_type=F32)

    o_ref[...] = jnp.dot(h_scr[...], w_ref[...], preferred_element_type=F32).astype(o_ref.dtype)


def _in_proj(x2d, g, w_main, w_gates, tm=1024, tn=512):
    T, D = x2d.shape
    N = w_main.shape[1]
    return pl.pallas_call(
        _in_proj_kernel,
        grid=(T // tm, N // tn),
        in_specs=[
            pl.BlockSpec((tm, D), lambda i, j: (i, 0)),
            pl.BlockSpec((1, D), lambda i, j: (0, 0)),
            pl.BlockSpec((D, tn), lambda i, j: (0, j)),
            pl.BlockSpec((D, LANES), lambda i, j: (0, 0)),
        ],
        out_specs=[
            pl.BlockSpec((tm, tn), lambda i, j: (i, j)),
            pl.BlockSpec((tm, LANES), lambda i, j: (i, 0)),
        ],
        out_shape=[
            jax.ShapeDtypeStruct((T, N), BF16),
            jax.ShapeDtypeStruct((T, LANES), F32),
        ],
        scratch_shapes=[pltpu.VMEM((tm, D), BF16)],
        compiler_params=_cparams(("parallel", "arbitrary")),
        name="in_proj",
    )(x2d, g, w_main, w_gates)


def _na_bias_table(rpb):
    off = np.arange(NA_KR)[:, None]
    a = np.arange(NA_KR)[None, :]
    dr = a - off + (NA_KR - 1)
    c = np.arange(GRID_W)
    col_start = np.clip(c - NA_KC // 2, 0, GRID_W - NA_KC)
    col_in = (c[None, :] >= col_start[:, None]) & (c[None, :] < col_start[:, None] + NA_KC)
    dc = np.clip(c[None, :] - c[:, None] + (NA_KC - 1), 0, 2 * NA_KC - 2)
    tb = rpb[:, dr[:, None, :, None], dc[None, :, None, :]]
    tb = jnp.where(col_in[None, None, :, None, :], tb.astype(F32), NEG)
    return tb.reshape(rpb.shape[0], NA_KR, GRID_W, NA_KR * GRID_W)


def _na_kernel(q_ref, k_ref, v_ref, tb_ref, o_ref, *, rb, rows):
    i = pl.program_id(2)
    scale = NA_HEAD_DIM ** -0.5
    win = NA_KR * GRID_W
    for rr in range(rb):
        r = i * rb + rr
        rs = jnp.clip(r - NA_KR // 2, 0, rows - NA_KR)
        off = r - rs
        start = pl.multiple_of(rs * GRID_W, GRID_W)
        q = q_ref[rr * GRID_W:(rr + 1) * GRID_W, :]
        kw = k_ref[pl.ds(start, win), :]
        vw = v_ref[pl.ds(start, win), :]
        s = lax.dot_general(q, kw, (((1,), (1,)), ((), ())), preferred_element_type=F32) * scale
        s = s + tb_ref[off]
        m = jnp.max(s, axis=-1, keepdims=True)
        p = jnp.exp(s - m)
        l = jnp.sum(p, axis=-1, keepdims=True)
        o = jnp.dot(p.astype(BF16), vw, preferred_element_type=F32)
        o_ref[rr * GRID_W:(rr + 1) * GRID_W, :] = (o / l).astype(o_ref.dtype)


def _na_attention(proj, tb, B, S, rb=16):
    T = B * S
    rows = S // GRID_W
    assert rows >= NA_KR and rows % rb == 0
    nblk = rows // rb
    tq = rb * GRID_W
    kern = functools.partial(_na_kernel, rb=rb, rows=rows)
    return pl.pallas_call(
        kern,
        grid=(B, NA_HEADS, nblk),
        in_specs=[
            pl.BlockSpec((tq, NA_HEAD_DIM), lambda b, h, i: (b * nblk + i, h)),
            pl.BlockSpec((S, NA_HEAD_DIM), lambda b, h, i: (b, NA_HEADS + h)),
            pl.BlockSpec((S, NA_HEAD_DIM), lambda b, h, i: (b, 2 * NA_HEADS + h)),
            pl.BlockSpec((None, NA_KR, GRID_W, NA_KR * GRID_W), lambda b, h, i: (h, 0, 0, 0)),
        ],
        out_specs=pl.BlockSpec((tq, NA_HEAD_DIM), lambda b, h, i: (b * nblk + i, h)),
        out_shape=jax.ShapeDtypeStruct((T, NA_WIDTH), BF16),
        compiler_params=_cparams(("parallel", "parallel", "arbitrary")),
        name="na_attention",
    )(proj, proj, proj, tb)


def _log_sigmoid(x):
    return jnp.minimum(x, 0.0) - jnp.log(1.0 + jnp.exp(-jnp.abs(x)))


def _sigmoid(x):
    return 1.0 / (1.0 + jnp.exp(-x))


def _mlstm_kernel(q_ref, k_ref, v_ref, og_ref, g_ref, bg_ref, cwq_ref, cwk_ref, cbq_ref, cbk_ref,
                  ng_ref, out_ref, qs, ks, rrow, ccol, hf, hb, cst, *, S, L):
    nc = S // L
    halo = 8

    def conv_silu(src_ref, cw_ref, cb_ref, dst_ref, scale):
        cw = cw_ref[...]
        cbv = cb_ref[...]

        def body(c, carry):
            base = pl.multiple_of(c * L, L)
            lo = jnp.maximum(base - halo, 0)
            hi = jnp.minimum(base + L, S - halo)
            prev = src_ref[pl.ds(pl.multiple_of(lo, halo), halo), :].astype(F32)
            nxt = src_ref[pl.ds(pl.multiple_of(hi, halo), halo), :].astype(F32)
            prev = jnp.where(c > 0, prev, 0.0)
            nxt = jnp.where(c < nc - 1, nxt, 0.0)
            main = src_ref[pl.ds(base, L), :].astype(F32)
            blk = jnp.concatenate([prev, main, nxt], axis=0)
            n = L + 2 * halo
            acc = jnp.zeros((L, LANES), F32) + cbv
            for w in range(CONV_W):
                sh = (CONV_W // 2 - w) % n
                xs = blk if sh == 0 else pltpu.roll(blk, sh, 0)
                acc = acc + xs[halo:halo + L, :] * cw[w:w + 1, :]
            y = acc * _sigmoid(acc) * scale
            dst_ref[pl.ds(base, L), :] = y.astype(BF16)
            return carry

        lax.fori_loop(0, nc, body, 0)

    conv_silu(q_ref, cwq_ref, cbq_ref, qs, ML_QK_DIM ** -0.5)
    conv_silu(k_ref, cwk_ref, cbk_ref, ks, 1.0)

    t_idx = lax.broadcasted_iota(jnp.int32, (L, L), 0)
    j_idx = lax.broadcasted_iota(jnp.int32, (L, L), 1)
    upper = (t_idx <= j_idx).astype(F32)
    lower = (t_idx >= j_idx).astype(F32)
    row_id = lax.broadcasted_iota(jnp.int32, (8, L), 0)

    def gate_body(c, carry):
        base = pl.multiple_of(c * L, L)
        g = g_ref[:, pl.ds(base, L)] + bg_ref[...]
        lf = _log_sigmoid(g)
        pre = jnp.dot(lf, upper, preferred_element_type=F32, precision=lax.Precision.HIGHEST)
        suf = jnp.dot(lf, lower, preferred_element_type=F32, precision=lax.Precision.HIGHEST)
        r = jnp.where(row_id == 1, pre, jnp.where(row_id == 3, suf, g))
        rrow[:, pl.ds(base, L)] = r
        r128 = jnp.concatenate([r, jnp.zeros((LANES - 8, L), F32)], axis=0)
        ccol[pl.ds(base, L), :] = r128.T
        return carry

    lax.fori_loop(0, nc, gate_body, 0)

    fwd_mask = j_idx <= t_idx
    bwd_mask = j_idx >= t_idx

    def chunk(c, irow, mask, total_lane, C_ref, n_st, m_st, h_ref):
        base = pl.multiple_of(c * L, L)
        qc = qs[pl.ds(base, L), :]
        kc = ks[pl.ds(base, L), :]
        vc = v_ref[pl.ds(base, L), :]
        cols = ccol[pl.ds(base, L), :]
        rws = rrow[:, pl.ds(base, L)]
        i_col = cols[:, irow:irow + 1]
        cum_col = cols[:, irow + 1:irow + 2]
        i_row = rws[irow:irow + 1, :]
        cum_row = rws[irow + 1:irow + 2, :]
        total = cum_row[:, total_lane:total_lane + 1]
        C = C_ref[...]

        dmat = jnp.where(mask, cum_col - cum_row + i_row, NEG)
        m_inter = cum_col + m_st
        m_j = jnp.maximum(m_inter, jnp.max(dmat, axis=-1, keepdims=True))
        w = jnp.exp(dmat - m_j)
        inter = jnp.exp(m_inter - m_j)
        s = lax.dot_general(qc, kc, (((1,), (1,)), ((), ())), preferred_element_type=F32)
        qkw = s * w
        num = inter * jnp.dot(qc, C.astype(BF16), preferred_element_type=F32) \
            + jnp.dot(qkw.astype(BF16), vc, preferred_element_type=F32)
        den = inter * jnp.sum(qc.astype(F32) * n_st, axis=-1, keepdims=True) \
            + jnp.sum(qkw, axis=-1, keepdims=True)
        h = num / jnp.maximum(jnp.abs(den), jnp.exp(-m_j))
        h_ref[pl.ds(base, L), :] = h

        g_col = total - cum_col + i_col
        m_new = jnp.maximum(total + m_st, jnp.max(g_col, axis=0, keepdims=True))
        wk = jnp.exp(g_col - m_new)
        decay = jnp.exp(total + m_st - m_new)
        vw = (wk * vc.astype(F32)).astype(BF16)
        dC = lax.dot_general(kc, vw, (((0,), (0,)), ((), ())), preferred_element_type=F32)
        C_ref[...] = decay * C + dC
        n_new = decay * n_st + jnp.sum(wk * kc.astype(F32), axis=0, keepdims=True)
        return n_new, m_new

    cst[...] = jnp.zeros(cst.shape, F32)

    def scan_body(t, carry):
        n_f, m_f, n_b, m_b = carry
        n_f, m_f = chunk(t, 0, fwd_mask, L - 1, cst.at[0], n_f, m_f, hf)
        n_b, m_b = chunk(nc - 1 - t, 2, bwd_mask, 0, cst.at[1], n_b, m_b, hb)
        return n_f, m_f, n_b, m_b

    z_n = jnp.zeros((1, ML_QK_DIM), F32)
    z_m = jnp.zeros((1, 1), F32)
    lax.fori_loop(0, nc, scan_body, (z_n, z_m, z_n, z_m))

    ng = ng_ref[...]

    def fin_body(c, carry):
        base = pl.multiple_of(c * L, L)
        h = hf[pl.ds(base, L), :] + hb[pl.ds(base, L), :]
        ms = jnp.mean(h * h, axis=-1, keepdims=True)
        y = h * lax.rsqrt(ms + EPS) * ng
        y = y * _sigmoid(og_ref[pl.ds(base, L), :].astype(F32))
        out_ref[pl.ds(base, L), :] = y.astype(out_ref.dtype)
        return carry

    lax.fori_loop(0, nc, fin_body, 0)


def _mlstm(proj, gates_rows, bg_rows, conv_w, conv_b, norm_g, B, S):
    T = B * S
    L = min(ML_CHUNK, S)
    assert S % L == 0
    q0 = 3 * NA_WIDTH // ML_QK_DIM
    k0 = q0 + ML_HEADS
    v0 = (3 * NA_WIDTH + 2 * ML_QK_WIDTH) // ML_V_DIM
    o0 = v0 + ML_HEADS
    kern = functools.partial(_mlstm_kernel, S=S, L=L)
    return pl.pallas_call(
        kern,
        grid=(B, ML_HEADS),
        in_specs=[
            pl.BlockSpec((S, ML_QK_DIM), lambda b, h: (b, q0 + h)),
            pl.BlockSpec((S, ML_QK_DIM), lambda b, h: (b, k0 + h)),
            pl.BlockSpec((S, ML_V_DIM), lambda b, h: (b, v0 + h)),
            pl.BlockSpec((S, ML_V_DIM), lambda b, h: (b, o0 + h)),
            pl.BlockSpec((None, None, 8, S), lambda b, h: (b, h, 0, 0)),
            pl.BlockSpec((None, 8, 1), lambda b, h: (h, 0, 0)),
            pl.BlockSpec((CONV_W, ML_QK_DIM), lambda b, h: (0, h)),
            pl.BlockSpec((CONV_W, ML_QK_DIM), lambda b, h: (0, ML_HEADS + h)),
            pl.BlockSpec((1, ML_QK_DIM), lambda b, h: (0, h)),
            pl.BlockSpec((1, ML_QK_DIM), lambda b, h: (0, ML_HEADS + h)),
            pl.BlockSpec((1, ML_V_DIM), lambda b, h: (0, h)),
        ],
        out_specs=pl.BlockSpec((S, ML_V_DIM), lambda b, h: (b, h)),
        out_shape=jax.ShapeDtypeStruct((T, ML_V_WIDTH), BF16),
        scratch_shapes=[
            pltpu.VMEM((S, ML_QK_DIM), BF16),
            pltpu.VMEM((S, ML_QK_DIM), BF16),
            pltpu.VMEM((8, S), F32),
            pltpu.VMEM((S, LANES), F32),
            pltpu.VMEM((S, ML_V_DIM), F32),
            pltpu.VMEM((S, ML_V_DIM), F32),
            pltpu.VMEM((2, ML_QK_DIM, ML_V_DIM), F32),
        ],
        compiler_params=_cparams(("parallel", "arbitrary")),
        name="mlstm",
    )(proj, proj, proj, proj, gates_rows, bg_rows, conv_w, conv_w, conv_b, conv_b, norm_g)


def _out_proj_kernel(na_ref, ml_ref, x_ref, w_ref, nag_ref, fg_ref, wr_ref, xo_ref, h_ref, aff_ref):
    na = na_ref[...].astype(F32)
    ms = jnp.mean(na * na, axis=-1, keepdims=True)
    na_n = (na * lax.rsqrt(ms + EPS) * nag_ref[...]).astype(BF16)
    upd = jnp.dot(na_n, w_ref[0:NA_WIDTH, :], preferred_element_type=F32)
    upd = upd + jnp.dot(ml_ref[...], w_ref[NA_WIDTH:, :], preferred_element_type=F32)
    xn = x_ref[...] + upd
    xo_ref[...] = xn
    ms2 = jnp.mean(xn * xn, axis=-1, keepdims=True)
    h = (xn * lax.rsqrt(ms2 + EPS) * fg_ref[...]).astype(BF16)
    h_ref[...] = h
    logits = jnp.dot(h, wr_ref[...], preferred_element_type=F32)
    lane = lax.broadcasted_iota(jnp.int32, logits.shape, 1)
    logits = jnp.where(lane < N_EXPERTS, logits, NEG)
    m = jnp.max(logits, axis=-1, keepdims=True)
    e = jnp.exp(logits - m)
    aff_ref[...] = e / jnp.sum(e, axis=-1, keepdims=True)


def _out_proj(na_o, ml_o, x2d, w_out, na_g, ffn_g, w_router, tm=512):
    T, D = x2d.shape
    return pl.pallas_call(
        _out_proj_kernel,
        grid=(T // tm,),
        in_specs=[
            pl.BlockSpec((tm, NA_WIDTH), lambda i: (i, 0)),
            pl.BlockSpec((tm, ML_V_WIDTH), lambda i: (i, 0)),
            pl.BlockSpec((tm, D), lambda i: (i, 0)),
            pl.BlockSpec((NA_WIDTH + ML_V_WIDTH, D), lambda i: (0, 0)),
            pl.BlockSpec((1, NA_WIDTH), lambda i: (0, 0)),
            pl.BlockSpec((1, D), lambda i: (0, 0)),
            pl.BlockSpec((D, LANES), lambda i: (0, 0)),
        ],
        out_specs=[
            pl.BlockSpec((tm, D), lambda i: (i, 0)),
            pl.BlockSpec((tm, D), lambda i: (i, 0)),
            pl.BlockSpec((tm, LANES), lambda i: (i, 0)),
        ],
        out_shape=[
            jax.ShapeDtypeStruct((T, D), F32),
            jax.ShapeDtypeStruct((T, D), BF16),
            jax.ShapeDtypeStruct((T, LANES), F32),
        ],
        compiler_params=_cparams(("parallel",)),
        name="out_proj",
    )(na_o, ml_o, x2d, w_out, na_g, ffn_g, w_router)


def _ffn_kernel(x_ref, wg_ref, wu_ref, wd_ref, gate_ref, y_ref):
    f = pl.program_id(2)
    x = x_ref[...]
    a = jnp.dot(x, wg_ref[...].astype(BF16), preferred_element_type=F32)
    u = jnp.dot(x, wu_ref[...].astype(BF16), preferred_element_type=F32)
    act = (a * _sigmoid(a) * u).astype(BF16)
    y = jnp.dot(act, wd_ref[...].astype(BF16), preferred_element_type=F32)

    @pl.when(f == 0)
    def _():
        y_ref[...] = y

    @pl.when(f > 0)
    def _():
        y_ref[...] += y

    @pl.when(f == pl.num_programs(2) - 1)
    def _():
        y_ref[...] = y_ref[...] * gate_ref[...]


def _expert_ffn(xin, w_gate, w_up, w_down, gate, tm=1024, tf=256):
    E, M, D = xin.shape
    F = w_gate.shape[-1]
    tm = min(tm, M)
    return pl.pallas_call(
        _ffn_kernel,
        grid=(E, M // tm, F // tf),
        in_specs=[
            pl.BlockSpec((None, tm, D), lambda e, m, f: (e, m, 0)),
            pl.BlockSpec((None, D, tf), lambda e, m, f: (e, 0, f)),
            pl.BlockSpec((None, D, tf), lambda e, m, f: (e, 0, f)),
            pl.BlockSpec((None, tf, D), lambda e, m, f: (e, f, 0)),
            pl.BlockSpec((None, tm, 1), lambda e, m, f: (e, m, 0)),
        ],
        out_specs=pl.BlockSpec((None, tm, D), lambda e, m, f: (e, m, 0)),
        out_shape=jax.ShapeDtypeStruct((E, M, D), F32),
        compiler_params=_cparams(("parallel", "parallel", "arbitrary")),
        name="expert_ffn",
    )(xin, w_gate, w_up, w_down, gate)


def _final_norm_kernel(x_ref, g_ref, o_ref):
    x = x_ref[...]
    ms = jnp.mean(x * x, axis=-1, keepdims=True)
    o_ref[...] = x * lax.rsqrt(ms + EPS) * g_ref[...]


def _final_norm(x2d, g, tm=1024):
    T, D = x2d.shape
    return pl.pallas_call(
        _final_norm_kernel,
        grid=(T // tm,),
        in_specs=[pl.BlockSpec((tm, D), lambda i: (i, 0)), pl.BlockSpec((1, D), lambda i: (0, 0))],
        out_specs=pl.BlockSpec((tm, D), lambda i: (i, 0)),
        out_shape=jax.ShapeDtypeStruct((T, D), F32),
        compiler_params=_cparams(("parallel",)),
        name="final_norm",
    )(x2d, g)


def kernel(x, norm_mix_g, w_in, b_gates, conv_w, conv_b, na_rpb, na_norm_g, ml_norm_g, w_out,
           norm_ffn_g, w_router, w_gate, w_up, w_down, final_norm_g):
    B, S, D = x.shape
    T = B * S
    depth = w_in.shape[0]
    cap = CAPACITY_FACTOR * S // N_EXPERTS
    x2d = x.reshape(T, D)
    bidx = jnp.arange(B)[:, None, None]

    for l in range(depth):
        w_main = w_in[l, :, :N_MAIN].astype(BF16)
        w_gates = jnp.pad(w_in[l, :, N_MAIN:], ((0, 0), (0, LANES - N_GATES))).astype(BF16)
        proj, gates = _in_proj(x2d, norm_mix_g[l][None, :], w_main, w_gates)

        tb = _na_bias_table(na_rpb[l])
        na_o = _na_attention(proj, tb, B, S)

        g4 = gates[:, :N_GATES].reshape(B, S, 4, ML_HEADS)
        g_rows = jnp.pad(jnp.transpose(g4, (0, 3, 2, 1)), ((0, 0), (0, 0), (0, 4), (0, 0)))
        bg_rows = jnp.pad(b_gates[l].reshape(4, ML_HEADS).T, ((0, 0), (0, 4)))[:, :, None]
        ml_o = _mlstm(proj, g_rows, bg_rows.astype(F32), conv_w[l], conv_b[l][None, :],
                      ml_norm_g[l][None, :], B, S)

        w_r = jnp.pad(w_router[l], ((0, 0), (0, LANES - N_EXPERTS))).astype(BF16)
        x2d, h2, aff = _out_proj(na_o, ml_o, x2d, w_out[l].astype(BF16), na_norm_g[l][None, :],
                                 norm_ffn_g[l][None, :], w_r)

        aff_t = jnp.swapaxes(aff[:, :N_EXPERTS].reshape(B, S, N_EXPERTS), 1, 2)
        gate, tok = lax.top_k(aff_t, cap)
        xin = h2.reshape(B, S, D)[bidx, tok]
        xin = jnp.swapaxes(xin, 0, 1).reshape(N_EXPERTS, B * cap, D)
        gcol = jnp.swapaxes(gate, 0, 1).reshape(N_EXPERTS, B * cap, 1)
        y = _expert_ffn(xin, w_gate[l], w_up[l], w_down[l], gcol)
        y = jnp.swapaxes(y.reshape(N_EXPERTS, B, cap, D), 0, 1)
        x2d = (x2d.reshape(B, S, D).at[bidx, tok].add(y)).reshape(T, D)

    return _final_norm(x2d, final_norm_g[None, :]).reshape(B, S, D)
```

```python
import functools

import numpy as np
import jax
import jax.numpy as jnp
from jax import lax
from jax.experimental import pallas as pl
from jax.experimental.pallas import tpu as pltpu

F32 = jnp.float32
BF16 = jnp.bfloat16

EPS = 1e-6
GRID_W = 64
NA_HEADS = 8
NA_HEAD_DIM = 128
NA_KR = 8
NA_KC = 16
NA_WIDTH = NA_HEADS * NA_HEAD_DIM
NA_LOOKAHEAD = 4
ML_HEADS = 4
ML_QK_DIM = 128
ML_V_DIM = 256
ML_QK_WIDTH = ML_HEADS * ML_QK_DIM
ML_V_WIDTH = ML_HEADS * ML_V_DIM
CONV_W = 5
N_GATES = 4 * ML_HEADS
N_EXPERTS = 16
CAPACITY_FACTOR = 2
N_MAIN = 3 * NA_WIDTH + 2 * ML_QK_WIDTH + 2 * ML_V_WIDTH
LANES = 128
NEG = -1e30
ML_CHUNK = 256
VMEM_LIMIT = 56 * 1024 * 1024


def _cparams(sem):
    return pltpu.CompilerParams(dimension_semantics=sem, vmem_limit_bytes=VMEM_LIMIT)


def _in_proj_kernel(x_ref, g_ref, w_ref, wg_ref, o_ref, og_ref, h_scr):
    @pl.when(pl.program_id(1) == 0)
    def _():
        x = x_ref[...]
        ms = jnp.mean(x * x, axis=-1, keepdims=True)
        h = (x * lax.rsqrt(ms + EPS) * g_ref[...]).astype(BF16)
        h_scr[...] = h
        og_ref[...] = jnp.dot(h, wg_ref[...], preferred_element_type=F32)

    o_ref[...] = jnp.dot(h_scr[...], w_ref[...], preferred_element_type=F32).astype(o_ref.dtype)


def _in_proj(x2d, g, w_main, w_gates, tm=1024, tn=512):
    T, D = x2d.shape
    N = w_main.shape[1]
    return pl.pallas_call(
        _in_proj_kernel,
        grid=(T // tm, N // tn),
        in_specs=[
            pl.BlockSpec((tm, D), lambda i, j: (i, 0)),
            pl.BlockSpec((1, D), lambda i, j: (0, 0)),
            pl.BlockSpec((D, tn), lambda i, j: (0, j)),
            pl.BlockSpec((D, LANES), lambda i, j: (0, 0)),
        ],
        out_specs=[
            pl.BlockSpec((tm, tn), lambda i, j: (i, j)),
            pl.BlockSpec((tm, LANES), lambda i, j: (i, 0)),
        ],
        out_shape=[
            jax.ShapeDtypeStruct((T, N), BF16),
            jax.ShapeDtypeStruct((T, LANES), F32),
        ],
        scratch_shapes=[pltpu.VMEM((tm, D), BF16)],
        compiler_params=_cparams(("parallel", "arbitrary")),
        name="in_proj",
    )(x2d, g, w_main, w_gates)


def _na_bias_table(rpb):
    n_dr, n_dc = 2 * NA_KR - 1, 2 * NA_KC - 1
    off = np.arange(NA_KR)[:, None]
    a = np.arange(NA_KR)[None, :]
    dr = a - off + (NA_KR - 1)
    c = np.arange(GRID_W)
    col_start = np.clip(c - NA_KC // 2, 0, GRID_W - NA_KC)
    col_in = (c[None, :] >= col_start[:, None]) & (c[None, :] < col_start[:, None] + NA_KC)
    dc = np.clip(c[None, :] - c[:, None] + (NA_KC - 1), 0, n_dc - 1)
    sel_r = jnp.asarray(dr[:, :, None] == np.arange(n_dr), F32)
    sel_c = jnp.asarray(dc[:, :, None] == np.arange(n_dc), F32)
    tb = jnp.einsum("oai,hij,qkj->hoqak", sel_r, rpb.astype(F32), sel_c,
                    precision=lax.Precision.HIGHEST)
    tb = jnp.where(col_in[None, None, :, None, :], tb, NEG)
    return tb.reshape(rpb.shape[0], NA_KR, GRID_W, NA_KR * GRID_W)


def _na_kernel(q_ref, k_ref, v_ref, tb_ref, o_ref, *, rb, rows):
    i = pl.program_id(2)
    scale = NA_HEAD_DIM ** -0.5
    win = NA_KR * GRID_W

    def scores(rr):
        r = i * rb + rr
        rs = jnp.clip(r - NA_KR // 2, 0, rows - NA_KR)
        start = pl.multiple_of(rs * GRID_W, GRID_W)
        q = q_ref[rr * GRID_W:(rr + 1) * GRID_W, :]
        kw = k_ref[pl.ds(start, win), :]
        s = lax.dot_general(q, kw, (((1,), (1,)), ((), ())), preferred_element_type=F32)
        return s, r - rs, start

    queue = [scores(rr) for rr in range(min(NA_LOOKAHEAD, rb))]
    for rr in range(rb):
        s, off, start = queue.pop(0)
        if rr + NA_LOOKAHEAD < rb:
            queue.append(scores(rr + NA_LOOKAHEAD))
        vw = v_ref[pl.ds(start, win), :]
        s = s * scale + tb_ref[off]
        m = jnp.max(s, axis=-1, keepdims=True)
        p = jnp.exp(s - m)
        l = jnp.sum(p, axis=-1, keepdims=True)
        o = jnp.dot(p.astype(BF16), vw, preferred_element_type=F32)
        o_ref[rr * GRID_W:(rr + 1) * GRID_W, :] = (o / l).astype(o_ref.dtype)


def _na_attention(proj, tb, B, S, rb=16):
    T = B * S
    rows = S // GRID_W
    assert rows >= NA_KR and rows % rb == 0
    nblk = rows // rb
    tq = rb * GRID_W
    kern = functools.partial(_na_kernel, rb=rb, rows=rows)
    return pl.pallas_call(
        kern,
        grid=(B, NA_HEADS, nblk),
        in_specs=[
            pl.BlockSpec((tq, NA_HEAD_DIM), lambda b, h, i: (b * nblk + i, h)),
            pl.BlockSpec((S, NA_HEAD_DIM), lambda b, h, i: (b, NA_HEADS + h)),
            pl.BlockSpec((S, NA_HEAD_DIM), lambda b, h, i: (b, 2 * NA_HEADS + h)),
            pl.BlockSpec((None, NA_KR, GRID_W, NA_KR * GRID_W), lambda b, h, i: (h, 0, 0, 0)),
        ],
        out_specs=pl.BlockSpec((tq, NA_HEAD_DIM), lambda b, h, i: (b * nblk + i, h)),
        out_shape=jax.ShapeDtypeStruct((T, NA_WIDTH), BF16),
        compiler_params=_cparams(("parallel", "parallel", "arbitrary")),
        name="na_attention",
    )(proj, proj, proj, tb)


def _log_sigmoid(x):
    return jnp.minimum(x, 0.0) - jnp.log(1.0 + jnp.exp(-jnp.abs(x)))


def _sigmoid(x):
    return 1.0 / (1.0 + jnp.exp(-x))


def _mlstm_kernel(q_ref, k_ref, v_ref, og_ref, g_ref, bg_ref, cwq_ref, cwk_ref, cbq_ref, cbk_ref,
                  ng_ref, out_ref, qs, kst, rrow, bcol, hf, hb, cst, nst, *, S, L):
    nc = S // L
    halo = 8

    def conv_silu(src_ref, cw_ref, cb_ref, dst_ref, transposed, scale):
        cw = cw_ref[...]
        cbv = cb_ref[...]

        def body(c, carry):
            base = pl.multiple_of(c * L, L)
            lo = jnp.maximum(base - halo, 0)
            hi = jnp.minimum(base + L, S - halo)
            prev = src_ref[pl.ds(pl.multiple_of(lo, halo), halo), :].astype(F32)
            nxt = src_ref[pl.ds(pl.multiple_of(hi, halo), halo), :].astype(F32)
            prev = jnp.where(c > 0, prev, 0.0)
            nxt = jnp.where(c < nc - 1, nxt, 0.0)
            main = src_ref[pl.ds(base, L), :].astype(F32)
            blk = jnp.concatenate([prev, main, nxt], axis=0)
            n = L + 2 * halo
            acc = jnp.zeros((L, LANES), F32) + cbv
            for w in range(CONV_W):
                sh = (CONV_W // 2 - w) % n
                xs = blk if sh == 0 else pltpu.roll(blk, sh, 0)
                acc = acc + xs[halo:halo + L, :] * cw[w:w + 1, :]
            y = acc * _sigmoid(acc) * scale
            if transposed:
                dst_ref[:, pl.ds(base, L)] = y.T.astype(BF16)
            else:
                dst_ref[pl.ds(base, L), :] = y.astype(BF16)
            return carry

        lax.fori_loop(0, nc, body, 0)

    conv_silu(q_ref, cwq_ref, cbq_ref, qs, False, ML_QK_DIM ** -0.5)
    conv_silu(k_ref, cwk_ref, cbk_ref, kst, True, 1.0)

    t_idx = lax.broadcasted_iota(jnp.int32, (L, L), 0)
    j_idx = lax.broadcasted_iota(jnp.int32, (L, L), 1)
    upper = (t_idx <= j_idx).astype(BF16)
    lower = (t_idx >= j_idx).astype(BF16)

    def split3(x):
        hi = x.astype(BF16).astype(F32)
        mid = (x - hi).astype(BF16).astype(F32)
        lo = (x - hi - mid).astype(BF16).astype(F32)
        return hi, mid, lo

    def gate_body(c, carry):
        base = pl.multiple_of(c * L, L)
        g = g_ref[:, pl.ds(base, L)] + bg_ref[...]
        hi, mid, lo = split3(_log_sigmoid(g))
        lf3 = jnp.concatenate([hi, mid, lo, jnp.zeros_like(hi)], axis=0).astype(BF16)
        pre = jnp.dot(lf3, upper, preferred_element_type=F32)
        suf = jnp.dot(lf3, lower, preferred_element_type=F32)
        pre = pre[0:8] + pre[8:16] + pre[16:24]
        suf = suf[0:8] + suf[8:16] + suf[16:24]
        cum_f, cum_b = pre[1:2], suf[3:4]
        gx_f = cum_f[:, L - 1:L] - cum_f + g[0:1]
        gx_b = cum_b[:, 0:1] - cum_b + g[2:3]
        r = jnp.concatenate([g[0:1], cum_f, g[2:3], cum_b, gx_f, gx_b, g[0:1] - cum_f, g[2:3] - cum_b],
                            axis=0)
        rrow[:, pl.ds(base, L)] = r
        return carry

    lax.fori_loop(0, nc, gate_body, 0, unroll=4)

    r_all = rrow[...]
    row_all = lax.broadcasted_iota(jnp.int32, (8, S), 0)
    lane_all = lax.broadcasted_iota(jnp.int32, (8, S), 1) & (L - 1)
    pm = r_all
    sm = r_all
    sh = 1
    while sh < L:
        pm = jnp.maximum(pm, jnp.where(lane_all >= sh, pltpu.roll(pm, sh, 1), NEG))
        sm = jnp.maximum(sm, jnp.where(lane_all < L - sh, pltpu.roll(sm, S - sh, 1), NEG))
        sh *= 2
    mi = jnp.where(row_all == 6, pm + pltpu.roll(r_all, 5, 0), sm + pltpu.roll(r_all, 4, 0))
    rrow[...] = jnp.where(row_all >= 6, mi, r_all)

    pick_r = lax.broadcasted_iota(jnp.int32, (LANES, 2 * LANES), 0)
    pick_l = lax.broadcasted_iota(jnp.int32, (LANES, 2 * LANES), 1)

    def picker(k_lo, k_hi):
        want = jnp.where(pick_l < LANES, k_lo, k_hi)
        return ((pick_r < 24) & ((pick_r & 7) == want)).astype(BF16)

    pairs = ((1, 6, 0, 1), (4, 3, 2, 3), (7, 5, 4, 5))

    def bcol_body(c, carry):
        base = pl.multiple_of(c * L, L)
        hi, mid, lo = split3(rrow[:, pl.ds(base, L)])
        x = jnp.concatenate([hi, mid, lo, jnp.zeros((LANES - 24, L), F32)], axis=0)
        xt = x.T.astype(BF16)
        for k_lo, k_hi, d_lo, d_hi in pairs:
            y = jnp.dot(xt, picker(k_lo, k_hi), preferred_element_type=F32)
            bcol[d_lo, pl.ds(base, L), :] = y[:, :LANES]
            bcol[d_hi, pl.ds(base, L), :] = y[:, LANES:]
        return carry

    lax.fori_loop(0, nc, bcol_body, 0, unroll=4)

    fwd_mask = j_idx <= t_idx
    bwd_mask = j_idx >= t_idx
    ones_b = jnp.ones((L, LANES), BF16)

    def wide(x, n):
        return jnp.concatenate([x] * n, axis=1)

    def load(c, d, total_row):
        base = pl.multiple_of(c * L, L)
        qc = qs[pl.ds(base, L), :]
        C = cst[d]
        cum = bcol[3 * d, pl.ds(base, L), :]
        return dict(
            base=base, qc=qc, vc=v_ref[pl.ds(base, L), :], kt=kst[:, pl.ds(base, L)],
            cum=cum, mi=bcol[3 * d + 1, pl.ds(base, L), :], gx=bcol[3 * d + 2, pl.ds(base, L), :],
            i_row=rrow[2 * d:2 * d + 1, pl.ds(base, L)],
            cum_row=rrow[2 * d + 1:2 * d + 2, pl.ds(base, L)],
            total=cum[total_row:total_row + 1, :], C=C,
            s=jnp.dot(qc, kst[:, pl.ds(base, L)], preferred_element_type=F32),
            qC=jnp.dot(qc, C.astype(BF16), preferred_element_type=F32),
            qn=jnp.dot(qc, nst[d].astype(BF16), preferred_element_type=F32))

    def weights(x, mask, m_st):
        m_inter = x["cum"] + m_st
        m_j = jnp.maximum(m_inter, x["mi"])
        dmat = jnp.where(mask, wide(x["cum"], L // LANES) - x["cum_row"] + x["i_row"], NEG)
        qkw = (x["s"] * jnp.exp(dmat - wide(m_j, L // LANES))).astype(BF16)
        inter = jnp.exp(m_inter - m_j)
        m_new = jnp.maximum(x["total"] + m_st, jnp.max(x["gx"], axis=0, keepdims=True))
        wk = jnp.exp(x["gx"] - m_new)
        decay = jnp.exp(x["total"] + m_st - m_new)
        vw = (wide(wk, ML_V_DIM // LANES) * x["vc"].astype(F32)).astype(BF16)
        return qkw, inter, m_j, wk.astype(BF16), vw, decay, m_new

    def finish(x, d, qkw, inter, m_j, wk, vw, decay, h_ref):
        base = x["base"]
        den = inter * x["qn"] + jnp.dot(qkw, ones_b, preferred_element_type=F32)
        rden = 1.0 / jnp.maximum(jnp.abs(den), jnp.exp(-m_j))
        num = wide(inter, ML_V_DIM // LANES) * x["qC"] + jnp.dot(qkw, x["vc"], preferred_element_type=F32)
        h_ref[pl.ds(base, L), :] = num * wide(rden, ML_V_DIM // LANES)
        cst[d] = wide(decay, ML_V_DIM // LANES) * x["C"] + jnp.dot(x["kt"], vw, preferred_element_type=F32)
        nst[d] = decay * nst[d] + jnp.dot(x["kt"], wk, preferred_element_type=F32)

    cst[...] = jnp.zeros(cst.shape, F32)
    nst[...] = jnp.zeros(nst.shape, F32)

    def scan_body(t, carry):
        m_f, m_b = carry
        xf = load(t, 0, L - 1)
        xb = load(nc - 1 - t, 1, 0)
        wf = weights(xf, fwd_mask, m_f)
        finish(xf, 0, *wf[:-1], hf)
        wb = weights(xb, bwd_mask, m_b)
        finish(xb, 1, *wb[:-1], hb)
        return wf[-1], wb[-1]

    z_m = jnp.zeros((1, LANES), F32)
    lax.fori_loop(0, nc, scan_body, (z_m, z_m))

    ng = ng_ref[...]

    def fin_body(c, carry):
        base = pl.multiple_of(c * L, L)
        h = hf[pl.ds(base, L), :] + hb[pl.ds(base, L), :]
        ms = jnp.mean(h * h, axis=-1, keepdims=True)
        y = h * lax.rsqrt(ms + EPS) * ng
        y = y * _sigmoid(og_ref[pl.ds(base, L), :].astype(F32))
        out_ref[pl.ds(base, L), :] = y.astype(out_ref.dtype)
        return carry

    lax.fori_loop(0, nc, fin_body, 0)


def _mlstm(proj, gates_rows, bg_rows, conv_w, conv_b, norm_g, B, S):
    T = B * S
    L = min(ML_CHUNK, S)
    assert S % L == 0 and L % LANES == 0
    q0 = 3 * NA_WIDTH // ML_QK_DIM
    k0 = q0 + ML_HEADS
    v0 = (3 * NA_WIDTH + 2 * ML_QK_WIDTH) // ML_V_DIM
    o0 = v0 + ML_HEADS
    kern = functools.partial(_mlstm_kernel, S=S, L=L)
    return pl.pallas_call(
        kern,
        grid=(B, ML_HEADS),
        in_specs=[
            pl.BlockSpec((S, ML_QK_DIM), lambda b, h: (b, q0 + h)),
            pl.BlockSpec((S, ML_QK_DIM), lambda b, h: (b, k0 + h)),
            pl.BlockSpec((S, ML_V_DIM), lambda b, h: (b, v0 + h)),
            pl.BlockSpec((S, ML_V_DIM), lambda b, h: (b, o0 + h)),
            pl.BlockSpec((None, None, 8, S), lambda b, h: (b, h, 0, 0)),
            pl.BlockSpec((None, 8, 1), lambda b, h: (h, 0, 0)),
            pl.BlockSpec((CONV_W, ML_QK_DIM), lambda b, h: (0, h)),
            pl.BlockSpec((CONV_W, ML_QK_DIM), lambda b, h: (0, ML_HEADS + h)),
            pl.BlockSpec((1, ML_QK_DIM), lambda b, h: (0, h)),
            pl.BlockSpec((1, ML_QK_DIM), lambda b, h: (0, ML_HEADS + h)),
            pl.BlockSpec((1, ML_V_DIM), lambda b, h: (0, h)),
        ],
        out_specs=pl.BlockSpec((S, ML_V_DIM), lambda b, h: (b, h)),
        out_shape=jax.ShapeDtypeStruct((T, ML_V_WIDTH), BF16),
        scratch_shapes=[
            pltpu.VMEM((S, ML_QK_DIM), BF16),
            pltpu.VMEM((ML_QK_DIM, S), BF16),
            pltpu.VMEM((8, S), F32),
            pltpu.VMEM((6, S, LANES), F32),
            pltpu.VMEM((S, ML_V_DIM), F32),
            pltpu.VMEM((S, ML_V_DIM), F32),
            pltpu.VMEM((2, ML_QK_DIM, ML_V_DIM), F32),
            pltpu.VMEM((2, ML_QK_DIM, LANES), F32),
        ],
        compiler_params=_cparams(("parallel", "arbitrary")),
        name="mlstm",
    )(proj, proj, proj, proj, gates_rows, bg_rows, conv_w, conv_w, conv_b, conv_b, norm_g)


def _out_proj_kernel(na_ref, ml_ref, x_ref, w_ref, nag_ref, fg_ref, wr_ref, xo_ref, h_ref, aff_ref):
    na = na_ref[...].astype(F32)
    ms = jnp.mean(na * na, axis=-1, keepdims=True)
    na_n = (na * lax.rsqrt(ms + EPS) * nag_ref[...]).astype(BF16)
    upd = jnp.dot(na_n, w_ref[0:NA_WIDTH, :], preferred_element_type=F32)
    upd = upd + jnp.dot(ml_ref[...], w_ref[NA_WIDTH:, :], preferred_element_type=F32)
    xn = x_ref[...] + upd
    xo_ref[...] = xn
    ms2 = jnp.mean(xn * xn, axis=-1, keepdims=True)
    h = (xn * lax.rsqrt(ms2 + EPS) * fg_ref[...]).astype(BF16)
    h_ref[...] = h
    logits = jnp.dot(h, wr_ref[...], preferred_element_type=F32)
    lane = lax.broadcasted_iota(jnp.int32, logits.shape, 1)
    logits = jnp.where(lane < N_EXPERTS, logits, NEG)
    m = jnp.max(logits, axis=-1, keepdims=True)
    e = jnp.exp(logits - m)
    aff_ref[...] = e / jnp.sum(e, axis=-1, keepdims=True)


def _out_proj(na_o, ml_o, x2d, w_out, na_g, ffn_g, w_router, tm=512):
    T, D = x2d.shape
    return pl.pallas_call(
        _out_proj_kernel,
        grid=(T // tm,),
        in_specs=[
            pl.BlockSpec((tm, NA_WIDTH), lambda i: (i, 0)),
            pl.BlockSpec((tm, ML_V_WIDTH), lambda i: (i, 0)),
            pl.BlockSpec((tm, D), lambda i: (i, 0)),
            pl.BlockSpec((NA_WIDTH + ML_V_WIDTH, D), lambda i: (0, 0)),
            pl.BlockSpec((1, NA_WIDTH), lambda i: (0, 0)),
            pl.BlockSpec((1, D), lambda i: (0, 0)),
            pl.BlockSpec((D, LANES), lambda i: (0, 0)),
        ],
        out_specs=[
            pl.BlockSpec((tm, D), lambda i: (i, 0)),
            pl.BlockSpec((tm, D), lambda i: (i, 0)),
            pl.BlockSpec((tm, LANES), lambda i: (i, 0)),
        ],
        out_shape=[
            jax.ShapeDtypeStruct((T, D), F32),
            jax.ShapeDtypeStruct((T, D), BF16),
            jax.ShapeDtypeStruct((T, LANES), F32),
        ],
        compiler_params=_cparams(("parallel",)),
        name="out_proj",
    )(na_o, ml_o, x2d, w_out, na_g, ffn_g, w_router)


def _ffn_kernel(x_ref, wg_ref, wu_ref, wd_ref, gate_ref, y_ref):
    f = pl.program_id(2)
    x = x_ref[...]
    a = jnp.dot(x, wg_ref[...].astype(BF16), preferred_element_type=F32)
    u = jnp.dot(x, wu_ref[...].astype(BF16), preferred_element_type=F32)
    act = (a * _sigmoid(a) * u).astype(BF16)
    y = jnp.dot(act, wd_ref[...].astype(BF16), preferred_element_type=F32)

    @pl.when(f == 0)
    def _():
        y_ref[...] = y

    @pl.when(f > 0)
    def _():
        y_ref[...] += y

    @pl.when(f == pl.num_programs(2) - 1)
    def _():
        y_ref[...] = y_ref[...] * gate_ref[...]


def _expert_ffn(xin, w_gate, w_up, w_down, gate, layer, tm=1024, tf=256):
    E, M, D = xin.shape
    F = w_gate.shape[-1]
    tm = min(tm, M)
    return pl.pallas_call(
        _ffn_kernel,
        grid=(E, M // tm, F // tf),
        in_specs=[
            pl.BlockSpec((None, tm, D), lambda e, m, f: (e, m, 0)),
            pl.BlockSpec((None, None, D, tf), lambda e, m, f: (layer, e, 0, f)),
            pl.BlockSpec((None, None, D, tf), lambda e, m, f: (layer, e, 0, f)),
            pl.BlockSpec((None, None, tf, D), lambda e, m, f: (layer, e, f, 0)),
            pl.BlockSpec((None, tm, 1), lambda e, m, f: (e, m, 0)),
        ],
        out_specs=pl.BlockSpec((None, tm, D), lambda e, m, f: (e, m, 0)),
        out_shape=jax.ShapeDtypeStruct((E, M, D), F32),
        compiler_params=_cparams(("parallel", "parallel", "arbitrary")),
        name="expert_ffn",
    )(xin, w_gate, w_up, w_down, gate)


def _final_norm_kernel(x_ref, g_ref, o_ref):
    x = x_ref[...]
    ms = jnp.mean(x * x, axis=-1, keepdims=True)
    o_ref[...] = x * lax.rsqrt(ms + EPS) * g_ref[...]


def _final_norm(x2d, g, tm=1024):
    T, D = x2d.shape
    return pl.pallas_call(
        _final_norm_kernel,
        grid=(T // tm,),
        in_specs=[pl.BlockSpec((tm, D), lambda i: (i, 0)), pl.BlockSpec((1, D), lambda i: (0, 0))],
        out_specs=pl.BlockSpec((tm, D), lambda i: (i, 0)),
        out_shape=jax.ShapeDtypeStruct((T, D), F32),
        compiler_params=_cparams(("parallel",)),
        name="final_norm",
    )(x2d, g)


def kernel(x, norm_mix_g, w_in, b_gates, conv_w, conv_b, na_rpb, na_norm_g, ml_norm_g, w_out,
           norm_ffn_g, w_router, w_gate, w_up, w_down, final_norm_g):
    B, S, D = x.shape
    T = B * S
    depth = w_in.shape[0]
    cap = CAPACITY_FACTOR * S // N_EXPERTS
    x2d = x.reshape(T, D)
    bidx = jnp.arange(B)[:, None, None]

    for l in range(depth):
        w_main = w_in[l, :, :N_MAIN].astype(BF16)
        w_gates = jnp.pad(w_in[l, :, N_MAIN:], ((0, 0), (0, LANES - N_GATES))).astype(BF16)
        proj, gates = _in_proj(x2d, norm_mix_g[l][None, :], w_main, w_gates)

        tb = _na_bias_table(na_rpb[l])
        na_o = _na_attention(proj, tb, B, S)

        g4 = gates[:, :N_GATES].reshape(B, S, 4, ML_HEADS)
        g_rows = jnp.pad(jnp.transpose(g4, (0, 3, 2, 1)), ((0, 0), (0, 0), (0, 4), (0, 0)))
        bg_rows = jnp.pad(b_gates[l].reshape(4, ML_HEADS).T, ((0, 0), (0, 4)))[:, :, None]
        ml_o = _mlstm(proj, g_rows, bg_rows.astype(F32), conv_w[l], conv_b[l][None, :],
                      ml_norm_g[l][None, :], B, S)

        w_r = jnp.pad(w_router[l], ((0, 0), (0, LANES - N_EXPERTS))).astype(BF16)
        x2d, h2, aff = _out_proj(na_o, ml_o, x2d, w_out[l].astype(BF16), na_norm_g[l][None, :],
                                 norm_ffn_g[l][None, :], w_r)

        aff_t = jnp.swapaxes(aff[:, :N_EXPERTS].reshape(B, S, N_EXPERTS), 1, 2)
        gate, tok = lax.top_k(aff_t, cap)
        xin = h2.reshape(B, S, D)[bidx, tok]
        xin = jnp.swapaxes(xin, 0, 1).reshape(N_EXPERTS, B * cap, D)
        gcol = jnp.swapaxes(gate, 0, 1).reshape(N_EXPERTS, B * cap, 1)
        y = _expert_ffn(xin, w_gate, w_up, w_down, gcol, l)
        y = jnp.swapaxes(y.reshape(N_EXPERTS, B, cap, D), 0, 1)
        x2d = (x2d.reshape(B, S, D).at[bidx, tok].add(y)).reshape(T, D)

    return _final_norm(x2d, final_norm_g[None, :]).reshape(B, S, D)
```

```python
import functools

import numpy as np
import jax
import jax.numpy as jnp
from jax import lax
from jax.experimental import pallas as pl
from jax.experimental.pallas import tpu as pltpu

F32 = jnp.float32
BF16 = jnp.bfloat16

EPS = 1e-6
GRID_W = 64
NA_HEADS = 8
NA_HEAD_DIM = 128
NA_KR = 8
NA_KC = 16
NA_WIDTH = NA_HEADS * NA_HEAD_DIM
NA_LOOKAHEAD = 4
ML_HEADS = 4
ML_QK_DIM = 128
ML_V_DIM = 256
ML_QK_WIDTH = ML_HEADS * ML_QK_DIM
ML_V_WIDTH = ML_HEADS * ML_V_DIM
CONV_W = 5
N_GATES = 4 * ML_HEADS
N_EXPERTS = 16
CAPACITY_FACTOR = 2
N_MAIN = 3 * NA_WIDTH + 2 * ML_QK_WIDTH + 2 * ML_V_WIDTH
LANES = 128
NEG = -1e30
ML_CHUNK = 256
VMEM_LIMIT = 56 * 1024 * 1024


def _cparams(sem):
    return pltpu.CompilerParams(dimension_semantics=sem, vmem_limit_bytes=VMEM_LIMIT)


def _in_proj_kernel(x_ref, g_ref, w_ref, wg_ref, o_ref, og_ref, h_scr):
    @pl.when(pl.program_id(1) == 0)
    def _():
        x = x_ref[...]
        ms = jnp.mean(x * x, axis=-1, keepdims=True)
        h = (x * lax.rsqrt(ms + EPS) * g_ref[...]).astype(BF16)
        h_scr[...] = h
        og_ref[...] = jnp.dot(h, wg_ref[...], preferred_element_type=F32)

    o_ref[...] = jnp.dot(h_scr[...], w_ref[...], preferred_element_type=F32).astype(o_ref.dtype)


def _in_proj(x2d, g, w_main, w_gates, tm=1024, tn=512):
    T, D = x2d.shape
    N = w_main.shape[1]
    return pl.pallas_call(
        _in_proj_kernel,
        grid=(T // tm, N // tn),
        in_specs=[
            pl.BlockSpec((tm, D), lambda i, j: (i, 0)),
            pl.BlockSpec((1, D), lambda i, j: (0, 0)),
            pl.BlockSpec((D, tn), lambda i, j: (0, j)),
            pl.BlockSpec((D, LANES), lambda i, j: (0, 0)),
        ],
        out_specs=[
            pl.BlockSpec((tm, tn), lambda i, j: (i, j)),
            pl.BlockSpec((tm, LANES), lambda i, j: (i, 0)),
        ],
        out_shape=[
            jax.ShapeDtypeStruct((T, N), BF16),
            jax.ShapeDtypeStruct((T, LANES), F32),
        ],
        scratch_shapes=[pltpu.VMEM((tm, D), BF16)],
        compiler_params=_cparams(("parallel", "arbitrary")),
        name="in_proj",
    )(x2d, g, w_main, w_gates)


def _na_bias_table(rpb):
    n_dr, n_dc = 2 * NA_KR - 1, 2 * NA_KC - 1
    off = np.arange(NA_KR)[:, None]
    a = np.arange(NA_KR)[None, :]
    dr = a - off + (NA_KR - 1)
    c = np.arange(GRID_W)
    col_start = np.clip(c - NA_KC // 2, 0, GRID_W - NA_KC)
    col_in = (c[None, :] >= col_start[:, None]) & (c[None, :] < col_start[:, None] + NA_KC)
    dc = np.clip(c[None, :] - c[:, None] + (NA_KC - 1), 0, n_dc - 1)
    sel_r = jnp.asarray(dr[:, :, None] == np.arange(n_dr), F32)
    sel_c = jnp.asarray(dc[:, :, None] == np.arange(n_dc), F32)
    tb = jnp.einsum("oai,hij,qkj->hoqak", sel_r, rpb.astype(F32), sel_c,
                    precision=lax.Precision.HIGHEST)
    tb = jnp.where(col_in[None, None, :, None, :], tb, NEG)
    return tb.reshape(rpb.shape[0], NA_KR, GRID_W, NA_KR * GRID_W)


def _na_kernel(q_ref, k_ref, v_ref, tb_ref, o_ref, *, rb, rows):
    i = pl.program_id(2)
    scale = NA_HEAD_DIM ** -0.5
    win = NA_KR * GRID_W

    def scores(rr):
        r = i * rb + rr
        rs = jnp.clip(r - NA_KR // 2, 0, rows - NA_KR)
        start = pl.multiple_of(rs * GRID_W, GRID_W)
        q = q_ref[rr * GRID_W:(rr + 1) * GRID_W, :]
        kw = k_ref[pl.ds(start, win), :]
        s = lax.dot_general(q, kw, (((1,), (1,)), ((), ())), preferred_element_type=F32)
        return s, r - rs, start

    queue = [scores(rr) for rr in range(min(NA_LOOKAHEAD, rb))]
    for rr in range(rb):
        s, off, start = queue.pop(0)
        if rr + NA_LOOKAHEAD < rb:
            queue.append(scores(rr + NA_LOOKAHEAD))
        vw = v_ref[pl.ds(start, win), :]
        s = s * scale + tb_ref[off]
        m = jnp.max(s, axis=-1, keepdims=True)
        p = jnp.exp(s - m)
        l = jnp.sum(p, axis=-1, keepdims=True)
        o = jnp.dot(p.astype(BF16), vw, preferred_element_type=F32)
        o_ref[rr * GRID_W:(rr + 1) * GRID_W, :] = (o / l).astype(o_ref.dtype)


def _na_attention(proj, tb, B, S, rb=16):
    T = B * S
    rows = S // GRID_W
    assert rows >= NA_KR and rows % rb == 0
    nblk = rows // rb
    tq = rb * GRID_W
    kern = functools.partial(_na_kernel, rb=rb, rows=rows)
    return pl.pallas_call(
        kern,
        grid=(B, NA_HEADS, nblk),
        in_specs=[
            pl.BlockSpec((tq, NA_HEAD_DIM), lambda b, h, i: (b * nblk + i, h)),
            pl.BlockSpec((S, NA_HEAD_DIM), lambda b, h, i: (b, NA_HEADS + h)),
            pl.BlockSpec((S, NA_HEAD_DIM), lambda b, h, i: (b, 2 * NA_HEADS + h)),
            pl.BlockSpec((None, NA_KR, GRID_W, NA_KR * GRID_W), lambda b, h, i: (h, 0, 0, 0)),
        ],
        out_specs=pl.BlockSpec((tq, NA_HEAD_DIM), lambda b, h, i: (b * nblk + i, h)),
        out_shape=jax.ShapeDtypeStruct((T, NA_WIDTH), BF16),
        compiler_params=_cparams(("parallel", "parallel", "arbitrary")),
        name="na_attention",
    )(proj, proj, proj, tb)


def _log_sigmoid(x):
    return jnp.minimum(x, 0.0) - jnp.log(1.0 + jnp.exp(-jnp.abs(x)))


def _sigmoid(x):
    return 1.0 / (1.0 + jnp.exp(-x))


def _mlstm_kernel(q_ref, k_ref, v_ref, og_ref, g_ref, bg_ref, cwq_ref, cwk_ref, cbq_ref, cbk_ref,
                  ng_ref, out_ref, qs, kst, rrow, bcol, hf, hb, cst, nst, *, S, L):
    nc = S // L
    halo = 8

    def conv_silu(src_ref, cw_ref, cb_ref, dst_ref, transposed, scale):
        cw = cw_ref[...]
        cbv = cb_ref[...]

        def body(c, carry):
            base = pl.multiple_of(c * L, L)
            lo = jnp.maximum(base - halo, 0)
            hi = jnp.minimum(base + L, S - halo)
            prev = src_ref[pl.ds(pl.multiple_of(lo, halo), halo), :].astype(F32)
            nxt = src_ref[pl.ds(pl.multiple_of(hi, halo), halo), :].astype(F32)
            prev = jnp.where(c > 0, prev, 0.0)
            nxt = jnp.where(c < nc - 1, nxt, 0.0)
            main = src_ref[pl.ds(base, L), :].astype(F32)
            blk = jnp.concatenate([prev, main, nxt], axis=0)
            n = L + 2 * halo
            acc = jnp.zeros((L, LANES), F32) + cbv
            for w in range(CONV_W):
                sh = (CONV_W // 2 - w) % n
                xs = blk if sh == 0 else pltpu.roll(blk, sh, 0)
                acc = acc + xs[halo:halo + L, :] * cw[w:w + 1, :]
            y = acc * _sigmoid(acc) * scale
            if transposed:
                dst_ref[:, pl.ds(base, L)] = y.T.astype(BF16)
            else:
                dst_ref[pl.ds(base, L), :] = y.astype(BF16)
            return carry

        lax.fori_loop(0, nc, body, 0)

    conv_silu(q_ref, cwq_ref, cbq_ref, qs, False, ML_QK_DIM ** -0.5)
    conv_silu(k_ref, cwk_ref, cbk_ref, kst, True, 1.0)

    t_idx = lax.broadcasted_iota(jnp.int32, (L, L), 0)
    j_idx = lax.broadcasted_iota(jnp.int32, (L, L), 1)
    upper = (t_idx <= j_idx).astype(BF16)
    lower = (t_idx >= j_idx).astype(BF16)

    def split3(x):
        hi = x.astype(BF16).astype(F32)
        mid = (x - hi).astype(BF16).astype(F32)
        lo = (x - hi - mid).astype(BF16).astype(F32)
        return hi, mid, lo

    def gate_body(c, carry):
        base = pl.multiple_of(c * L, L)
        g = g_ref[:, pl.ds(base, L)] + bg_ref[...]
        hi, mid, lo = split3(_log_sigmoid(g))
        lf3 = jnp.concatenate([hi, mid, lo, jnp.zeros_like(hi)], axis=0).astype(BF16)
        pre = jnp.dot(lf3, upper, preferred_element_type=F32)
        suf = jnp.dot(lf3, lower, preferred_element_type=F32)
        pre = pre[0:8] + pre[8:16] + pre[16:24]
        suf = suf[0:8] + suf[8:16] + suf[16:24]
        cum_f, cum_b = pre[1:2], suf[3:4]
        gx_f = cum_f[:, L - 1:L] - cum_f + g[0:1]
        gx_b = cum_b[:, 0:1] - cum_b + g[2:3]
        r = jnp.concatenate([g[0:1], cum_f, g[2:3], cum_b, gx_f, gx_b, g[0:1] - cum_f, g[2:3] - cum_b],
                            axis=0)
        rrow[:, pl.ds(base, L)] = r
        return carry

    lax.fori_loop(0, nc, gate_body, 0, unroll=4)

    r_all = rrow[...]
    row_all = lax.broadcasted_iota(jnp.int32, (8, S), 0)
    lane_all = lax.broadcasted_iota(jnp.int32, (8, S), 1) & (L - 1)
    pm = r_all
    sm = r_all
    sh = 1
    while sh < L:
        pm = jnp.maximum(pm, jnp.where(lane_all >= sh, pltpu.roll(pm, sh, 1), NEG))
        sm = jnp.maximum(sm, jnp.where(lane_all < L - sh, pltpu.roll(sm, S - sh, 1), NEG))
        sh *= 2
    mi = jnp.where(row_all == 6, pm + pltpu.roll(r_all, 5, 0), sm + pltpu.roll(r_all, 4, 0))
    rrow[...] = jnp.where(row_all >= 6, mi, r_all)

    pick_r = lax.broadcasted_iota(jnp.int32, (LANES, 2 * LANES), 0)
    pick_l = lax.broadcasted_iota(jnp.int32, (LANES, 2 * LANES), 1)

    def picker(k_lo, k_hi):
        want = jnp.where(pick_l < LANES, k_lo, k_hi)
        return ((pick_r < 24) & ((pick_r & 7) == want)).astype(BF16)

    pairs = ((1, 6, 0, 1), (4, 3, 2, 3), (7, 5, 4, 5))

    def bcol_body(c, carry):
        base = pl.multiple_of(c * L, L)
        hi, mid, lo = split3(rrow[:, pl.ds(base, L)])
        x = jnp.concatenate([hi, mid, lo, jnp.zeros((LANES - 24, L), F32)], axis=0)
        xt = x.T.astype(BF16)
        for k_lo, k_hi, d_lo, d_hi in pairs:
            y = jnp.dot(xt, picker(k_lo, k_hi), preferred_element_type=F32)
            bcol[d_lo, pl.ds(base, L), :] = y[:, :LANES]
            bcol[d_hi, pl.ds(base, L), :] = y[:, LANES:]
        return carry

    lax.fori_loop(0, nc, bcol_body, 0, unroll=4)

    fwd_mask = j_idx <= t_idx
    bwd_mask = j_idx >= t_idx
    ones_b = jnp.ones((L, LANES), BF16)

    def wide(x, n):
        return jnp.concatenate([x] * n, axis=1)

    def load(c, d, total_row):
        base = pl.multiple_of(c * L, L)
        qc = qs[pl.ds(base, L), :]
        C = cst[d]
        cum = bcol[3 * d, pl.ds(base, L), :]
        return dict(
            base=base, qc=qc, vc=v_ref[pl.ds(base, L), :], kt=kst[:, pl.ds(base, L)],
            cum=cum, mi=bcol[3 * d + 1, pl.ds(base, L), :], gx=bcol[3 * d + 2, pl.ds(base, L), :],
            i_row=rrow[2 * d:2 * d + 1, pl.ds(base, L)],
            cum_row=rrow[2 * d + 1:2 * d + 2, pl.ds(base, L)],
            total=cum[total_row:total_row + 1, :], C=C,
            s=jnp.dot(qc, kst[:, pl.ds(base, L)], preferred_element_type=F32),
            qC=jnp.dot(qc, C.astype(BF16), preferred_element_type=F32),
            qn=jnp.dot(qc, nst[d].astype(BF16), preferred_element_type=F32))

    def weights(x, mask, m_st):
        m_inter = x["cum"] + m_st
        m_j = jnp.maximum(m_inter, x["mi"])
        dmat = jnp.where(mask, wide(x["cum"], L // LANES) - x["cum_row"] + x["i_row"], NEG)
        qkw = (x["s"] * jnp.exp(dmat - wide(m_j, L // LANES))).astype(BF16)
        inter = jnp.exp(m_inter - m_j)
        m_new = jnp.maximum(x["total"] + m_st, jnp.max(x["gx"], axis=0, keepdims=True))
        wk = jnp.exp(x["gx"] - m_new)
        decay = jnp.exp(x["total"] + m_st - m_new)
        vw = (wide(wk, ML_V_DIM // LANES) * x["vc"].astype(F32)).astype(BF16)
        return qkw, inter, m_j, wk.astype(BF16), vw, decay, m_new

    def finish(x, d, qkw, inter, m_j, wk, vw, decay, h_ref):
        base = x["base"]
        den = inter * x["qn"] + jnp.dot(qkw, ones_b, preferred_element_type=F32)
        rden = 1.0 / jnp.maximum(jnp.abs(den), jnp.exp(-m_j))
        num = wide(inter, ML_V_DIM // LANES) * x["qC"] + jnp.dot(qkw, x["vc"], preferred_element_type=F32)
        h_ref[pl.ds(base, L), :] = num * wide(rden, ML_V_DIM // LANES)
        cst[d] = wide(decay, ML_V_DIM // LANES) * x["C"] + jnp.dot(x["kt"], vw, preferred_element_type=F32)
        nst[d] = decay * nst[d] + jnp.dot(x["kt"], wk, preferred_element_type=F32)

    cst[...] = jnp.zeros(cst.shape, F32)
    nst[...] = jnp.zeros(nst.shape, F32)

    def scan_body(t, carry):
        m_f, m_b = carry
        xf = load(t, 0, L - 1)
        xb = load(nc - 1 - t, 1, 0)
        wf = weights(xf, fwd_mask, m_f)
        finish(xf, 0, *wf[:-1], hf)
        wb = weights(xb, bwd_mask, m_b)
        finish(xb, 1, *wb[:-1], hb)
        return wf[-1], wb[-1]

    z_m = jnp.zeros((1, LANES), F32)
    lax.fori_loop(0, nc, scan_body, (z_m, z_m))

    ng = ng_ref[...]

    def fin_body(c, carry):
        base = pl.multiple_of(c * L, L)
        h = hf[pl.ds(base, L), :] + hb[pl.ds(base, L), :]
        ms = jnp.mean(h * h, axis=-1, keepdims=True)
        y = h * lax.rsqrt(ms + EPS) * ng
        y = y * _sigmoid(og_ref[pl.ds(base, L), :].astype(F32))
        out_ref[pl.ds(base, L), :] = y.astype(out_ref.dtype)
        return carry

    lax.fori_loop(0, nc, fin_body, 0)


def _mlstm(proj, gates_rows, bg_rows, conv_w, conv_b, norm_g, B, S):
    T = B * S
    L = min(ML_CHUNK, S)
    assert S % L == 0 and L % LANES == 0
    q0 = 3 * NA_WIDTH // ML_QK_DIM
    k0 = q0 + ML_HEADS
    v0 = (3 * NA_WIDTH + 2 * ML_QK_WIDTH) // ML_V_DIM
    o0 = v0 + ML_HEADS
    kern = functools.partial(_mlstm_kernel, S=S, L=L)
    return pl.pallas_call(
        kern,
        grid=(B, ML_HEADS),
        in_specs=[
            pl.BlockSpec((S, ML_QK_DIM), lambda b, h: (b, q0 + h)),
            pl.BlockSpec((S, ML_QK_DIM), lambda b, h: (b, k0 + h)),
            pl.BlockSpec((S, ML_V_DIM), lambda b, h: (b, v0 + h)),
            pl.BlockSpec((S, ML_V_DIM), lambda b, h: (b, o0 + h)),
            pl.BlockSpec((None, None, 8, S), lambda b, h: (b, h, 0, 0)),
            pl.BlockSpec((None, 8, 1), lambda b, h: (h, 0, 0)),
            pl.BlockSpec((CONV_W, ML_QK_DIM), lambda b, h: (0, h)),
            pl.BlockSpec((CONV_W, ML_QK_DIM), lambda b, h: (0, ML_HEADS + h)),
            pl.BlockSpec((1, ML_QK_DIM), lambda b, h: (0, h)),
            pl.BlockSpec((1, ML_QK_DIM), lambda b, h: (0, ML_HEADS + h)),
            pl.BlockSpec((1, ML_V_DIM), lambda b, h: (0, h)),
        ],
        out_specs=pl.BlockSpec((S, ML_V_DIM), lambda b, h: (b, h)),
        out_shape=jax.ShapeDtypeStruct((T, ML_V_WIDTH), BF16),
        scratch_shapes=[
            pltpu.VMEM((S, ML_QK_DIM), BF16),
            pltpu.VMEM((ML_QK_DIM, S), BF16),
            pltpu.VMEM((8, S), F32),
            pltpu.VMEM((6, S, LANES), F32),
            pltpu.VMEM((S, ML_V_DIM), F32),
            pltpu.VMEM((S, ML_V_DIM), F32),
            pltpu.VMEM((2, ML_QK_DIM, ML_V_DIM), F32),
            pltpu.VMEM((2, ML_QK_DIM, LANES), F32),
        ],
        compiler_params=_cparams(("parallel", "arbitrary")),
        name="mlstm",
    )(proj, proj, proj, proj, gates_rows, bg_rows, conv_w, conv_w, conv_b, conv_b, norm_g)


def _out_proj_kernel(na_ref, ml_ref, x_ref, w_ref, nag_ref, fg_ref, wr_ref, xo_ref, h_ref, aff_ref):
    na = na_ref[...].astype(F32)
    ms = jnp.mean(na * na, axis=-1, keepdims=True)
    na_n = (na * lax.rsqrt(ms + EPS) * nag_ref[...]).astype(BF16)
    upd = jnp.dot(na_n, w_ref[0:NA_WIDTH, :], preferred_element_type=F32)
    upd = upd + jnp.dot(ml_ref[...], w_ref[NA_WIDTH:, :], preferred_element_type=F32)
    xn = x_ref[...] + upd
    xo_ref[...] = xn
    ms2 = jnp.mean(xn * xn, axis=-1, keepdims=True)
    h = xn * lax.rsqrt(ms2 + EPS) * fg_ref[...]
    h_ref[...] = h
    logits = jnp.dot(h.astype(BF16), wr_ref[...], preferred_element_type=F32)
    lane = lax.broadcasted_iota(jnp.int32, logits.shape, 1)
    logits = jnp.where(lane < N_EXPERTS, logits, NEG)
    m = jnp.max(logits, axis=-1, keepdims=True)
    e = jnp.exp(logits - m)
    aff_ref[...] = e / jnp.sum(e, axis=-1, keepdims=True)


def _out_proj(na_o, ml_o, x2d, w_out, na_g, ffn_g, w_router, tm=512):
    T, D = x2d.shape
    return pl.pallas_call(
        _out_proj_kernel,
        grid=(T // tm,),
        in_specs=[
            pl.BlockSpec((tm, NA_WIDTH), lambda i: (i, 0)),
            pl.BlockSpec((tm, ML_V_WIDTH), lambda i: (i, 0)),
            pl.BlockSpec((tm, D), lambda i: (i, 0)),
            pl.BlockSpec((NA_WIDTH + ML_V_WIDTH, D), lambda i: (0, 0)),
            pl.BlockSpec((1, NA_WIDTH), lambda i: (0, 0)),
            pl.BlockSpec((1, D), lambda i: (0, 0)),
            pl.BlockSpec((D, LANES), lambda i: (0, 0)),
        ],
        out_specs=[
            pl.BlockSpec((tm, D), lambda i: (i, 0)),
            pl.BlockSpec((tm, D), lambda i: (i, 0)),
            pl.BlockSpec((tm, LANES), lambda i: (i, 0)),
        ],
        out_shape=[
            jax.ShapeDtypeStruct((T, D), F32),
            jax.ShapeDtypeStruct((T, D), F32),
            jax.ShapeDtypeStruct((T, LANES), F32),
        ],
        compiler_params=_cparams(("parallel",)),
        name="out_proj",
    )(na_o, ml_o, x2d, w_out, na_g, ffn_g, w_router)


I32 = jnp.int32
SLOT_LO = 32
TOK_LO = 64
CHUNK = 256


def _route_kernel(aff_ref, tok_ref, dst_ref, gate_ref, off_ref, cnt_ref, a_scr, sel_scr, pos_scr, tmp_scr,
                  *, S, cap):
    b = pl.program_id(0)
    E = N_EXPERTS
    nchunk = S // CHUNK
    zrows = E * cap
    n_hi = cap // SLOT_LO

    TB = 512
    for c in range(S // TB):
        a_scr[:, c * TB:(c + 1) * TB] = aff_ref[c * TB:(c + 1) * TB, :].T[0:E, :]
    A = a_scr[...]

    def thr_body(i, thr):
        cand = thr | jnp.left_shift(jnp.int32(1), 30 - i)
        cand_f = lax.bitcast_convert_type(cand, F32)
        cnt = jnp.sum((a_scr[...] >= cand_f).astype(F32), axis=1, keepdims=True)
        return jnp.where(cnt >= cap, cand, thr)

    thr = lax.fori_loop(0, 31, thr_body, jnp.zeros((E, 1), I32))
    thr_f = lax.bitcast_convert_type(thr, F32)
    gt = A > thr_f
    eq = A == thr_f
    need = cap - jnp.sum(gt.astype(F32), axis=1, keepdims=True)
    idx = lax.broadcasted_iota(I32, (E, S), 1)

    nbits = S.bit_length() - 1

    def tie_body(i, jt):
        cand = jt | jnp.left_shift(jnp.int32(1), nbits - 1 - i)
        c = jnp.sum((eq & (idx < cand)).astype(F32), axis=1, keepdims=True)
        return jnp.where(c < need, cand, jt)

    jt = lax.fori_loop(0, nbits, tie_body, jnp.zeros((E, 1), I32))
    sel = (gt | (eq & (idx <= jt))).astype(F32)
    sel_scr[...] = sel

    r_i = lax.broadcasted_iota(I32, (CHUNK, CHUNK), 0)
    c_i = lax.broadcasted_iota(I32, (CHUNK, CHUNK), 1)
    strict_upper = (r_i < c_i).astype(BF16)
    e_r = lax.broadcasted_iota(I32, (E, E), 0)
    e_c = lax.broadcasted_iota(I32, (E, E), 1)
    strict_lower = (e_c < e_r).astype(BF16)
    run = jnp.zeros((E, 1), F32)
    run_t = jnp.zeros((1, 1), F32)
    for c in range(nchunk):
        sc = sel[:, c * CHUNK:(c + 1) * CHUNK]
        scb = sc.astype(BF16)
        pos_scr[:, c * CHUNK:(c + 1) * CHUNK] = jnp.dot(scb, strict_upper, preferred_element_type=F32) + run
        run = run + jnp.sum(sc, axis=1, keepdims=True)
        cnt_c = jnp.sum(sc, axis=0, keepdims=True)
        cnt8 = jnp.broadcast_to(cnt_c, (8, CHUNK)).astype(BF16)
        base_c = jnp.dot(cnt8, strict_upper, preferred_element_type=F32)[0:1] + run_t
        run_t = run_t + jnp.sum(cnt_c, axis=1, keepdims=True)
        rank_c = jnp.dot(strict_lower, scb, preferred_element_type=F32)
        zoff = (b * zrows).astype(F32)
        tmp_scr[:, c * CHUNK:(c + 1) * CHUNK] = base_c + rank_c + zoff
        off_ref[:, c * CHUNK:(c + 1) * CHUNK] = (base_c + zoff).astype(I32)
        cnt_ref[:, c * CHUNK:(c + 1) * CHUNK] = cnt_c.astype(I32)

    lo_id = lax.broadcasted_iota(I32, (SLOT_LO, S), 0).astype(F32)
    hi_id = lax.broadcasted_iota(I32, (n_hi, S), 0).astype(F32)
    t_row = lax.broadcasted_iota(I32, (1, S), 1)
    t_hi = (t_row // TOK_LO).astype(F32)
    t_lo = (t_row % TOK_LO).astype(F32)

    def split3(x):
        hi = x.astype(BF16).astype(F32)
        mid = (x - hi).astype(BF16).astype(F32)
        lo = (x - hi - mid).astype(BF16).astype(F32)
        return hi, mid, lo

    for e in range(E):
        sel_e = sel_scr[e:e + 1, :]
        pos_e = pos_scr[e:e + 1, :]
        p_hi = jnp.floor(pos_e * (1.0 / SLOT_LO))
        p_lo = pos_e - p_hi * SLOT_LO
        oh_lo = jnp.where((lo_id == p_lo) & (sel_e > 0), 1.0, 0.0).astype(BF16)
        oh_hi = jnp.where(hi_id == p_hi, 1.0, 0.0)
        d = tmp_scr[e:e + 1, :]
        d2 = jnp.floor(d * (1.0 / 1024.0))
        d1 = jnp.floor((d - d2 * 1024.0) * (1.0 / 32.0))
        d0 = d - d2 * 1024.0 - d1 * 32.0
        g_hi, g_mid, g_lo = split3(a_scr[e:e + 1, :])
        vals = (t_hi, t_lo, d2, d1, d0, g_hi, g_mid, g_lo)
        stack = jnp.concatenate([oh_hi * v for v in vals], axis=0).astype(BF16)
        r = lax.dot_general(stack, oh_lo, (((1,), (1,)), ((), ())), preferred_element_type=F32)
        rr = [r[k * n_hi:(k + 1) * n_hi] for k in range(8)]
        tok_ref[e] = (rr[0] * TOK_LO + rr[1]).astype(I32) + b * S
        dst_ref[e] = (rr[2] * 1024.0 + rr[3] * 32.0 + rr[4]).astype(I32)
        gate_ref[e] = rr[5] + rr[6] + rr[7]


def _route(aff, B, S):
    cap = CAPACITY_FACTOR * S // N_EXPERTS
    n_hi = cap // SLOT_LO
    E = N_EXPERTS
    kern = functools.partial(_route_kernel, S=S, cap=cap)
    lists = jax.ShapeDtypeStruct((B, E, n_hi, SLOT_LO), I32)
    return pl.pallas_call(
        kern,
        grid=(B,),
        in_specs=[pl.BlockSpec((S, LANES), lambda b: (b, 0))],
        out_specs=[
            pl.BlockSpec((None, E, n_hi, SLOT_LO), lambda b: (b, 0, 0, 0)),
            pl.BlockSpec((None, E, n_hi, SLOT_LO), lambda b: (b, 0, 0, 0)),
            pl.BlockSpec((None, E, n_hi, SLOT_LO), lambda b: (b, 0, 0, 0)),
            pl.BlockSpec((None, 1, S), lambda b: (b, 0, 0)),
            pl.BlockSpec((None, 1, S), lambda b: (b, 0, 0)),
        ],
        out_shape=[lists, lists, jax.ShapeDtypeStruct((B, E, n_hi, SLOT_LO), F32),
                   jax.ShapeDtypeStruct((B, 1, S), I32), jax.ShapeDtypeStruct((B, 1, S), I32)],
        scratch_shapes=[pltpu.VMEM((E, S), F32)] * 4,
        compiler_params=_cparams(("parallel",)),
        name="route",
    )(aff)


FFN_ROWS = 128


def _ffn_kernel(tok_ref, dst_ref, h_hbm, wg_ref, wu_ref, wd_ref, gate_ref, z_hbm, xg, xb, yacc, sem_g, sem_s,
                *, tm):
    f = pl.program_id(2)

    def gather_row(r):
        return pltpu.make_async_copy(h_hbm.at[pl.ds(tok_ref[0, r], 1), :], xg.at[pl.ds(r, 1), :], sem_g)

    def scatter_row(r):
        return pltpu.make_async_copy(yacc.at[pl.ds(r, 1), :], z_hbm.at[pl.ds(dst_ref[0, r], 1), :], sem_s)

    def for_rows(fn):
        def body(r, c):
            fn(r)
            return c
        lax.fori_loop(0, tm, body, 0, unroll=8)

    @pl.when(f == 0)
    def _():
        for_rows(lambda r: gather_row(r).start())
        for_rows(lambda r: gather_row(r).wait())

        def cast(j, c):
            rows = pl.ds(pl.multiple_of(j * FFN_ROWS, FFN_ROWS), FFN_ROWS)
            xb[rows, :] = xg[rows, :].astype(BF16)
            return c
        lax.fori_loop(0, tm // FFN_ROWS, cast, 0)

    x = xb[...]
    a = jnp.dot(x, wg_ref[...].astype(BF16), preferred_element_type=F32)
    u = jnp.dot(x, wu_ref[...].astype(BF16), preferred_element_type=F32)
    act = (a * _sigmoid(a) * u).astype(BF16)
    y = jnp.dot(act, wd_ref[...].astype(BF16), preferred_element_type=F32)

    @pl.when(f == 0)
    def _():
        yacc[...] = y

    @pl.when(f > 0)
    def _():
        yacc[...] += y

    @pl.when(f == pl.num_programs(2) - 1)
    def _():
        def scale(j, c):
            rows = pl.ds(pl.multiple_of(j * FFN_ROWS, FFN_ROWS), FFN_ROWS)
            yacc[rows, :] = yacc[rows, :] * gate_ref[rows, :]
            return c
        lax.fori_loop(0, tm // FFN_ROWS, scale, 0)
        for_rows(lambda r: scatter_row(r).start())
        for_rows(lambda r: scatter_row(r).wait())


def _expert_ffn(h2, tok, dst, gate, w_gate, w_up, w_down, layer, tm=1024, tf=256):
    E, M = tok.shape
    T, D = h2.shape
    F = w_gate.shape[-1]
    tm = min(tm, M)
    nm = M // tm
    tok3 = tok.reshape(E * nm, 1, tm)
    dst3 = dst.reshape(E * nm, 1, tm)
    smem_spec = pl.BlockSpec((None, 1, tm), lambda e, m, f: (e * nm + m, 0, 0), memory_space=pltpu.SMEM)
    kern = functools.partial(_ffn_kernel, tm=tm)
    return pl.pallas_call(
        kern,
        grid=(E, nm, F // tf),
        in_specs=[
            smem_spec,
            smem_spec,
            pl.BlockSpec(memory_space=pl.ANY),
            pl.BlockSpec((None, None, D, tf), lambda e, m, f: (layer, e, 0, f)),
            pl.BlockSpec((None, None, D, tf), lambda e, m, f: (layer, e, 0, f)),
            pl.BlockSpec((None, None, tf, D), lambda e, m, f: (layer, e, f, 0)),
            pl.BlockSpec((None, tm, 1), lambda e, m, f: (e, m, 0)),
        ],
        out_specs=pl.BlockSpec(memory_space=pl.ANY),
        out_shape=jax.ShapeDtypeStruct((E * M, D), F32),
        scratch_shapes=[
            pltpu.VMEM((tm, D), F32),
            pltpu.VMEM((tm, D), BF16),
            pltpu.VMEM((tm, D), F32),
            pltpu.SemaphoreType.DMA(()),
            pltpu.SemaphoreType.DMA(()),
        ],
        compiler_params=_cparams(("arbitrary", "arbitrary", "arbitrary")),
        name="expert_ffn",
    )(tok3, dst3, h2, w_gate, w_up, w_down, gate)


COMB_TOK = 256
COMB_ROWS = 256


def _combine_kernel(start_ref, nch_ref, x_ref, off_ref, cnt_ref, z_hbm, o_ref, zbuf, sem, *, zrows):
    i = pl.program_id(0)
    start = start_ref[i]
    n = nch_ref[i]
    o_ref[...] = x_ref[...]
    off = off_ref[...]
    end = off + cnt_ref[...]

    def chunk_start(k):
        return pl.multiple_of(jnp.minimum(start + k * COMB_ROWS, zrows - COMB_ROWS), 8)

    def copy(k, slot):
        return pltpu.make_async_copy(z_hbm.at[pl.ds(chunk_start(k), COMB_ROWS), :], zbuf.at[slot], sem.at[slot])

    @pl.when(n > 0)
    def _():
        copy(0, 0).start()

    def body(k, c):
        slot = k & 1
        copy(k, slot).wait()

        @pl.when(k + 1 < n)
        def _():
            copy(k + 1, 1 - slot).start()

        rows = chunk_start(k) + lax.broadcasted_iota(I32, (1, COMB_ROWS), 1)
        lo = jnp.maximum(off, start + k * COMB_ROWS)
        seg = jnp.where((rows >= lo) & (rows < end), 1.0, 0.0).astype(BF16)
        z = zbuf[slot]
        z_hi = z.astype(BF16)
        z_lo = (z - z_hi.astype(F32)).astype(BF16)
        o_ref[...] += jnp.dot(seg, z_hi, preferred_element_type=F32) + jnp.dot(seg, z_lo, preferred_element_type=F32)
        return c

    lax.fori_loop(0, n, body, 0)


def _combine(x2d, z, off, cnt):
    T, D = x2d.shape
    zrows = z.shape[0]
    tt = COMB_TOK
    nt = T // tt
    start = (off[::tt] // 8) * 8
    last = (off + cnt)[tt - 1::tt]
    nch = (last - start + COMB_ROWS - 1) // COMB_ROWS
    kern = functools.partial(_combine_kernel, zrows=zrows)
    grid_spec = pltpu.PrefetchScalarGridSpec(
        num_scalar_prefetch=2,
        grid=(nt,),
        in_specs=[
            pl.BlockSpec((tt, D), lambda i, s, n: (i, 0)),
            pl.BlockSpec((tt, 1), lambda i, s, n: (i, 0)),
            pl.BlockSpec((tt, 1), lambda i, s, n: (i, 0)),
            pl.BlockSpec(memory_space=pl.ANY),
        ],
        out_specs=pl.BlockSpec((tt, D), lambda i, s, n: (i, 0)),
        scratch_shapes=[pltpu.VMEM((2, COMB_ROWS, D), F32), pltpu.SemaphoreType.DMA((2,))],
    )
    return pl.pallas_call(
        kern,
        grid_spec=grid_spec,
        out_shape=jax.ShapeDtypeStruct((T, D), F32),
        compiler_params=_cparams(("arbitrary",)),
        name="combine",
    )(start.astype(I32), nch.astype(I32), x2d, off[:, None], cnt[:, None], z)


def moe(x2d, h2, aff, w_gate, w_up, w_down, layer, B, S):
    E = N_EXPERTS
    cap = CAPACITY_FACTOR * S // E
    tok, dst, gate, off, cnt = _route(aff, B, S)
    to_em = lambda a: jnp.swapaxes(a.reshape(B, E, cap), 0, 1).reshape(E, B * cap)
    z = _expert_ffn(h2, to_em(tok), to_em(dst), to_em(gate)[:, :, None], w_gate, w_up, w_down, layer)
    return _combine(x2d, z, off.reshape(B * S), cnt.reshape(B * S))


def _final_norm_kernel(x_ref, g_ref, o_ref):
    x = x_ref[...]
    ms = jnp.mean(x * x, axis=-1, keepdims=True)
    o_ref[...] = x * lax.rsqrt(ms + EPS) * g_ref[...]


def _final_norm(x2d, g, tm=1024):
    T, D = x2d.shape
    return pl.pallas_call(
        _final_norm_kernel,
        grid=(T // tm,),
        in_specs=[pl.BlockSpec((tm, D), lambda i: (i, 0)), pl.BlockSpec((1, D), lambda i: (0, 0))],
        out_specs=pl.BlockSpec((tm, D), lambda i: (i, 0)),
        out_shape=jax.ShapeDtypeStruct((T, D), F32),
        compiler_params=_cparams(("parallel",)),
        name="final_norm",
    )(x2d, g)


def kernel(x, norm_mix_g, w_in, b_gates, conv_w, conv_b, na_rpb, na_norm_g, ml_norm_g, w_out,
           norm_ffn_g, w_router, w_gate, w_up, w_down, final_norm_g):
    B, S, D = x.shape
    T = B * S
    depth = w_in.shape[0]
    x2d = x.reshape(T, D)

    for l in range(depth):
        w_main = w_in[l, :, :N_MAIN].astype(BF16)
        w_gates = jnp.pad(w_in[l, :, N_MAIN:], ((0, 0), (0, LANES - N_GATES))).astype(BF16)
        proj, gates = _in_proj(x2d, norm_mix_g[l][None, :], w_main, w_gates)

        tb = _na_bias_table(na_rpb[l])
        na_o = _na_attention(proj, tb, B, S)

        g4 = gates[:, :N_GATES].reshape(B, S, 4, ML_HEADS)
        g_rows = jnp.pad(jnp.transpose(g4, (0, 3, 2, 1)), ((0, 0), (0, 0), (0, 4), (0, 0)))
        bg_rows = jnp.pad(b_gates[l].reshape(4, ML_HEADS).T, ((0, 0), (0, 4)))[:, :, None]
        ml_o = _mlstm(proj, g_rows, bg_rows.astype(F32), conv_w[l], conv_b[l][None, :],
                      ml_norm_g[l][None, :], B, S)

        w_r = jnp.pad(w_router[l], ((0, 0), (0, LANES - N_EXPERTS))).astype(BF16)
        x2d, h2, aff = _out_proj(na_o, ml_o, x2d, w_out[l].astype(BF16), na_norm_g[l][None, :],
                                 norm_ffn_g[l][None, :], w_r)

        x2d = moe(x2d, h2, aff, w_gate, w_up, w_down, l, B, S)

    return _final_norm(x2d, final_norm_g[None, :]).reshape(B, S, D)
```

```python
import functools

import numpy as np
import jax
import jax.numpy as jnp
from jax import lax
from jax.experimental import pallas as pl
from jax.experimental.pallas import tpu as pltpu

F32 = jnp.float32
BF16 = jnp.bfloat16

EPS = 1e-6
GRID_W = 64
NA_HEADS = 8
NA_HEAD_DIM = 128
NA_KR = 8
NA_KC = 16
NA_WIDTH = NA_HEADS * NA_HEAD_DIM
NA_LOOKAHEAD = 4
ML_HEADS = 4
ML_QK_DIM = 128
ML_V_DIM = 256
ML_QK_WIDTH = ML_HEADS * ML_QK_DIM
ML_V_WIDTH = ML_HEADS * ML_V_DIM
CONV_W = 5
N_GATES = 4 * ML_HEADS
N_EXPERTS = 16
CAPACITY_FACTOR = 2
N_MAIN = 3 * NA_WIDTH + 2 * ML_QK_WIDTH + 2 * ML_V_WIDTH
LANES = 128
NEG = -1e30
ML_CHUNK = 256
VMEM_LIMIT = 56 * 1024 * 1024


def _cparams(sem):
    return pltpu.CompilerParams(dimension_semantics=sem, vmem_limit_bytes=VMEM_LIMIT)


def _in_proj_kernel(x_ref, g_ref, w_ref, wg_ref, o_ref, og_ref, h_scr):
    @pl.when(pl.program_id(1) == 0)
    def _():
        x = x_ref[...]
        ms = jnp.mean(x * x, axis=-1, keepdims=True)
        h = (x * lax.rsqrt(ms + EPS) * g_ref[...]).astype(BF16)
        h_scr[...] = h
        og_ref[...] = jnp.dot(h, wg_ref[...], preferred_element_type=F32)

    o_ref[...] = jnp.dot(h_scr[...], w_ref[...], preferred_element_type=F32).astype(o_ref.dtype)


def _in_proj(x2d, g, w_main, w_gates, tm=1024, tn=2048):
    T, D = x2d.shape
    N = w_main.shape[1]
    return pl.pallas_call(
        _in_proj_kernel,
        grid=(T // tm, N // tn),
        in_specs=[
            pl.BlockSpec((tm, D), lambda i, j: (i, 0)),
            pl.BlockSpec((1, D), lambda i, j: (0, 0)),
            pl.BlockSpec((D, tn), lambda i, j: (0, j)),
            pl.BlockSpec((D, LANES), lambda i, j: (0, 0)),
        ],
        out_specs=[
            pl.BlockSpec((tm, tn), lambda i, j: (i, j)),
            pl.BlockSpec((tm, LANES), lambda i, j: (i, 0)),
        ],
        out_shape=[
            jax.ShapeDtypeStruct((T, N), BF16),
            jax.ShapeDtypeStruct((T, LANES), F32),
        ],
        scratch_shapes=[pltpu.VMEM((tm, D), BF16)],
        compiler_params=_cparams(("parallel", "arbitrary")),
        name="in_proj",
    )(x2d, g, w_main, w_gates)


def _na_bias_table(rpb):
    n_dr, n_dc = 2 * NA_KR - 1, 2 * NA_KC - 1
    off = np.arange(NA_KR)[:, None]
    a = np.arange(NA_KR)[None, :]
    dr = a - off + (NA_KR - 1)
    c = np.arange(GRID_W)
    col_start = np.clip(c - NA_KC // 2, 0, GRID_W - NA_KC)
    col_in = (c[None, :] >= col_start[:, None]) & (c[None, :] < col_start[:, None] + NA_KC)
    dc = np.clip(c[None, :] - c[:, None] + (NA_KC - 1), 0, n_dc - 1)
    sel_r = jnp.asarray(dr[:, :, None] == np.arange(n_dr), F32)
    sel_c = jnp.asarray(dc[:, :, None] == np.arange(n_dc), F32)
    tb = jnp.einsum("oai,hij,qkj->hoqak", sel_r, rpb.astype(F32), sel_c,
                    precision=lax.Precision.HIGHEST)
    tb = jnp.where(col_in[None, None, :, None, :], tb, NEG)
    return tb.reshape(rpb.shape[0], NA_KR, GRID_W, NA_KR * GRID_W)


def _na_kernel(q_ref, k_ref, v_ref, tb_ref, o_ref, *, rb, rows):
    i = pl.program_id(2)
    scale = NA_HEAD_DIM ** -0.5
    win = NA_KR * GRID_W

    def scores(rr):
        r = i * rb + rr
        rs = jnp.clip(r - NA_KR // 2, 0, rows - NA_KR)
        start = pl.multiple_of(rs * GRID_W, GRID_W)
        q = q_ref[rr * GRID_W:(rr + 1) * GRID_W, :]
        kw = k_ref[pl.ds(start, win), :]
        s = lax.dot_general(q, kw, (((1,), (1,)), ((), ())), preferred_element_type=F32)
        return s, r - rs, start

    queue = [scores(rr) for rr in range(min(NA_LOOKAHEAD, rb))]
    for rr in range(rb):
        s, off, start = queue.pop(0)
        if rr + NA_LOOKAHEAD < rb:
            queue.append(scores(rr + NA_LOOKAHEAD))
        vw = v_ref[pl.ds(start, win), :]
        s = s * scale + tb_ref[off]
        m = jnp.max(s, axis=-1, keepdims=True)
        p = jnp.exp(s - m)
        l = jnp.sum(p, axis=-1, keepdims=True)
        o = jnp.dot(p.astype(BF16), vw, preferred_element_type=F32)
        o_ref[rr * GRID_W:(rr + 1) * GRID_W, :] = (o / l).astype(o_ref.dtype)


def _na_attention(proj, tb, B, S, rb=16):
    T = B * S
    rows = S // GRID_W
    assert rows >= NA_KR and rows % rb == 0
    nblk = rows // rb
    tq = rb * GRID_W
    kern = functools.partial(_na_kernel, rb=rb, rows=rows)
    return pl.pallas_call(
        kern,
        grid=(B, NA_HEADS, nblk),
        in_specs=[
            pl.BlockSpec((tq, NA_HEAD_DIM), lambda b, h, i: (b * nblk + i, h)),
            pl.BlockSpec((S, NA_HEAD_DIM), lambda b, h, i: (b, NA_HEADS + h)),
            pl.BlockSpec((S, NA_HEAD_DIM), lambda b, h, i: (b, 2 * NA_HEADS + h)),
            pl.BlockSpec((None, NA_KR, GRID_W, NA_KR * GRID_W), lambda b, h, i: (h, 0, 0, 0)),
        ],
        out_specs=pl.BlockSpec((tq, NA_HEAD_DIM), lambda b, h, i: (b * nblk + i, h)),
        out_shape=jax.ShapeDtypeStruct((T, NA_WIDTH), BF16),
        compiler_params=_cparams(("parallel", "parallel", "arbitrary")),
        name="na_attention",
    )(proj, proj, proj, tb)


def _log_sigmoid(x):
    return jnp.minimum(x, 0.0) - jnp.log(1.0 + jnp.exp(-jnp.abs(x)))


def _sigmoid(x):
    return 1.0 / (1.0 + jnp.exp(-x))


def _mlstm_kernel(q_ref, k_ref, v_ref, og_ref, g_ref, bg_ref, cwq_ref, cwk_ref, cbq_ref, cbk_ref,
                  ng_ref, out_ref, qs, kst, rrow, bcol, hf, hb, cst, nst, *, S, L):
    nc = S // L
    halo = 8

    def conv_silu(src_ref, cw_ref, cb_ref, dst_ref, transposed, scale):
        cw = cw_ref[...]
        cbv = cb_ref[...]

        def body(c, carry):
            base = pl.multiple_of(c * L, L)
            lo = jnp.maximum(base - halo, 0)
            hi = jnp.minimum(base + L, S - halo)
            prev = src_ref[pl.ds(pl.multiple_of(lo, halo), halo), :].astype(F32)
            nxt = src_ref[pl.ds(pl.multiple_of(hi, halo), halo), :].astype(F32)
            prev = jnp.where(c > 0, prev, 0.0)
            nxt = jnp.where(c < nc - 1, nxt, 0.0)
            main = src_ref[pl.ds(base, L), :].astype(F32)
            blk = jnp.concatenate([prev, main, nxt], axis=0)
            n = L + 2 * halo
            acc = jnp.zeros((L, LANES), F32) + cbv
            for w in range(CONV_W):
                sh = (CONV_W // 2 - w) % n
                xs = blk if sh == 0 else pltpu.roll(blk, sh, 0)
                acc = acc + xs[halo:halo + L, :] * cw[w:w + 1, :]
            y = acc * _sigmoid(acc) * scale
            if transposed:
                dst_ref[:, pl.ds(base, L)] = y.T.astype(BF16)
            else:
                dst_ref[pl.ds(base, L), :] = y.astype(BF16)
            return carry

        lax.fori_loop(0, nc, body, 0)

    conv_silu(q_ref, cwq_ref, cbq_ref, qs, False, ML_QK_DIM ** -0.5)
    conv_silu(k_ref, cwk_ref, cbk_ref, kst, True, 1.0)

    t_idx = lax.broadcasted_iota(jnp.int32, (L, L), 0)
    j_idx = lax.broadcasted_iota(jnp.int32, (L, L), 1)
    upper = (t_idx <= j_idx).astype(BF16)
    lower = (t_idx >= j_idx).astype(BF16)

    def split3(x):
        hi = x.astype(BF16).astype(F32)
        mid = (x - hi).astype(BF16).astype(F32)
        lo = (x - hi - mid).astype(BF16).astype(F32)
        return hi, mid, lo

    def gate_body(c, carry):
        base = pl.multiple_of(c * L, L)
        g = g_ref[:, pl.ds(base, L)] + bg_ref[...]
        hi, mid, lo = split3(_log_sigmoid(g))
        lf3 = jnp.concatenate([hi, mid, lo, jnp.zeros_like(hi)], axis=0).astype(BF16)
        pre = jnp.dot(lf3, upper, preferred_element_type=F32)
        suf = jnp.dot(lf3, lower, preferred_element_type=F32)
        pre = pre[0:8] + pre[8:16] + pre[16:24]
        suf = suf[0:8] + suf[8:16] + suf[16:24]
        cum_f, cum_b = pre[1:2], suf[3:4]
        gx_f = cum_f[:, L - 1:L] - cum_f + g[0:1]
        gx_b = cum_b[:, 0:1] - cum_b + g[2:3]
        r = jnp.concatenate([g[0:1], cum_f, g[2:3], cum_b, gx_f, gx_b, g[0:1] - cum_f, g[2:3] - cum_b],
                            axis=0)
        rrow[:, pl.ds(base, L)] = r
        return carry

    lax.fori_loop(0, nc, gate_body, 0, unroll=4)

    r_all = rrow[...]
    row_all = lax.broadcasted_iota(jnp.int32, (8, S), 0)
    lane_all = lax.broadcasted_iota(jnp.int32, (8, S), 1) & (L - 1)
    pm = r_all
    sm = r_all
    sh = 1
    while sh < L:
        pm = jnp.maximum(pm, jnp.where(lane_all >= sh, pltpu.roll(pm, sh, 1), NEG))
        sm = jnp.maximum(sm, jnp.where(lane_all < L - sh, pltpu.roll(sm, S - sh, 1), NEG))
        sh *= 2
    mi = jnp.where(row_all == 6, pm + pltpu.roll(r_all, 5, 0), sm + pltpu.roll(r_all, 4, 0))
    rrow[...] = jnp.where(row_all >= 6, mi, r_all)

    pick_r = lax.broadcasted_iota(jnp.int32, (LANES, 2 * LANES), 0)
    pick_l = lax.broadcasted_iota(jnp.int32, (LANES, 2 * LANES), 1)

    def picker(k_lo, k_hi):
        want = jnp.where(pick_l < LANES, k_lo, k_hi)
        return ((pick_r < 24) & ((pick_r & 7) == want)).astype(BF16)

    pairs = ((1, 6, 0, 1), (4, 3, 2, 3), (7, 5, 4, 5))

    def bcol_body(c, carry):
        base = pl.multiple_of(c * L, L)
        hi, mid, lo = split3(rrow[:, pl.ds(base, L)])
        x = jnp.concatenate([hi, mid, lo, jnp.zeros((LANES - 24, L), F32)], axis=0)
        xt = x.T.astype(BF16)
        for k_lo, k_hi, d_lo, d_hi in pairs:
            y = jnp.dot(xt, picker(k_lo, k_hi), preferred_element_type=F32)
            bcol[d_lo, pl.ds(base, L), :] = y[:, :LANES]
            bcol[d_hi, pl.ds(base, L), :] = y[:, LANES:]
        return carry

    lax.fori_loop(0, nc, bcol_body, 0, unroll=4)

    fwd_mask = j_idx <= t_idx
    bwd_mask = j_idx >= t_idx
    ones_b = jnp.ones((L, LANES), BF16)

    def wide(x, n):
        return jnp.concatenate([x] * n, axis=1)

    def load(c, d, total_row):
        base = pl.multiple_of(c * L, L)
        qc = qs[pl.ds(base, L), :]
        C = cst[d]
        cum = bcol[3 * d, pl.ds(base, L), :]
        return dict(
            base=base, qc=qc, vc=v_ref[pl.ds(base, L), :], kt=kst[:, pl.ds(base, L)],
            cum=cum, mi=bcol[3 * d + 1, pl.ds(base, L), :], gx=bcol[3 * d + 2, pl.ds(base, L), :],
            i_row=rrow[2 * d:2 * d + 1, pl.ds(base, L)],
            cum_row=rrow[2 * d + 1:2 * d + 2, pl.ds(base, L)],
            total=cum[total_row:total_row + 1, :], C=C,
            s=jnp.dot(qc, kst[:, pl.ds(base, L)], preferred_element_type=F32),
            qC=jnp.dot(qc, C.astype(BF16), preferred_element_type=F32),
            qn=jnp.dot(qc, nst[d].astype(BF16), preferred_element_type=F32))

    def weights(x, mask, m_st):
        m_inter = x["cum"] + m_st
        m_j = jnp.maximum(m_inter, x["mi"])
        dmat = jnp.where(mask, wide(x["cum"], L // LANES) - x["cum_row"] + x["i_row"], NEG)
        qkw = (x["s"] * jnp.exp(dmat - wide(m_j, L // LANES))).astype(BF16)
        inter = jnp.exp(m_inter - m_j)
        m_new = jnp.maximum(x["total"] + m_st, jnp.max(x["gx"], axis=0, keepdims=True))
        wk = jnp.exp(x["gx"] - m_new)
        decay = jnp.exp(x["total"] + m_st - m_new)
        vw = (wide(wk, ML_V_DIM // LANES) * x["vc"].astype(F32)).astype(BF16)
        return qkw, inter, m_j, wk.astype(BF16), vw, decay, m_new

    def finish(x, d, qkw, inter, m_j, wk, vw, decay, h_ref):
        base = x["base"]
        den = inter * x["qn"] + jnp.dot(qkw, ones_b, preferred_element_type=F32)
        rden = 1.0 / jnp.maximum(jnp.abs(den), jnp.exp(-m_j))
        num = wide(inter, ML_V_DIM // LANES) * x["qC"] + jnp.dot(qkw, x["vc"], preferred_element_type=F32)
        h_ref[pl.ds(base, L), :] = num * wide(rden, ML_V_DIM // LANES)
        cst[d] = wide(decay, ML_V_DIM // LANES) * x["C"] + jnp.dot(x["kt"], vw, preferred_element_type=F32)
        nst[d] = decay * nst[d] + jnp.dot(x["kt"], wk, preferred_element_type=F32)

    cst[...] = jnp.zeros(cst.shape, F32)
    nst[...] = jnp.zeros(nst.shape, F32)

    def scan_body(t, carry):
        m_f, m_b = carry
        xf = load(t, 0, L - 1)
        xb = load(nc - 1 - t, 1, 0)
        wf = weights(xf, fwd_mask, m_f)
        finish(xf, 0, *wf[:-1], hf)
        wb = weights(xb, bwd_mask, m_b)
        finish(xb, 1, *wb[:-1], hb)
        return wf[-1], wb[-1]

    z_m = jnp.zeros((1, LANES), F32)
    lax.fori_loop(0, nc, scan_body, (z_m, z_m))

    ng = ng_ref[...]

    def fin_body(c, carry):
        base = pl.multiple_of(c * L, L)
        h = hf[pl.ds(base, L), :] + hb[pl.ds(base, L), :]
        ms = jnp.mean(h * h, axis=-1, keepdims=True)
        y = h * lax.rsqrt(ms + EPS) * ng
        y = y * _sigmoid(og_ref[pl.ds(base, L), :].astype(F32))
        out_ref[pl.ds(base, L), :] = y.astype(out_ref.dtype)
        return carry

    lax.fori_loop(0, nc, fin_body, 0)


def _mlstm(proj, gates_rows, bg_rows, conv_w, conv_b, norm_g, B, S):
    T = B * S
    L = min(ML_CHUNK, S)
    assert S % L == 0 and L % LANES == 0
    q0 = 3 * NA_WIDTH // ML_QK_DIM
    k0 = q0 + ML_HEADS
    v0 = (3 * NA_WIDTH + 2 * ML_QK_WIDTH) // ML_V_DIM
    o0 = v0 + ML_HEADS
    kern = functools.partial(_mlstm_kernel, S=S, L=L)
    return pl.pallas_call(
        kern,
        grid=(B, ML_HEADS),
        in_specs=[
            pl.BlockSpec((S, ML_QK_DIM), lambda b, h: (b, q0 + h)),
            pl.BlockSpec((S, ML_QK_DIM), lambda b, h: (b, k0 + h)),
            pl.BlockSpec((S, ML_V_DIM), lambda b, h: (b, v0 + h)),
            pl.BlockSpec((S, ML_V_DIM), lambda b, h: (b, o0 + h)),
            pl.BlockSpec((None, None, 8, S), lambda b, h: (b, h, 0, 0)),
            pl.BlockSpec((None, 8, 1), lambda b, h: (h, 0, 0)),
            pl.BlockSpec((CONV_W, ML_QK_DIM), lambda b, h: (0, h)),
            pl.BlockSpec((CONV_W, ML_QK_DIM), lambda b, h: (0, ML_HEADS + h)),
            pl.BlockSpec((1, ML_QK_DIM), lambda b, h: (0, h)),
            pl.BlockSpec((1, ML_QK_DIM), lambda b, h: (0, ML_HEADS + h)),
            pl.BlockSpec((1, ML_V_DIM), lambda b, h: (0, h)),
        ],
        out_specs=pl.BlockSpec((S, ML_V_DIM), lambda b, h: (b, h)),
        out_shape=jax.ShapeDtypeStruct((T, ML_V_WIDTH), BF16),
        scratch_shapes=[
            pltpu.VMEM((S, ML_QK_DIM), BF16),
            pltpu.VMEM((ML_QK_DIM, S), BF16),
            pltpu.VMEM((8, S), F32),
            pltpu.VMEM((6, S, LANES), F32),
            pltpu.VMEM((S, ML_V_DIM), F32),
            pltpu.VMEM((S, ML_V_DIM), F32),
            pltpu.VMEM((2, ML_QK_DIM, ML_V_DIM), F32),
            pltpu.VMEM((2, ML_QK_DIM, LANES), F32),
        ],
        compiler_params=_cparams(("parallel", "arbitrary")),
        name="mlstm",
    )(proj, proj, proj, proj, gates_rows, bg_rows, conv_w, conv_w, conv_b, conv_b, norm_g)


def _out_proj_kernel(na_ref, ml_ref, x_ref, w_ref, nag_ref, fg_ref, wr_ref, xo_ref, h_ref, aff_ref):
    na = na_ref[...].astype(F32)
    ms = jnp.mean(na * na, axis=-1, keepdims=True)
    na_n = (na * lax.rsqrt(ms + EPS) * nag_ref[...]).astype(BF16)
    upd = jnp.dot(na_n, w_ref[0:NA_WIDTH, :], preferred_element_type=F32)
    upd = upd + jnp.dot(ml_ref[...], w_ref[NA_WIDTH:, :], preferred_element_type=F32)
    xn = x_ref[...] + upd
    xo_ref[...] = xn
    ms2 = jnp.mean(xn * xn, axis=-1, keepdims=True)
    h = xn * lax.rsqrt(ms2 + EPS) * fg_ref[...]
    h_ref[...] = h
    logits = jnp.dot(h.astype(BF16), wr_ref[...], preferred_element_type=F32)
    lane = lax.broadcasted_iota(jnp.int32, logits.shape, 1)
    logits = jnp.where(lane < N_EXPERTS, logits, NEG)
    m = jnp.max(logits, axis=-1, keepdims=True)
    e = jnp.exp(logits - m)
    aff_ref[...] = e / jnp.sum(e, axis=-1, keepdims=True)


def _out_proj(na_o, ml_o, x2d, w_out, na_g, ffn_g, w_router, tm=512):
    T, D = x2d.shape
    return pl.pallas_call(
        _out_proj_kernel,
        grid=(T // tm,),
        in_specs=[
            pl.BlockSpec((tm, NA_WIDTH), lambda i: (i, 0)),
            pl.BlockSpec((tm, ML_V_WIDTH), lambda i: (i, 0)),
            pl.BlockSpec((tm, D), lambda i: (i, 0)),
            pl.BlockSpec((NA_WIDTH + ML_V_WIDTH, D), lambda i: (0, 0)),
            pl.BlockSpec((1, NA_WIDTH), lambda i: (0, 0)),
            pl.BlockSpec((1, D), lambda i: (0, 0)),
            pl.BlockSpec((D, LANES), lambda i: (0, 0)),
        ],
        out_specs=[
            pl.BlockSpec((tm, D), lambda i: (i, 0)),
            pl.BlockSpec((tm, D), lambda i: (i, 0)),
            pl.BlockSpec((tm, LANES), lambda i: (i, 0)),
        ],
        out_shape=[
            jax.ShapeDtypeStruct((T, D), F32),
            jax.ShapeDtypeStruct((T, D), F32),
            jax.ShapeDtypeStruct((T, LANES), F32),
        ],
        compiler_params=_cparams(("parallel",)),
        name="out_proj",
    )(na_o, ml_o, x2d, w_out, na_g, ffn_g, w_router)


I32 = jnp.int32
SLOT_LO = 32
TOK_LO = 64
CHUNK = 256


def _route_kernel(aff_ref, tok_ref, dst_ref, gate_ref, off_ref, cnt_ref, a_scr, sel_scr, pos_scr, tmp_scr,
                  *, S, cap):
    b = pl.program_id(0)
    E = N_EXPERTS
    nchunk = S // CHUNK
    zrows = E * cap
    n_hi = cap // SLOT_LO

    TB = 512
    for c in range(S // TB):
        a_scr[:, c * TB:(c + 1) * TB] = aff_ref[c * TB:(c + 1) * TB, :].T[0:E, :]
    A = a_scr[...]

    def thr_body(i, thr):
        cand = thr | jnp.left_shift(jnp.int32(1), 30 - i)
        cand_f = lax.bitcast_convert_type(cand, F32)
        cnt = jnp.sum((a_scr[...] >= cand_f).astype(F32), axis=1, keepdims=True)
        return jnp.where(cnt >= cap, cand, thr)

    thr = lax.fori_loop(0, 31, thr_body, jnp.zeros((E, 1), I32))
    thr_f = lax.bitcast_convert_type(thr, F32)
    gt = A > thr_f
    eq = A == thr_f
    need = cap - jnp.sum(gt.astype(F32), axis=1, keepdims=True)
    idx = lax.broadcasted_iota(I32, (E, S), 1)

    nbits = S.bit_length() - 1

    def tie_body(i, jt):
        cand = jt | jnp.left_shift(jnp.int32(1), nbits - 1 - i)
        c = jnp.sum((eq & (idx < cand)).astype(F32), axis=1, keepdims=True)
        return jnp.where(c < need, cand, jt)

    jt = lax.fori_loop(0, nbits, tie_body, jnp.zeros((E, 1), I32))
    sel = (gt | (eq & (idx <= jt))).astype(F32)
    sel_scr[...] = sel

    r_i = lax.broadcasted_iota(I32, (CHUNK, CHUNK), 0)
    c_i = lax.broadcasted_iota(I32, (CHUNK, CHUNK), 1)
    strict_upper = (r_i < c_i).astype(BF16)
    e_r = lax.broadcasted_iota(I32, (E, E), 0)
    e_c = lax.broadcasted_iota(I32, (E, E), 1)
    strict_lower = (e_c < e_r).astype(BF16)
    run = jnp.zeros((E, 1), F32)
    run_t = jnp.zeros((1, 1), F32)
    for c in range(nchunk):
        sc = sel[:, c * CHUNK:(c + 1) * CHUNK]
        scb = sc.astype(BF16)
        pos_scr[:, c * CHUNK:(c + 1) * CHUNK] = jnp.dot(scb, strict_upper, preferred_element_type=F32) + run
        run = run + jnp.sum(sc, axis=1, keepdims=True)
        cnt_c = jnp.sum(sc, axis=0, keepdims=True)
        cnt8 = jnp.broadcast_to(cnt_c, (8, CHUNK)).astype(BF16)
        base_c = jnp.dot(cnt8, strict_upper, preferred_element_type=F32)[0:1] + run_t
        run_t = run_t + jnp.sum(cnt_c, axis=1, keepdims=True)
        rank_c = jnp.dot(strict_lower, scb, preferred_element_type=F32)
        zoff = (b * zrows).astype(F32)
        tmp_scr[:, c * CHUNK:(c + 1) * CHUNK] = base_c + rank_c + zoff
        off_ref[:, c * CHUNK:(c + 1) * CHUNK] = (base_c + zoff).astype(I32)
        cnt_ref[:, c * CHUNK:(c + 1) * CHUNK] = cnt_c.astype(I32)

    lo_id = lax.broadcasted_iota(I32, (SLOT_LO, S), 0).astype(F32)
    hi_id = lax.broadcasted_iota(I32, (n_hi, S), 0).astype(F32)
    t_row = lax.broadcasted_iota(I32, (1, S), 1)
    t_hi = (t_row // TOK_LO).astype(F32)
    t_lo = (t_row % TOK_LO).astype(F32)

    def split3(x):
        hi = x.astype(BF16).astype(F32)
        mid = (x - hi).astype(BF16).astype(F32)
        lo = (x - hi - mid).astype(BF16).astype(F32)
        return hi, mid, lo

    for e in range(E):
        sel_e = sel_scr[e:e + 1, :]
        pos_e = pos_scr[e:e + 1, :]
        p_hi = jnp.floor(pos_e * (1.0 / SLOT_LO))
        p_lo = pos_e - p_hi * SLOT_LO
        oh_lo = jnp.where((lo_id == p_lo) & (sel_e > 0), 1.0, 0.0).astype(BF16)
        oh_hi = jnp.where(hi_id == p_hi, 1.0, 0.0)
        d = tmp_scr[e:e + 1, :]
        d2 = jnp.floor(d * (1.0 / 1024.0))
        d1 = jnp.floor((d - d2 * 1024.0) * (1.0 / 32.0))
        d0 = d - d2 * 1024.0 - d1 * 32.0
        g_hi, g_mid, g_lo = split3(a_scr[e:e + 1, :])
        vals = (t_hi, t_lo, d2, d1, d0, g_hi, g_mid, g_lo)
        stack = jnp.concatenate([oh_hi * v for v in vals], axis=0).astype(BF16)
        r = lax.dot_general(stack, oh_lo, (((1,), (1,)), ((), ())), preferred_element_type=F32)
        rr = [r[k * n_hi:(k + 1) * n_hi] for k in range(8)]
        tok_ref[e] = (rr[0] * TOK_LO + rr[1]).astype(I32) + b * S
        dst_ref[e] = (rr[2] * 1024.0 + rr[3] * 32.0 + rr[4]).astype(I32)
        gate_ref[e] = rr[5] + rr[6] + rr[7]


def _route(aff, B, S):
    cap = CAPACITY_FACTOR * S // N_EXPERTS
    n_hi = cap // SLOT_LO
    E = N_EXPERTS
    kern = functools.partial(_route_kernel, S=S, cap=cap)
    lists = jax.ShapeDtypeStruct((B, E, n_hi, SLOT_LO), I32)
    return pl.pallas_call(
        kern,
        grid=(B,),
        in_specs=[pl.BlockSpec((S, LANES), lambda b: (b, 0))],
        out_specs=[
            pl.BlockSpec((None, E, n_hi, SLOT_LO), lambda b: (b, 0, 0, 0)),
            pl.BlockSpec((None, E, n_hi, SLOT_LO), lambda b: (b, 0, 0, 0)),
            pl.BlockSpec((None, E, n_hi, SLOT_LO), lambda b: (b, 0, 0, 0)),
            pl.BlockSpec((None, 1, S), lambda b: (b, 0, 0)),
            pl.BlockSpec((None, 1, S), lambda b: (b, 0, 0)),
        ],
        out_shape=[lists, lists, jax.ShapeDtypeStruct((B, E, n_hi, SLOT_LO), F32),
                   jax.ShapeDtypeStruct((B, 1, S), I32), jax.ShapeDtypeStruct((B, 1, S), I32)],
        scratch_shapes=[pltpu.VMEM((E, S), F32)] * 4,
        compiler_params=_cparams(("parallel",)),
        name="route",
    )(aff)


FFN_ROWS = 128
FFN_STEPS = 4


def _ffn_kernel(tok_ref, dst_ref, h_hbm, wg_ref, wu_ref, wd_ref, gate_ref, z_hbm, xb, act, yacc, sem_g, sem_s,
                *, tm):
    s = pl.program_id(2)
    tn = yacc.shape[1] // FFN_STEPS

    def gather_row(r):
        return pltpu.make_async_copy(h_hbm.at[pl.ds(tok_ref[0, r], 1), :], yacc.at[pl.ds(r, 1), :], sem_g)

    def scatter_row(r):
        return pltpu.make_async_copy(yacc.at[pl.ds(r, 1), :], z_hbm.at[pl.ds(dst_ref[0, r], 1), :], sem_s)

    def for_rows(fn):
        def body(r, c):
            fn(r)
            return c
        lax.fori_loop(0, tm, body, 0, unroll=8)

    @pl.when(s == 0)
    def _():
        for_rows(lambda r: gather_row(r).start())
        for_rows(lambda r: gather_row(r).wait())

        def cast(j, c):
            rows = pl.ds(pl.multiple_of(j * FFN_ROWS, FFN_ROWS), FFN_ROWS)
            xb[rows, :] = yacc[rows, :].astype(BF16)
            return c
        lax.fori_loop(0, tm // FFN_ROWS, cast, 0)

    @pl.when(s < FFN_STEPS)
    def _():
        x = xb[...]
        a = jnp.dot(x, wg_ref[...].astype(BF16), preferred_element_type=F32)
        u = jnp.dot(x, wu_ref[...].astype(BF16), preferred_element_type=F32)
        act[s] = (a * _sigmoid(a) * u).astype(BF16)

    @pl.when(s >= FFN_STEPS)
    def _():
        a_all = jnp.concatenate([act[f] for f in range(FFN_STEPS)], axis=1)
        y = jnp.dot(a_all, wd_ref[...].astype(BF16), preferred_element_type=F32)
        col = pl.multiple_of((s - FFN_STEPS) * tn, tn)
        yacc[:, pl.ds(col, tn)] = y * gate_ref[...]

    @pl.when(s == 2 * FFN_STEPS - 1)
    def _():
        for_rows(lambda r: scatter_row(r).start())
        for_rows(lambda r: scatter_row(r).wait())


def _expert_ffn(h2, tok, dst, gate, w_gate, w_up, w_down, layer, tm=1024):
    E, M = tok.shape
    T, D = h2.shape
    F = w_gate.shape[-1]
    tm = min(tm, M)
    nm = M // tm
    tf = F // FFN_STEPS
    tn = D // FFN_STEPS
    last = FFN_STEPS - 1
    tok3 = tok.reshape(E * nm, 1, tm)
    dst3 = dst.reshape(E * nm, 1, tm)
    smem_spec = pl.BlockSpec((None, 1, tm), lambda e, m, s: (e * nm + m, 0, 0), memory_space=pltpu.SMEM)
    kern = functools.partial(_ffn_kernel, tm=tm)
    return pl.pallas_call(
        kern,
        grid=(E, nm, 2 * FFN_STEPS),
        in_specs=[
            smem_spec,
            smem_spec,
            pl.BlockSpec(memory_space=pl.ANY),
            pl.BlockSpec((None, None, D, tf), lambda e, m, s: (layer, e, 0, jnp.minimum(s, last))),
            pl.BlockSpec((None, None, D, tf), lambda e, m, s: (layer, e, 0, jnp.minimum(s, last))),
            pl.BlockSpec((None, None, F, tn), lambda e, m, s: (layer, e, 0, jnp.maximum(s - FFN_STEPS, 0))),
            pl.BlockSpec((None, tm, 1), lambda e, m, s: (e, m, 0)),
        ],
        out_specs=pl.BlockSpec(memory_space=pl.ANY),
        out_shape=jax.ShapeDtypeStruct((E * M, D), F32),
        scratch_shapes=[
            pltpu.VMEM((tm, D), BF16),
            pltpu.VMEM((FFN_STEPS, tm, tf), BF16),
            pltpu.VMEM((tm, D), F32),
            pltpu.SemaphoreType.DMA(()),
            pltpu.SemaphoreType.DMA(()),
        ],
        compiler_params=_cparams(("arbitrary", "arbitrary", "arbitrary")),
        name="expert_ffn",
    )(tok3, dst3, h2, w_gate, w_up, w_down, gate)


COMB_TOK = 512
COMB_ROWS = 256


def _combine_kernel(start_ref, nch_ref, x_ref, off_ref, cnt_ref, g_ref, z_hbm, o_ref, zbuf, sem, st,
                    *, zrows, final):
    i = pl.program_id(0)
    nt = pl.num_programs(0)
    start = start_ref[i]
    n = nch_ref[i]
    i_next = jnp.minimum(i + 1, nt - 1)
    next_ready = (i + 1 < nt) & (nch_ref[i_next] > 0)

    @pl.when(i == 0)
    def _():
        st[0] = 0
        st[1] = 0

    gc = st[0]
    o_ref[...] = x_ref[...]
    off = off_ref[...]
    end = off + cnt_ref[...]

    def chunk_start(tile, k):
        return pl.multiple_of(jnp.minimum(start_ref[tile] + k * COMB_ROWS, zrows - COMB_ROWS), 8)

    def copy(tile, k, slot):
        return pltpu.make_async_copy(z_hbm.at[pl.ds(chunk_start(tile, k), COMB_ROWS), :], zbuf.at[slot],
                                     sem.at[slot])

    @pl.when((n > 0) & (st[1] == 0))
    def _():
        copy(i, 0, gc & 1).start()

    def body(k, c):
        slot = (gc + k) & 1
        copy(i, k, slot).wait()

        @pl.when(k + 1 < n)
        def _():
            copy(i, k + 1, 1 - slot).start()

        @pl.when((k + 1 == n) & next_ready)
        def _():
            copy(i_next, 0, 1 - slot).start()

        rows = chunk_start(i, k) + lax.broadcasted_iota(I32, (1, COMB_ROWS), 1)
        lo = jnp.maximum(off, start + k * COMB_ROWS)
        seg = jnp.where((rows >= lo) & (rows < end), 1.0, 0.0).astype(BF16)
        z = zbuf[slot]
        z_hi = z.astype(BF16)
        z_lo = (z - z_hi.astype(F32)).astype(BF16)
        o_ref[...] += jnp.dot(seg, z_hi, preferred_element_type=F32) + jnp.dot(seg, z_lo, preferred_element_type=F32)
        return c

    lax.fori_loop(0, n, body, 0)
    st[0] = gc + n
    st[1] = ((n > 0) & next_ready).astype(I32)

    if final:
        x = o_ref[...]
        ms = jnp.mean(x * x, axis=-1, keepdims=True)
        o_ref[...] = x * lax.rsqrt(ms + EPS) * g_ref[...]


def _combine(x2d, z, off, cnt, gain, final):
    T, D = x2d.shape
    zrows = z.shape[0]
    tt = min(COMB_TOK, T)
    nt = T // tt
    start = (off[::tt] // 8) * 8
    last = (off + cnt)[tt - 1::tt]
    nch = (last - start + COMB_ROWS - 1) // COMB_ROWS
    kern = functools.partial(_combine_kernel, zrows=zrows, final=final)
    grid_spec = pltpu.PrefetchScalarGridSpec(
        num_scalar_prefetch=2,
        grid=(nt,),
        in_specs=[
            pl.BlockSpec((tt, D), lambda i, s, n: (i, 0)),
            pl.BlockSpec((tt, 1), lambda i, s, n: (i, 0)),
            pl.BlockSpec((tt, 1), lambda i, s, n: (i, 0)),
            pl.BlockSpec((1, D), lambda i, s, n: (0, 0)),
            pl.BlockSpec(memory_space=pl.ANY),
        ],
        out_specs=pl.BlockSpec((tt, D), lambda i, s, n: (i, 0)),
        scratch_shapes=[pltpu.VMEM((2, COMB_ROWS, D), F32), pltpu.SemaphoreType.DMA((2,)),
                        pltpu.SMEM((2,), I32)],
    )
    return pl.pallas_call(
        kern,
        grid_spec=grid_spec,
        out_shape=jax.ShapeDtypeStruct((T, D), F32),
        compiler_params=_cparams(("arbitrary",)),
        name="combine",
    )(start.astype(I32), nch.astype(I32), x2d, off[:, None], cnt[:, None], gain, z)


def moe(x2d, h2, aff, w_gate, w_up, w_down, layer, B, S, gain, final):
    E = N_EXPERTS
    cap = CAPACITY_FACTOR * S // E
    tok, dst, gate, off, cnt = _route(aff, B, S)
    to_em = lambda a: jnp.swapaxes(a.reshape(B, E, cap), 0, 1).reshape(E, B * cap)
    z = _expert_ffn(h2, to_em(tok), to_em(dst), to_em(gate)[:, :, None], w_gate, w_up, w_down, layer)
    return _combine(x2d, z, off.reshape(B * S), cnt.reshape(B * S), gain, final)


def kernel(x, norm_mix_g, w_in, b_gates, conv_w, conv_b, na_rpb, na_norm_g, ml_norm_g, w_out,
           norm_ffn_g, w_router, w_gate, w_up, w_down, final_norm_g):
    B, S, D = x.shape
    T = B * S
    depth = w_in.shape[0]
    x2d = x.reshape(T, D)

    for l in range(depth):
        w_main = w_in[l, :, :N_MAIN].astype(BF16)
        w_gates = jnp.pad(w_in[l, :, N_MAIN:], ((0, 0), (0, LANES - N_GATES))).astype(BF16)
        proj, gates = _in_proj(x2d, norm_mix_g[l][None, :], w_main, w_gates)

        tb = _na_bias_table(na_rpb[l])
        na_o = _na_attention(proj, tb, B, S)

        g4 = gates[:, :N_GATES].reshape(B, S, 4, ML_HEADS)
        g_rows = jnp.pad(jnp.transpose(g4, (0, 3, 2, 1)), ((0, 0), (0, 0), (0, 4), (0, 0)))
        bg_rows = jnp.pad(b_gates[l].reshape(4, ML_HEADS).T, ((0, 0), (0, 4)))[:, :, None]
        ml_o = _mlstm(proj, g_rows, bg_rows.astype(F32), conv_w[l], conv_b[l][None, :],
                      ml_norm_g[l][None, :], B, S)

        w_r = jnp.pad(w_router[l], ((0, 0), (0, LANES - N_EXPERTS))).astype(BF16)
        x2d, h2, aff = _out_proj(na_o, ml_o, x2d, w_out[l].astype(BF16), na_norm_g[l][None, :],
                                 norm_ffn_g[l][None, :], w_r)

        x2d = moe(x2d, h2, aff, w_gate, w_up, w_down, l, B, S, final_norm_g[None, :], l == depth - 1)

    return x2d.reshape(B, S, D)
```

```python
import functools

import numpy as np
import jax
import jax.numpy as jnp
from jax import lax
from jax.experimental import pallas as pl
from jax.experimental.pallas import tpu as pltpu

F32 = jnp.float32
BF16 = jnp.bfloat16

EPS = 1e-6
GRID_W = 64
NA_HEADS = 8
NA_HEAD_DIM = 128
NA_KR = 8
NA_KC = 16
NA_WIDTH = NA_HEADS * NA_HEAD_DIM
NA_LOOKAHEAD = 6
ML_HEADS = 4
ML_QK_DIM = 128
ML_V_DIM = 256
ML_QK_WIDTH = ML_HEADS * ML_QK_DIM
ML_V_WIDTH = ML_HEADS * ML_V_DIM
CONV_W = 5
N_GATES = 4 * ML_HEADS
N_EXPERTS = 16
CAPACITY_FACTOR = 2
N_MAIN = 3 * NA_WIDTH + 2 * ML_QK_WIDTH + 2 * ML_V_WIDTH
LANES = 128
NEG = -1e30
ML_CHUNK = 256
VMEM_LIMIT = 56 * 1024 * 1024


def _cparams(sem):
    return pltpu.CompilerParams(dimension_semantics=sem, vmem_limit_bytes=VMEM_LIMIT)


def _in_proj_kernel(x_ref, g_ref, w_ref, wg_ref, o_ref, og_ref, h_scr):
    @pl.when(pl.program_id(1) == 0)
    def _():
        x = x_ref[...]
        ms = jnp.mean(x * x, axis=-1, keepdims=True)
        h = (x * lax.rsqrt(ms + EPS) * g_ref[...]).astype(BF16)
        h_scr[...] = h
        og_ref[...] = jnp.dot(h, wg_ref[...], preferred_element_type=F32)

    o_ref[...] = jnp.dot(h_scr[...], w_ref[...], preferred_element_type=F32).astype(o_ref.dtype)


def _in_proj(x2d, g, w_main, w_gates, tm=1024, tn=2048):
    T, D = x2d.shape
    N = w_main.shape[1]
    return pl.pallas_call(
        _in_proj_kernel,
        grid=(T // tm, N // tn),
        in_specs=[
            pl.BlockSpec((tm, D), lambda i, j: (i, 0)),
            pl.BlockSpec((1, D), lambda i, j: (0, 0)),
            pl.BlockSpec((D, tn), lambda i, j: (0, j)),
            pl.BlockSpec((D, LANES), lambda i, j: (0, 0)),
        ],
        out_specs=[
            pl.BlockSpec((tm, tn), lambda i, j: (i, j)),
            pl.BlockSpec((tm, LANES), lambda i, j: (i, 0)),
        ],
        out_shape=[
            jax.ShapeDtypeStruct((T, N), BF16),
            jax.ShapeDtypeStruct((T, LANES), F32),
        ],
        scratch_shapes=[pltpu.VMEM((tm, D), BF16)],
        compiler_params=_cparams(("parallel", "arbitrary")),
        name="in_proj",
    )(x2d, g, w_main, w_gates)


def _na_bias_table(rpb):
    n_dc = 2 * NA_KC - 1
    off = np.arange(NA_KR)[:, None]
    a = np.arange(NA_KR)[None, :]
    dr = a - off + (NA_KR - 1)
    c = np.arange(GRID_W)
    col_start = np.clip(c - NA_KC // 2, 0, GRID_W - NA_KC)
    col_in = (c[None, :] >= col_start[:, None]) & (c[None, :] < col_start[:, None] + NA_KC)
    dc = np.clip(c[None, :] - c[:, None] + (NA_KC - 1), 0, n_dc - 1)
    sel_c = jnp.asarray(dc[:, :, None] == np.arange(n_dc), F32)
    x = jnp.einsum("hij,qkj->hiqk", rpb.astype(F32), sel_c, precision=lax.Precision.HIGHEST)
    tb = jnp.transpose(x[:, dr], (0, 1, 3, 2, 4))
    tb = jnp.where(col_in[None, None, :, None, :], tb, NEG)
    return tb.reshape(rpb.shape[0], NA_KR, GRID_W, NA_KR * GRID_W)


def _na_kernel(q_ref, k_ref, v_ref, tb_ref, o_ref, *, rb, rows):
    i = pl.program_id(2)
    scale = NA_HEAD_DIM ** -0.5
    win = NA_KR * GRID_W

    def scores(rr):
        r = i * rb + rr
        rs = jnp.clip(r - NA_KR // 2, 0, rows - NA_KR)
        start = pl.multiple_of(rs * GRID_W, GRID_W)
        q = q_ref[rr * GRID_W:(rr + 1) * GRID_W, :]
        kw = k_ref[pl.ds(start, win), :]
        s = lax.dot_general(q, kw, (((1,), (1,)), ((), ())), preferred_element_type=F32)
        return s, r - rs, start

    queue = [scores(rr) for rr in range(min(NA_LOOKAHEAD, rb))]
    for rr in range(rb):
        s, off, start = queue.pop(0)
        if rr + NA_LOOKAHEAD < rb:
            queue.append(scores(rr + NA_LOOKAHEAD))
        vw = v_ref[pl.ds(start, win), :]
        s = s * scale + tb_ref[off]
        m = jnp.max(s, axis=-1, keepdims=True)
        p = jnp.exp(s - m)
        l = jnp.sum(p, axis=-1, keepdims=True)
        o = jnp.dot(p.astype(BF16), vw, preferred_element_type=F32)
        o_ref[rr * GRID_W:(rr + 1) * GRID_W, :] = (o / l).astype(o_ref.dtype)


def _na_attention(proj, tb, B, S, rb=16):
    T = B * S
    rows = S // GRID_W
    assert rows >= NA_KR and rows % rb == 0
    nblk = rows // rb
    tq = rb * GRID_W
    kern = functools.partial(_na_kernel, rb=rb, rows=rows)
    return pl.pallas_call(
        kern,
        grid=(B, NA_HEADS, nblk),
        in_specs=[
            pl.BlockSpec((tq, NA_HEAD_DIM), lambda b, h, i: (b * nblk + i, h)),
            pl.BlockSpec((S, NA_HEAD_DIM), lambda b, h, i: (b, NA_HEADS + h)),
            pl.BlockSpec((S, NA_HEAD_DIM), lambda b, h, i: (b, 2 * NA_HEADS + h)),
            pl.BlockSpec((None, NA_KR, GRID_W, NA_KR * GRID_W), lambda b, h, i: (h, 0, 0, 0)),
        ],
        out_specs=pl.BlockSpec((tq, NA_HEAD_DIM), lambda b, h, i: (b * nblk + i, h)),
        out_shape=jax.ShapeDtypeStruct((T, NA_WIDTH), BF16),
        compiler_params=_cparams(("parallel", "parallel", "arbitrary")),
        name="na_attention",
    )(proj, proj, proj, tb)


def _log_sigmoid(x):
    return jnp.minimum(x, 0.0) - jnp.log(1.0 + jnp.exp(-jnp.abs(x)))


def _sigmoid(x):
    return 1.0 / (1.0 + jnp.exp(-x))


def _mlstm_kernel(q_ref, k_ref, v_ref, og_ref, g_ref, bg_ref, cwq_ref, cwk_ref, cbq_ref, cbk_ref,
                  ng_ref, out_ref, qs, kst, rrow, bcol, hf, hb, cst, nst, *, S, L):
    nc = S // L
    halo = 8

    def conv_silu(src_ref, cw, cbv, c, scale):
        base = pl.multiple_of(c * L, L)
        lo = jnp.maximum(base - halo, 0)
        hi = jnp.minimum(base + L, S - halo)
        prev = src_ref[pl.ds(pl.multiple_of(lo, halo), halo), :].astype(F32)
        nxt = src_ref[pl.ds(pl.multiple_of(hi, halo), halo), :].astype(F32)
        prev = jnp.where(c > 0, prev, 0.0)
        nxt = jnp.where(c < nc - 1, nxt, 0.0)
        main = src_ref[pl.ds(base, L), :].astype(F32)
        blk = jnp.concatenate([prev, main, nxt], axis=0)
        n = L + 2 * halo
        acc = jnp.zeros((L, LANES), F32) + cbv
        for w in range(CONV_W):
            sh = (CONV_W // 2 - w) % n
            xs = blk if sh == 0 else pltpu.roll(blk, sh, 0)
            acc = acc + xs[halo:halo + L, :] * cw[w:w + 1, :]
        return acc * _sigmoid(acc) * scale

    cwq, cbq, cwk, cbk = cwq_ref[...], cbq_ref[...], cwk_ref[...], cbk_ref[...]

    def conv_body(c, carry):
        base = pl.multiple_of(c * L, L)
        qs[pl.ds(base, L), :] = conv_silu(q_ref, cwq, cbq, c, ML_QK_DIM ** -0.5).astype(BF16)
        kst[:, pl.ds(base, L)] = conv_silu(k_ref, cwk, cbk, c, 1.0).T.astype(BF16)
        return carry

    lax.fori_loop(0, nc, conv_body, 0)

    t_idx = lax.broadcasted_iota(jnp.int32, (L, L), 0)
    j_idx = lax.broadcasted_iota(jnp.int32, (L, L), 1)
    upper = (t_idx <= j_idx).astype(BF16)
    lower = (t_idx >= j_idx).astype(BF16)

    def split3(x):
        hi = x.astype(BF16).astype(F32)
        mid = (x - hi).astype(BF16).astype(F32)
        lo = (x - hi - mid).astype(BF16).astype(F32)
        return hi, mid, lo

    def gate_body(c, carry):
        base = pl.multiple_of(c * L, L)
        g = g_ref[:, pl.ds(base, L)] + bg_ref[...]
        hi, mid, lo = split3(_log_sigmoid(g))
        lf3 = jnp.concatenate([hi, mid, lo, jnp.zeros_like(hi)], axis=0).astype(BF16)
        pre = jnp.dot(lf3, upper, preferred_element_type=F32)
        suf = jnp.dot(lf3, lower, preferred_element_type=F32)
        pre = pre[0:8] + pre[8:16] + pre[16:24]
        suf = suf[0:8] + suf[8:16] + suf[16:24]
        cum_f, cum_b = pre[1:2], suf[3:4]
        gx_f = cum_f[:, L - 1:L] - cum_f + g[0:1]
        gx_b = cum_b[:, 0:1] - cum_b + g[2:3]
        r = jnp.concatenate([g[0:1], cum_f, g[2:3], cum_b, gx_f, gx_b, g[0:1] - cum_f, g[2:3] - cum_b],
                            axis=0)
        rrow[:, pl.ds(base, L)] = r
        return carry

    lax.fori_loop(0, nc, gate_body, 0, unroll=4)

    r_all = rrow[...]
    row_all = lax.broadcasted_iota(jnp.int32, (8, S), 0)
    lane_all = lax.broadcasted_iota(jnp.int32, (8, S), 1) & (L - 1)
    pm = r_all
    sm = r_all
    sh = 1
    while sh < L:
        pm = jnp.maximum(pm, jnp.where(lane_all >= sh, pltpu.roll(pm, sh, 1), NEG))
        sm = jnp.maximum(sm, jnp.where(lane_all < L - sh, pltpu.roll(sm, S - sh, 1), NEG))
        sh *= 2
    mi = jnp.where(row_all == 6, pm + pltpu.roll(r_all, 5, 0), sm + pltpu.roll(r_all, 4, 0))
    rrow[...] = jnp.where(row_all >= 6, mi, r_all)

    pick_r = lax.broadcasted_iota(jnp.int32, (LANES, 2 * LANES), 0)
    pick_l = lax.broadcasted_iota(jnp.int32, (LANES, 2 * LANES), 1)

    def picker(k_lo, k_hi):
        want = jnp.where(pick_l < LANES, k_lo, k_hi)
        return ((pick_r < 24) & ((pick_r & 7) == want)).astype(BF16)

    pairs = ((1, 6, 0, 1), (4, 3, 2, 3), (7, 5, 4, 5))

    def bcol_body(c, carry):
        base = pl.multiple_of(c * L, L)
        hi, mid, lo = split3(rrow[:, pl.ds(base, L)])
        x = jnp.concatenate([hi, mid, lo, jnp.zeros((LANES - 24, L), F32)], axis=0)
        xt = x.T.astype(BF16)
        for k_lo, k_hi, d_lo, d_hi in pairs:
            y = jnp.dot(xt, picker(k_lo, k_hi), preferred_element_type=F32)
            bcol[d_lo, pl.ds(base, L), :] = y[:, :LANES]
            bcol[d_hi, pl.ds(base, L), :] = y[:, LANES:]
        return carry

    lax.fori_loop(0, nc, bcol_body, 0, unroll=4)

    fwd_mask = j_idx <= t_idx
    bwd_mask = j_idx >= t_idx
    ones_b = jnp.ones((L, LANES), BF16)

    def wide(x, n):
        return jnp.concatenate([x] * n, axis=1)

    def load(c, d, total_row):
        base = pl.multiple_of(c * L, L)
        qc = qs[pl.ds(base, L), :]
        C = cst[d]
        cum = bcol[3 * d, pl.ds(base, L), :]
        return dict(
            base=base, qc=qc, vc=v_ref[pl.ds(base, L), :], kt=kst[:, pl.ds(base, L)],
            cum=cum, mi=bcol[3 * d + 1, pl.ds(base, L), :], gx=bcol[3 * d + 2, pl.ds(base, L), :],
            i_row=rrow[2 * d:2 * d + 1, pl.ds(base, L)],
            cum_row=rrow[2 * d + 1:2 * d + 2, pl.ds(base, L)],
            total=cum[total_row:total_row + 1, :], C=C,
            s=jnp.dot(qc, kst[:, pl.ds(base, L)], preferred_element_type=F32),
            qC=jnp.dot(qc, C.astype(BF16), preferred_element_type=F32),
            qn=jnp.dot(qc, nst[d].astype(BF16), preferred_element_type=F32))

    def weights(x, mask, m_st):
        m_inter = x["cum"] + m_st
        m_j = jnp.maximum(m_inter, x["mi"])
        dmat = jnp.where(mask, wide(x["cum"], L // LANES) - x["cum_row"] + x["i_row"], NEG)
        qkw = (x["s"] * jnp.exp(dmat - wide(m_j, L // LANES))).astype(BF16)
        inter = jnp.exp(m_inter - m_j)
        m_new = jnp.maximum(x["total"] + m_st, jnp.max(x["gx"], axis=0, keepdims=True))
        wk = jnp.exp(x["gx"] - m_new)
        decay = jnp.exp(x["total"] + m_st - m_new)
        vw = (wide(wk, ML_V_DIM // LANES) * x["vc"].astype(F32)).astype(BF16)
        return qkw, inter, m_j, wk.astype(BF16), vw, decay, m_new

    def finish(x, d, qkw, inter, m_j, wk, vw, decay, h_ref):
        base = x["base"]
        den = inter * x["qn"] + jnp.dot(qkw, ones_b, preferred_element_type=F32)
        rden = 1.0 / jnp.maximum(jnp.abs(den), jnp.exp(-m_j))
        num = wide(inter, ML_V_DIM // LANES) * x["qC"] + jnp.dot(qkw, x["vc"], preferred_element_type=F32)
        h_ref[pl.ds(base, L), :] = num * wide(rden, ML_V_DIM // LANES)
        cst[d] = wide(decay, ML_V_DIM // LANES) * x["C"] + jnp.dot(x["kt"], vw, preferred_element_type=F32)
        nst[d] = decay * nst[d] + jnp.dot(x["kt"], wk, preferred_element_type=F32)

    cst[...] = jnp.zeros(cst.shape, F32)
    nst[...] = jnp.zeros(nst.shape, F32)

    def scan_body(t, carry):
        m_f, m_b = carry
        xf = load(t, 0, L - 1)
        xb = load(nc - 1 - t, 1, 0)
        wf = weights(xf, fwd_mask, m_f)
        finish(xf, 0, *wf[:-1], hf)
        wb = weights(xb, bwd_mask, m_b)
        finish(xb, 1, *wb[:-1], hb)
        return wf[-1], wb[-1]

    z_m = jnp.zeros((1, LANES), F32)
    lax.fori_loop(0, nc, scan_body, (z_m, z_m))

    ng = ng_ref[...]

    def fin_body(c, carry):
        base = pl.multiple_of(c * L, L)
        h = hf[pl.ds(base, L), :] + hb[pl.ds(base, L), :]
        ms = jnp.mean(h * h, axis=-1, keepdims=True)
        y = h * lax.rsqrt(ms + EPS) * ng
        y = y * _sigmoid(og_ref[pl.ds(base, L), :].astype(F32))
        out_ref[pl.ds(base, L), :] = y.astype(out_ref.dtype)
        return carry

    lax.fori_loop(0, nc, fin_body, 0)


def _mlstm(proj, gates_rows, bg_rows, conv_w, conv_b, norm_g, B, S):
    T = B * S
    L = min(ML_CHUNK, S)
    assert S % L == 0 and L % LANES == 0
    q0 = 3 * NA_WIDTH // ML_QK_DIM
    k0 = q0 + ML_HEADS
    v0 = (3 * NA_WIDTH + 2 * ML_QK_WIDTH) // ML_V_DIM
    o0 = v0 + ML_HEADS
    kern = functools.partial(_mlstm_kernel, S=S, L=L)
    return pl.pallas_call(
        kern,
        grid=(B, ML_HEADS),
        in_specs=[
            pl.BlockSpec((S, ML_QK_DIM), lambda b, h: (b, q0 + h)),
            pl.BlockSpec((S, ML_QK_DIM), lambda b, h: (b, k0 + h)),
            pl.BlockSpec((S, ML_V_DIM), lambda b, h: (b, v0 + h)),
            pl.BlockSpec((S, ML_V_DIM), lambda b, h: (b, o0 + h)),
            pl.BlockSpec((None, None, 8, S), lambda b, h: (b, h, 0, 0)),
            pl.BlockSpec((None, 8, 1), lambda b, h: (h, 0, 0)),
            pl.BlockSpec((CONV_W, ML_QK_DIM), lambda b, h: (0, h)),
            pl.BlockSpec((CONV_W, ML_QK_DIM), lambda b, h: (0, ML_HEADS + h)),
            pl.BlockSpec((1, ML_QK_DIM), lambda b, h: (0, h)),
            pl.BlockSpec((1, ML_QK_DIM), lambda b, h: (0, ML_HEADS + h)),
            pl.BlockSpec((1, ML_V_DIM), lambda b, h: (0, h)),
        ],
        out_specs=pl.BlockSpec((S, ML_V_DIM), lambda b, h: (b, h)),
        out_shape=jax.ShapeDtypeStruct((T, ML_V_WIDTH), BF16),
        scratch_shapes=[
            pltpu.VMEM((S, ML_QK_DIM), BF16),
            pltpu.VMEM((ML_QK_DIM, S), BF16),
            pltpu.VMEM((8, S), F32),
            pltpu.VMEM((6, S, LANES), F32),
            pltpu.VMEM((S, ML_V_DIM), F32),
            pltpu.VMEM((S, ML_V_DIM), F32),
            pltpu.VMEM((2, ML_QK_DIM, ML_V_DIM), F32),
            pltpu.VMEM((2, ML_QK_DIM, LANES), F32),
        ],
        compiler_params=_cparams(("parallel", "arbitrary")),
        name="mlstm",
    )(proj, proj, proj, proj, gates_rows, bg_rows, conv_w, conv_w, conv_b, conv_b, norm_g)


def _out_proj_kernel(na_ref, ml_ref, x_ref, w_ref, nag_ref, fg_ref, wr_ref, xo_ref, h_ref, aff_ref):
    na = na_ref[...].astype(F32)
    ms = jnp.mean(na * na, axis=-1, keepdims=True)
    na_n = (na * lax.rsqrt(ms + EPS) * nag_ref[...]).astype(BF16)
    upd = jnp.dot(na_n, w_ref[0:NA_WIDTH, :], preferred_element_type=F32)
    upd = upd + jnp.dot(ml_ref[...], w_ref[NA_WIDTH:, :], preferred_element_type=F32)
    xn = x_ref[...] + upd
    xo_ref[...] = xn
    ms2 = jnp.mean(xn * xn, axis=-1, keepdims=True)
    h = xn * lax.rsqrt(ms2 + EPS) * fg_ref[...]
    h_ref[...] = h
    logits = jnp.dot(h.astype(BF16), wr_ref[...], preferred_element_type=F32)
    lane = lax.broadcasted_iota(jnp.int32, logits.shape, 1)
    logits = jnp.where(lane < N_EXPERTS, logits, NEG)
    m = jnp.max(logits, axis=-1, keepdims=True)
    e = jnp.exp(logits - m)
    aff_ref[...] = e / jnp.sum(e, axis=-1, keepdims=True)


def _out_proj(na_o, ml_o, x2d, w_out, na_g, ffn_g, w_router, tm=512):
    T, D = x2d.shape
    return pl.pallas_call(
        _out_proj_kernel,
        grid=(T // tm,),
        in_specs=[
            pl.BlockSpec((tm, NA_WIDTH), lambda i: (i, 0)),
            pl.BlockSpec((tm, ML_V_WIDTH), lambda i: (i, 0)),
            pl.BlockSpec((tm, D), lambda i: (i, 0)),
            pl.BlockSpec((NA_WIDTH + ML_V_WIDTH, D), lambda i: (0, 0)),
            pl.BlockSpec((1, NA_WIDTH), lambda i: (0, 0)),
            pl.BlockSpec((1, D), lambda i: (0, 0)),
            pl.BlockSpec((D, LANES), lambda i: (0, 0)),
        ],
        out_specs=[
            pl.BlockSpec((tm, D), lambda i: (i, 0)),
            pl.BlockSpec((tm, D), lambda i: (i, 0)),
            pl.BlockSpec((tm, LANES), lambda i: (i, 0)),
        ],
        out_shape=[
            jax.ShapeDtypeStruct((T, D), F32),
            jax.ShapeDtypeStruct((T, D), F32),
            jax.ShapeDtypeStruct((T, LANES), F32),
        ],
        compiler_params=_cparams(("parallel",)),
        name="out_proj",
    )(na_o, ml_o, x2d, w_out, na_g, ffn_g, w_router)


I32 = jnp.int32
SLOT_LO = 32
TOK_LO = 64
CHUNK = 256


def _route_kernel(aff_ref, tok_ref, dst_ref, gate_ref, off_ref, cnt_ref, a_scr, sel_scr, pos_scr, tmp_scr,
                  *, S, cap):
    b = pl.program_id(0)
    E = N_EXPERTS
    nchunk = S // CHUNK
    zrows = E * cap
    n_hi = cap // SLOT_LO

    TB = 512
    for c in range(S // TB):
        a_scr[:, c * TB:(c + 1) * TB] = aff_ref[c * TB:(c + 1) * TB, :].T[0:E, :]
    A = a_scr[...]

    def thr_body(i, thr):
        cand = thr | jnp.left_shift(jnp.int32(1), 30 - i)
        cand_f = lax.bitcast_convert_type(cand, F32)
        cnt = jnp.sum((a_scr[...] >= cand_f).astype(F32), axis=1, keepdims=True)
        return jnp.where(cnt >= cap, cand, thr)

    thr = lax.fori_loop(0, 31, thr_body, jnp.zeros((E, 1), I32))
    thr_f = lax.bitcast_convert_type(thr, F32)
    gt = A > thr_f
    eq = A == thr_f
    need = cap - jnp.sum(gt.astype(F32), axis=1, keepdims=True)
    idx = lax.broadcasted_iota(I32, (E, S), 1)

    nbits = S.bit_length() - 1

    def tie_body(i, jt):
        cand = jt | jnp.left_shift(jnp.int32(1), nbits - 1 - i)
        c = jnp.sum((eq & (idx < cand)).astype(F32), axis=1, keepdims=True)
        return jnp.where(c < need, cand, jt)

    jt = lax.fori_loop(0, nbits, tie_body, jnp.zeros((E, 1), I32))
    sel = (gt | (eq & (idx <= jt))).astype(F32)
    sel_scr[...] = sel

    r_i = lax.broadcasted_iota(I32, (CHUNK, CHUNK), 0)
    c_i = lax.broadcasted_iota(I32, (CHUNK, CHUNK), 1)
    strict_upper = (r_i < c_i).astype(BF16)
    e_r = lax.broadcasted_iota(I32, (E, E), 0)
    e_c = lax.broadcasted_iota(I32, (E, E), 1)
    strict_lower = (e_c < e_r).astype(BF16)
    run = jnp.zeros((E, 1), F32)
    run_t = jnp.zeros((1, 1), F32)
    for c in range(nchunk):
        sc = sel[:, c * CHUNK:(c + 1) * CHUNK]
        scb = sc.astype(BF16)
        pos_scr[:, c * CHUNK:(c + 1) * CHUNK] = jnp.dot(scb, strict_upper, preferred_element_type=F32) + run
        run = run + jnp.sum(sc, axis=1, keepdims=True)
        cnt_c = jnp.sum(sc, axis=0, keepdims=True)
        cnt8 = jnp.broadcast_to(cnt_c, (8, CHUNK)).astype(BF16)
        base_c = jnp.dot(cnt8, strict_upper, preferred_element_type=F32)[0:1] + run_t
        run_t = run_t + jnp.sum(cnt_c, axis=1, keepdims=True)
        rank_c = jnp.dot(strict_lower, scb, preferred_element_type=F32)
        zoff = (b * zrows).astype(F32)
        tmp_scr[:, c * CHUNK:(c + 1) * CHUNK] = base_c + rank_c + zoff
        off_ref[:, c * CHUNK:(c + 1) * CHUNK] = (base_c + zoff).astype(I32)
        cnt_ref[:, c * CHUNK:(c + 1) * CHUNK] = cnt_c.astype(I32)

    lo_id = lax.broadcasted_iota(I32, (SLOT_LO, S), 0).astype(F32)
    hi_id = lax.broadcasted_iota(I32, (n_hi, S), 0).astype(F32)
    t_row = lax.broadcasted_iota(I32, (1, S), 1)
    t_hi = (t_row // TOK_LO).astype(F32)
    t_lo = (t_row % TOK_LO).astype(F32)

    def split3(x):
        hi = x.astype(BF16).astype(F32)
        mid = (x - hi).astype(BF16).astype(F32)
        lo = (x - hi - mid).astype(BF16).astype(F32)
        return hi, mid, lo

    for e in range(E):
        sel_e = sel_scr[e:e + 1, :]
        pos_e = pos_scr[e:e + 1, :]
        p_hi = jnp.floor(pos_e * (1.0 / SLOT_LO))
        p_lo = pos_e - p_hi * SLOT_LO
        oh_lo = jnp.where((lo_id == p_lo) & (sel_e > 0), 1.0, 0.0).astype(BF16)
        oh_hi = jnp.where(hi_id == p_hi, 1.0, 0.0)
        d = tmp_scr[e:e + 1, :]
        d2 = jnp.floor(d * (1.0 / 1024.0))
        d1 = jnp.floor((d - d2 * 1024.0) * (1.0 / 32.0))
        d0 = d - d2 * 1024.0 - d1 * 32.0
        g_hi, g_mid, g_lo = split3(a_scr[e:e + 1, :])
        vals = (t_hi, t_lo, d2, d1, d0, g_hi, g_mid, g_lo)
        stack = jnp.concatenate([oh_hi * v for v in vals], axis=0).astype(BF16)
        r = lax.dot_general(stack, oh_lo, (((1,), (1,)), ((), ())), preferred_element_type=F32)
        rr = [r[k * n_hi:(k + 1) * n_hi] for k in range(8)]
        tok_ref[e] = (rr[0] * TOK_LO + rr[1]).astype(I32) + b * S
        dst_ref[e] = (rr[2] * 1024.0 + rr[3] * 32.0 + rr[4]).astype(I32)
        gate_ref[e] = rr[5] + rr[6] + rr[7]


def _route(aff, B, S):
    cap = CAPACITY_FACTOR * S // N_EXPERTS
    n_hi = cap // SLOT_LO
    E = N_EXPERTS
    kern = functools.partial(_route_kernel, S=S, cap=cap)
    lists = jax.ShapeDtypeStruct((B, E, n_hi, SLOT_LO), I32)
    return pl.pallas_call(
        kern,
        grid=(B,),
        in_specs=[pl.BlockSpec((S, LANES), lambda b: (b, 0))],
        out_specs=[
            pl.BlockSpec((None, E, n_hi, SLOT_LO), lambda b: (b, 0, 0, 0)),
            pl.BlockSpec((None, E, n_hi, SLOT_LO), lambda b: (b, 0, 0, 0)),
            pl.BlockSpec((None, E, n_hi, SLOT_LO), lambda b: (b, 0, 0, 0)),
            pl.BlockSpec((None, 1, S), lambda b: (b, 0, 0)),
            pl.BlockSpec((None, 1, S), lambda b: (b, 0, 0)),
        ],
        out_shape=[lists, lists, jax.ShapeDtypeStruct((B, E, n_hi, SLOT_LO), F32),
                   jax.ShapeDtypeStruct((B, 1, S), I32), jax.ShapeDtypeStruct((B, 1, S), I32)],
        scratch_shapes=[pltpu.VMEM((E, S), F32)] * 4,
        compiler_params=_cparams(("parallel",)),
        name="route",
    )(aff)


FFN_ROWS = 128
FFN_STEPS = 4


def _ffn_kernel(tok_ref, dst_ref, h_hbm, wg_ref, wu_ref, wd_ref, gate_ref, z_hbm, xb, act, yacc, sem_g, sem_s,
                *, tm):
    s = pl.program_id(2)
    tn = yacc.shape[1] // FFN_STEPS

    def gather_row(r):
        return pltpu.make_async_copy(h_hbm.at[pl.ds(tok_ref[0, r], 1), :], yacc.at[pl.ds(r, 1), :], sem_g)

    def scatter_row(r):
        return pltpu.make_async_copy(yacc.at[pl.ds(r, 1), :], z_hbm.at[pl.ds(dst_ref[0, r], 1), :], sem_s)

    def for_rows(fn):
        def body(r, c):
            fn(r)
            return c
        lax.fori_loop(0, tm, body, 0, unroll=8)

    @pl.when(s == 0)
    def _():
        for_rows(lambda r: gather_row(r).start())
        for_rows(lambda r: gather_row(r).wait())

        def cast(j, c):
            rows = pl.ds(pl.multiple_of(j * FFN_ROWS, FFN_ROWS), FFN_ROWS)
            xb[rows, :] = yacc[rows, :].astype(BF16)
            return c
        lax.fori_loop(0, tm // FFN_ROWS, cast, 0)

    @pl.when(s < FFN_STEPS)
    def _():
        x = xb[...]
        a = jnp.dot(x, wg_ref[...].astype(BF16), preferred_element_type=F32)
        u = jnp.dot(x, wu_ref[...].astype(BF16), preferred_element_type=F32)
        act[s] = (a * _sigmoid(a) * u).astype(BF16)

    @pl.when(s >= FFN_STEPS)
    def _():
        a_all = jnp.concatenate([act[f] for f in range(FFN_STEPS)], axis=1)
        y = jnp.dot(a_all, wd_ref[...].astype(BF16), preferred_element_type=F32)
        col = pl.multiple_of((s - FFN_STEPS) * tn, tn)
        yacc[:, pl.ds(col, tn)] = y * gate_ref[...]

    @pl.when(s == 2 * FFN_STEPS - 1)
    def _():
        for_rows(lambda r: scatter_row(r).start())
        for_rows(lambda r: scatter_row(r).wait())


def _expert_ffn(h2, tok, dst, gate, w_gate, w_up, w_down, layer, tm=1024):
    E, M = tok.shape
    T, D = h2.shape
    F = w_gate.shape[-1]
    tm = min(tm, M)
    nm = M // tm
    tf = F // FFN_STEPS
    tn = D // FFN_STEPS
    last = FFN_STEPS - 1
    tok3 = tok.reshape(E * nm, 1, tm)
    dst3 = dst.reshape(E * nm, 1, tm)
    smem_spec = pl.BlockSpec((None, 1, tm), lambda e, m, s: (e * nm + m, 0, 0), memory_space=pltpu.SMEM)
    kern = functools.partial(_ffn_kernel, tm=tm)
    return pl.pallas_call(
        kern,
        grid=(E, nm, 2 * FFN_STEPS),
        in_specs=[
            smem_spec,
            smem_spec,
            pl.BlockSpec(memory_space=pl.ANY),
            pl.BlockSpec((None, None, D, tf), lambda e, m, s: (layer, e, 0, jnp.minimum(s, last))),
            pl.BlockSpec((None, None, D, tf), lambda e, m, s: (layer, e, 0, jnp.minimum(s, last))),
            pl.BlockSpec((None, None, F, tn), lambda e, m, s: (layer, e, 0, jnp.maximum(s - FFN_STEPS, 0))),
            pl.BlockSpec((None, tm, 1), lambda e, m, s: (e, m, 0)),
        ],
        out_specs=pl.BlockSpec(memory_space=pl.ANY),
        out_shape=jax.ShapeDtypeStruct((E * M, D), F32),
        scratch_shapes=[
            pltpu.VMEM((tm, D), BF16),
            pltpu.VMEM((FFN_STEPS, tm, tf), BF16),
            pltpu.VMEM((tm, D), F32),
            pltpu.SemaphoreType.DMA(()),
            pltpu.SemaphoreType.DMA(()),
        ],
        compiler_params=_cparams(("arbitrary", "arbitrary", "arbitrary")),
        name="expert_ffn",
    )(tok3, dst3, h2, w_gate, w_up, w_down, gate)


COMB_TOK = 512
COMB_ROWS = 256


def _combine_kernel(start_ref, nch_ref, x_ref, off_ref, cnt_ref, g_ref, z_hbm, o_ref, zbuf, sem, st,
                    *, zrows, final):
    i = pl.program_id(0)
    nt = pl.num_programs(0)
    start = start_ref[i]
    n = nch_ref[i]
    i_next = jnp.minimum(i + 1, nt - 1)
    next_ready = (i + 1 < nt) & (nch_ref[i_next] > 0)

    @pl.when(i == 0)
    def _():
        st[0] = 0
        st[1] = 0

    gc = st[0]
    o_ref[...] = x_ref[...]
    off = off_ref[...]
    end = off + cnt_ref[...]

    def chunk_start(tile, k):
        return pl.multiple_of(jnp.minimum(start_ref[tile] + k * COMB_ROWS, zrows - COMB_ROWS), 8)

    def copy(tile, k, slot):
        return pltpu.make_async_copy(z_hbm.at[pl.ds(chunk_start(tile, k), COMB_ROWS), :], zbuf.at[slot],
                                     sem.at[slot])

    @pl.when((n > 0) & (st[1] == 0))
    def _():
        copy(i, 0, gc & 1).start()

    def body(k, c):
        slot = (gc + k) & 1
        copy(i, k, slot).wait()

        @pl.when(k + 1 < n)
        def _():
            copy(i, k + 1, 1 - slot).start()

        @pl.when((k + 1 == n) & next_ready)
        def _():
            copy(i_next, 0, 1 - slot).start()

        rows = chunk_start(i, k) + lax.broadcasted_iota(I32, (1, COMB_ROWS), 1)
        lo = jnp.maximum(off, start + k * COMB_ROWS)
        seg = jnp.where((rows >= lo) & (rows < end), 1.0, 0.0).astype(BF16)
        o_ref[...] += jnp.dot(seg, zbuf[slot].astype(BF16), preferred_element_type=F32)
        return c

    lax.fori_loop(0, n, body, 0)
    st[0] = gc + n
    st[1] = ((n > 0) & next_ready).astype(I32)

    if final:
        x = o_ref[...]
        ms = jnp.mean(x * x, axis=-1, keepdims=True)
        o_ref[...] = x * lax.rsqrt(ms + EPS) * g_ref[...]


def _combine(x2d, z, off, cnt, gain, final):
    T, D = x2d.shape
    zrows = z.shape[0]
    tt = min(COMB_TOK, T)
    nt = T // tt
    start = (off[::tt] // 8) * 8
    last = (off + cnt)[tt - 1::tt]
    nch = (last - start + COMB_ROWS - 1) // COMB_ROWS
    kern = functools.partial(_combine_kernel, zrows=zrows, final=final)
    grid_spec = pltpu.PrefetchScalarGridSpec(
        num_scalar_prefetch=2,
        grid=(nt,),
        in_specs=[
            pl.BlockSpec((tt, D), lambda i, s, n: (i, 0)),
            pl.BlockSpec((tt, 1), lambda i, s, n: (i, 0)),
            pl.BlockSpec((tt, 1), lambda i, s, n: (i, 0)),
            pl.BlockSpec((1, D), lambda i, s, n: (0, 0)),
            pl.BlockSpec(memory_space=pl.ANY),
        ],
        out_specs=pl.BlockSpec((tt, D), lambda i, s, n: (i, 0)),
        scratch_shapes=[pltpu.VMEM((2, COMB_ROWS, D), F32), pltpu.SemaphoreType.DMA((2,)),
                        pltpu.SMEM((2,), I32)],
    )
    return pl.pallas_call(
        kern,
        grid_spec=grid_spec,
        out_shape=jax.ShapeDtypeStruct((T, D), F32),
        compiler_params=_cparams(("arbitrary",)),
        name="combine",
    )(start.astype(I32), nch.astype(I32), x2d, off[:, None], cnt[:, None], gain, z)


def moe(x2d, h2, aff, w_gate, w_up, w_down, layer, B, S, gain, final):
    E = N_EXPERTS
    cap = CAPACITY_FACTOR * S // E
    tok, dst, gate, off, cnt = _route(aff, B, S)
    to_em = lambda a: jnp.swapaxes(a.reshape(B, E, cap), 0, 1).reshape(E, B * cap)
    z = _expert_ffn(h2, to_em(tok), to_em(dst), to_em(gate)[:, :, None], w_gate, w_up, w_down, layer)
    return _combine(x2d, z, off.reshape(B * S), cnt.reshape(B * S), gain, final)


def kernel(x, norm_mix_g, w_in, b_gates, conv_w, conv_b, na_rpb, na_norm_g, ml_norm_g, w_out,
           norm_ffn_g, w_router, w_gate, w_up, w_down, final_norm_g):
    B, S, D = x.shape
    T = B * S
    depth = w_in.shape[0]
    x2d = x.reshape(T, D)

    for l in range(depth):
        w_main = w_in[l, :, :N_MAIN].astype(BF16)
        w_gates = jnp.pad(w_in[l, :, N_MAIN:], ((0, 0), (0, LANES - N_GATES))).astype(BF16)
        proj, gates = _in_proj(x2d, norm_mix_g[l][None, :], w_main, w_gates)

        tb = _na_bias_table(na_rpb[l])
        na_o = _na_attention(proj, tb, B, S)

        g4 = gates[:, :N_GATES].reshape(B, S, 4, ML_HEADS)
        g_rows = jnp.pad(jnp.transpose(g4, (0, 3, 2, 1)), ((0, 0), (0, 0), (0, 4), (0, 0)))
        bg_rows = jnp.pad(b_gates[l].reshape(4, ML_HEADS).T, ((0, 0), (0, 4)))[:, :, None]
        ml_o = _mlstm(proj, g_rows, bg_rows.astype(F32), conv_w[l], conv_b[l][None, :],
                      ml_norm_g[l][None, :], B, S)

        w_r = jnp.pad(w_router[l], ((0, 0), (0, LANES - N_EXPERTS))).astype(BF16)
        x2d, h2, aff = _out_proj(na_o, ml_o, x2d, w_out[l].astype(BF16), na_norm_g[l][None, :],
                                 norm_ffn_g[l][None, :], w_r)

        x2d = moe(x2d, h2, aff, w_gate, w_up, w_down, l, B, S, final_norm_g[None, :], l == depth - 1)

    return x2d.reshape(B, S, D)
```

```python
import functools

import numpy as np
import jax
import jax.numpy as jnp
from jax import lax
from jax.experimental import pallas as pl
from jax.experimental.pallas import tpu as pltpu

F32 = jnp.float32
BF16 = jnp.bfloat16

EPS = 1e-6
GRID_W = 64
NA_HEADS = 8
NA_HEAD_DIM = 128
NA_KR = 8
NA_KC = 16
NA_WIDTH = NA_HEADS * NA_HEAD_DIM
NA_LOOKAHEAD = 6
ML_HEADS = 4
ML_QK_DIM = 128
ML_V_DIM = 256
ML_QK_WIDTH = ML_HEADS * ML_QK_DIM
ML_V_WIDTH = ML_HEADS * ML_V_DIM
CONV_W = 5
N_GATES = 4 * ML_HEADS
N_EXPERTS = 16
CAPACITY_FACTOR = 2
N_MAIN = 3 * NA_WIDTH + 2 * ML_QK_WIDTH + 2 * ML_V_WIDTH
LANES = 128
NEG = -1e30
ML_CHUNK = 256
VMEM_LIMIT = 56 * 1024 * 1024


def _cparams(sem):
    return pltpu.CompilerParams(dimension_semantics=sem, vmem_limit_bytes=VMEM_LIMIT)


def _in_proj_kernel(x_ref, g_ref, w_ref, wg_ref, o_ref, og_ref, h_scr):
    @pl.when(pl.program_id(1) == 0)
    def _():
        x = x_ref[...]
        ms = jnp.mean(x * x, axis=-1, keepdims=True)
        h = (x * lax.rsqrt(ms + EPS) * g_ref[...]).astype(BF16)
        h_scr[...] = h
        og_ref[...] = jnp.dot(h, wg_ref[...], preferred_element_type=F32)

    o_ref[...] = jnp.dot(h_scr[...], w_ref[...], preferred_element_type=F32).astype(o_ref.dtype)


def _in_proj(x2d, g, w_main, w_gates, tm=1024, tn=2048):
    T, D = x2d.shape
    N = w_main.shape[1]
    return pl.pallas_call(
        _in_proj_kernel,
        grid=(T // tm, N // tn),
        in_specs=[
            pl.BlockSpec((tm, D), lambda i, j: (i, 0)),
            pl.BlockSpec((1, D), lambda i, j: (0, 0)),
            pl.BlockSpec((D, tn), lambda i, j: (0, j)),
            pl.BlockSpec((D, LANES), lambda i, j: (0, 0)),
        ],
        out_specs=[
            pl.BlockSpec((tm, tn), lambda i, j: (i, j)),
            pl.BlockSpec((tm, LANES), lambda i, j: (i, 0)),
        ],
        out_shape=[
            jax.ShapeDtypeStruct((T, N), BF16),
            jax.ShapeDtypeStruct((T, LANES), F32),
        ],
        scratch_shapes=[pltpu.VMEM((tm, D), BF16)],
        compiler_params=_cparams(("parallel", "arbitrary")),
        name="in_proj",
    )(x2d, g, w_main, w_gates)


def _na_bias_table(rpb):
    n_dc = 2 * NA_KC - 1
    off = np.arange(NA_KR)[:, None]
    a = np.arange(NA_KR)[None, :]
    dr = a - off + (NA_KR - 1)
    c = np.arange(GRID_W)
    col_start = np.clip(c - NA_KC // 2, 0, GRID_W - NA_KC)
    col_in = (c[None, :] >= col_start[:, None]) & (c[None, :] < col_start[:, None] + NA_KC)
    dc = np.clip(c[None, :] - c[:, None] + (NA_KC - 1), 0, n_dc - 1)
    sel_c = jnp.asarray(dc[:, :, None] == np.arange(n_dc), F32)
    x = jnp.einsum("hij,qkj->hiqk", rpb.astype(F32), sel_c, precision=lax.Precision.HIGHEST)
    tb = jnp.transpose(x[:, dr], (0, 1, 3, 2, 4))
    tb = jnp.where(col_in[None, None, :, None, :], tb, NEG)
    return tb.reshape(rpb.shape[0], NA_KR, GRID_W, NA_KR * GRID_W)


def _na_kernel(q_ref, k_ref, v_ref, tb_ref, o_ref, *, rb, rows):
    i = pl.program_id(2)
    scale = NA_HEAD_DIM ** -0.5
    win = NA_KR * GRID_W

    def scores(rr):
        r = i * rb + rr
        rs = jnp.clip(r - NA_KR // 2, 0, rows - NA_KR)
        start = pl.multiple_of(rs * GRID_W, GRID_W)
        q = q_ref[rr * GRID_W:(rr + 1) * GRID_W, :]
        kw = k_ref[pl.ds(start, win), :]
        s = lax.dot_general(q, kw, (((1,), (1,)), ((), ())), preferred_element_type=F32)
        return s, r - rs, start

    queue = [scores(rr) for rr in range(min(NA_LOOKAHEAD, rb))]
    for rr in range(rb):
        s, off, start = queue.pop(0)
        if rr + NA_LOOKAHEAD < rb:
            queue.append(scores(rr + NA_LOOKAHEAD))
        vw = v_ref[pl.ds(start, win), :]
        s = s * scale + tb_ref[off]
        m = jnp.max(s, axis=-1, keepdims=True)
        p = jnp.exp(s - m)
        l = jnp.sum(p, axis=-1, keepdims=True)
        o = jnp.dot(p.astype(BF16), vw, preferred_element_type=F32)
        o_ref[rr * GRID_W:(rr + 1) * GRID_W, :] = (o / l).astype(o_ref.dtype)


def _na_attention(proj, tb, B, S, rb=16):
    T = B * S
    rows = S // GRID_W
    assert rows >= NA_KR and rows % rb == 0
    nblk = rows // rb
    tq = rb * GRID_W
    kern = functools.partial(_na_kernel, rb=rb, rows=rows)
    return pl.pallas_call(
        kern,
        grid=(B, NA_HEADS, nblk),
        in_specs=[
            pl.BlockSpec((tq, NA_HEAD_DIM), lambda b, h, i: (b * nblk + i, h)),
            pl.BlockSpec((S, NA_HEAD_DIM), lambda b, h, i: (b, NA_HEADS + h)),
            pl.BlockSpec((S, NA_HEAD_DIM), lambda b, h, i: (b, 2 * NA_HEADS + h)),
            pl.BlockSpec((None, NA_KR, GRID_W, NA_KR * GRID_W), lambda b, h, i: (h, 0, 0, 0)),
        ],
        out_specs=pl.BlockSpec((tq, NA_HEAD_DIM), lambda b, h, i: (b * nblk + i, h)),
        out_shape=jax.ShapeDtypeStruct((T, NA_WIDTH), BF16),
        compiler_params=_cparams(("parallel", "parallel", "arbitrary")),
        name="na_attention",
    )(proj, proj, proj, tb)


def _log_sigmoid(x):
    return jnp.minimum(x, 0.0) - jnp.log(1.0 + jnp.exp(-jnp.abs(x)))


def _sigmoid(x):
    return 1.0 / (1.0 + jnp.exp(-x))


def _mlstm_kernel(q_ref, k_ref, v_ref, og_ref, g_ref, bg_ref, cwq_ref, cwk_ref, cbq_ref, cbk_ref,
                  ng_ref, out_ref, qs, kst, rrow, bcol, hf, hb, cst, nst, *, S, L):
    nc = S // L
    halo = 8

    def conv_silu(src_ref, cw, cbv, c, scale):
        base = pl.multiple_of(c * L, L)
        lo = jnp.maximum(base - halo, 0)
        hi = jnp.minimum(base + L, S - halo)
        prev = src_ref[pl.ds(pl.multiple_of(lo, halo), halo), :].astype(F32)
        nxt = src_ref[pl.ds(pl.multiple_of(hi, halo), halo), :].astype(F32)
        prev = jnp.where(c > 0, prev, 0.0)
        nxt = jnp.where(c < nc - 1, nxt, 0.0)
        main = src_ref[pl.ds(base, L), :].astype(F32)
        blk = jnp.concatenate([prev, main, nxt], axis=0)
        n = L + 2 * halo
        acc = jnp.zeros((L, LANES), F32) + cbv
        for w in range(CONV_W):
            sh = (CONV_W // 2 - w) % n
            xs = blk if sh == 0 else pltpu.roll(blk, sh, 0)
            acc = acc + xs[halo:halo + L, :] * cw[w:w + 1, :]
        return acc * _sigmoid(acc) * scale

    cwq, cbq, cwk, cbk = cwq_ref[...], cbq_ref[...], cwk_ref[...], cbk_ref[...]

    def conv_body(c, carry):
        base = pl.multiple_of(c * L, L)
        qs[pl.ds(base, L), :] = conv_silu(q_ref, cwq, cbq, c, ML_QK_DIM ** -0.5).astype(BF16)
        kst[:, pl.ds(base, L)] = conv_silu(k_ref, cwk, cbk, c, 1.0).T.astype(BF16)
        return carry

    lax.fori_loop(0, nc, conv_body, 0)

    t_idx = lax.broadcasted_iota(jnp.int32, (L, L), 0)
    j_idx = lax.broadcasted_iota(jnp.int32, (L, L), 1)
    upper = (t_idx <= j_idx).astype(BF16)
    lower = (t_idx >= j_idx).astype(BF16)

    def split3(x):
        hi = x.astype(BF16).astype(F32)
        mid = (x - hi).astype(BF16).astype(F32)
        lo = (x - hi - mid).astype(BF16).astype(F32)
        return hi, mid, lo

    def gate_body(c, carry):
        base = pl.multiple_of(c * L, L)
        g = g_ref[:, pl.ds(base, L)] + bg_ref[...]
        hi, mid, lo = split3(_log_sigmoid(g))
        lf3 = jnp.concatenate([hi, mid, lo, jnp.zeros_like(hi)], axis=0).astype(BF16)
        pre = jnp.dot(lf3, upper, preferred_element_type=F32)
        suf = jnp.dot(lf3, lower, preferred_element_type=F32)
        pre = pre[0:8] + pre[8:16] + pre[16:24]
        suf = suf[0:8] + suf[8:16] + suf[16:24]
        cum_f, cum_b = pre[1:2], suf[3:4]
        gx_f = cum_f[:, L - 1:L] - cum_f + g[0:1]
        gx_b = cum_b[:, 0:1] - cum_b + g[2:3]
        r = jnp.concatenate([g[0:1], cum_f, g[2:3], cum_b, gx_f, gx_b, g[0:1] - cum_f, g[2:3] - cum_b],
                            axis=0)
        rrow[:, pl.ds(base, L)] = r
        return carry

    lax.fori_loop(0, nc, gate_body, 0, unroll=4)

    r_all = rrow[...]
    row_all = lax.broadcasted_iota(jnp.int32, (8, S), 0)
    lane_all = lax.broadcasted_iota(jnp.int32, (8, S), 1) & (L - 1)
    pm = r_all
    sm = r_all
    sh = 1
    while sh < L:
        pm = jnp.maximum(pm, jnp.where(lane_all >= sh, pltpu.roll(pm, sh, 1), NEG))
        sm = jnp.maximum(sm, jnp.where(lane_all < L - sh, pltpu.roll(sm, S - sh, 1), NEG))
        sh *= 2
    mi = jnp.where(row_all == 6, pm + pltpu.roll(r_all, 5, 0), sm + pltpu.roll(r_all, 4, 0))
    rrow[...] = jnp.where(row_all >= 6, mi, r_all)

    pick_r = lax.broadcasted_iota(jnp.int32, (LANES, 2 * LANES), 0)
    pick_l = lax.broadcasted_iota(jnp.int32, (LANES, 2 * LANES), 1)

    def picker(k_lo, k_hi):
        want = jnp.where(pick_l < LANES, k_lo, k_hi)
        return ((pick_r < 24) & ((pick_r & 7) == want)).astype(BF16)

    pairs = ((1, 6, 0, 1), (4, 3, 2, 3), (7, 5, 4, 5))

    def bcol_body(c, carry):
        base = pl.multiple_of(c * L, L)
        hi, mid, lo = split3(rrow[:, pl.ds(base, L)])
        x = jnp.concatenate([hi, mid, lo, jnp.zeros((LANES - 24, L), F32)], axis=0)
        xt = x.T.astype(BF16)
        for k_lo, k_hi, d_lo, d_hi in pairs:
            y = jnp.dot(xt, picker(k_lo, k_hi), preferred_element_type=F32)
            bcol[d_lo, pl.ds(base, L), :] = y[:, :LANES]
            bcol[d_hi, pl.ds(base, L), :] = y[:, LANES:]
        return carry

    lax.fori_loop(0, nc, bcol_body, 0, unroll=4)

    fwd_mask = j_idx <= t_idx
    bwd_mask = j_idx >= t_idx
    ones_b = jnp.ones((L, LANES), BF16)

    def wide(x, n):
        return jnp.concatenate([x] * n, axis=1)

    def load(c, d, total_row):
        base = pl.multiple_of(c * L, L)
        qc = qs[pl.ds(base, L), :]
        C = cst[d]
        cum = bcol[3 * d, pl.ds(base, L), :]
        return dict(
            base=base, qc=qc, vc=v_ref[pl.ds(base, L), :], kt=kst[:, pl.ds(base, L)],
            cum=cum, mi=bcol[3 * d + 1, pl.ds(base, L), :], gx=bcol[3 * d + 2, pl.ds(base, L), :],
            i_row=rrow[2 * d:2 * d + 1, pl.ds(base, L)],
            cum_row=rrow[2 * d + 1:2 * d + 2, pl.ds(base, L)],
            total=cum[total_row:total_row + 1, :], C=C,
            s=jnp.dot(qc, kst[:, pl.ds(base, L)], preferred_element_type=F32),
            qC=jnp.dot(qc, C.astype(BF16), preferred_element_type=F32),
            qn=jnp.dot(qc, nst[d].astype(BF16), preferred_element_type=F32))

    def weights(x, mask, m_st):
        m_inter = x["cum"] + m_st
        m_j = jnp.maximum(m_inter, x["mi"])
        dmat = jnp.where(mask, wide(x["cum"], L // LANES) - x["cum_row"] + x["i_row"], NEG)
        qkw = (x["s"] * jnp.exp(dmat - wide(m_j, L // LANES))).astype(BF16)
        inter = jnp.exp(m_inter - m_j)
        m_new = jnp.maximum(x["total"] + m_st, jnp.max(x["gx"], axis=0, keepdims=True))
        wk = jnp.exp(x["gx"] - m_new)
        decay = jnp.exp(x["total"] + m_st - m_new)
        vw = (wide(wk, ML_V_DIM // LANES) * x["vc"].astype(F32)).astype(BF16)
        return qkw, inter, m_j, wk.astype(BF16), vw, decay, m_new

    def finish(x, d, qkw, inter, m_j, wk, vw, decay, h_ref):
        base = x["base"]
        den = inter * x["qn"] + jnp.dot(qkw, ones_b, preferred_element_type=F32)
        rden = 1.0 / jnp.maximum(jnp.abs(den), jnp.exp(-m_j))
        num = wide(inter, ML_V_DIM // LANES) * x["qC"] + jnp.dot(qkw, x["vc"], preferred_element_type=F32)
        h_ref[pl.ds(base, L), :] = num * wide(rden, ML_V_DIM // LANES)
        cst[d] = wide(decay, ML_V_DIM // LANES) * x["C"] + jnp.dot(x["kt"], vw, preferred_element_type=F32)
        nst[d] = decay * nst[d] + jnp.dot(x["kt"], wk, preferred_element_type=F32)

    cst[...] = jnp.zeros(cst.shape, F32)
    nst[...] = jnp.zeros(nst.shape, F32)

    def scan_body(t, carry):
        m_f, m_b = carry
        xf = load(t, 0, L - 1)
        xb = load(nc - 1 - t, 1, 0)
        wf = weights(xf, fwd_mask, m_f)
        finish(xf, 0, *wf[:-1], hf)
        wb = weights(xb, bwd_mask, m_b)
        finish(xb, 1, *wb[:-1], hb)
        return wf[-1], wb[-1]

    z_m = jnp.zeros((1, LANES), F32)
    lax.fori_loop(0, nc, scan_body, (z_m, z_m))

    ng = ng_ref[...]

    def fin_body(c, carry):
        base = pl.multiple_of(c * L, L)
        h = hf[pl.ds(base, L), :] + hb[pl.ds(base, L), :]
        ms = jnp.mean(h * h, axis=-1, keepdims=True)
        y = h * lax.rsqrt(ms + EPS) * ng
        y = y * _sigmoid(og_ref[pl.ds(base, L), :].astype(F32))
        out_ref[pl.ds(base, L), :] = y.astype(out_ref.dtype)
        return carry

    lax.fori_loop(0, nc, fin_body, 0)


def _mlstm(proj, gates_rows, bg_rows, conv_w, conv_b, norm_g, B, S):
    T = B * S
    L = min(ML_CHUNK, S)
    assert S % L == 0 and L % LANES == 0
    q0 = 3 * NA_WIDTH // ML_QK_DIM
    k0 = q0 + ML_HEADS
    v0 = (3 * NA_WIDTH + 2 * ML_QK_WIDTH) // ML_V_DIM
    o0 = v0 + ML_HEADS
    kern = functools.partial(_mlstm_kernel, S=S, L=L)
    return pl.pallas_call(
        kern,
        grid=(B, ML_HEADS),
        in_specs=[
            pl.BlockSpec((S, ML_QK_DIM), lambda b, h: (b, q0 + h)),
            pl.BlockSpec((S, ML_QK_DIM), lambda b, h: (b, k0 + h)),
            pl.BlockSpec((S, ML_V_DIM), lambda b, h: (b, v0 + h)),
            pl.BlockSpec((S, ML_V_DIM), lambda b, h: (b, o0 + h)),
            pl.BlockSpec((None, None, 8, S), lambda b, h: (b, h, 0, 0)),
            pl.BlockSpec((None, 8, 1), lambda b, h: (h, 0, 0)),
            pl.BlockSpec((CONV_W, ML_QK_DIM), lambda b, h: (0, h)),
            pl.BlockSpec((CONV_W, ML_QK_DIM), lambda b, h: (0, ML_HEADS + h)),
            pl.BlockSpec((1, ML_QK_DIM), lambda b, h: (0, h)),
            pl.BlockSpec((1, ML_QK_DIM), lambda b, h: (0, ML_HEADS + h)),
            pl.BlockSpec((1, ML_V_DIM), lambda b, h: (0, h)),
        ],
        out_specs=pl.BlockSpec((S, ML_V_DIM), lambda b, h: (b, h)),
        out_shape=jax.ShapeDtypeStruct((T, ML_V_WIDTH), BF16),
        scratch_shapes=[
            pltpu.VMEM((S, ML_QK_DIM), BF16),
            pltpu.VMEM((ML_QK_DIM, S), BF16),
            pltpu.VMEM((8, S), F32),
            pltpu.VMEM((6, S, LANES), F32),
            pltpu.VMEM((S, ML_V_DIM), F32),
            pltpu.VMEM((S, ML_V_DIM), F32),
            pltpu.VMEM((2, ML_QK_DIM, ML_V_DIM), F32),
            pltpu.VMEM((2, ML_QK_DIM, LANES), F32),
        ],
        compiler_params=_cparams(("parallel", "arbitrary")),
        name="mlstm",
    )(proj, proj, proj, proj, gates_rows, bg_rows, conv_w, conv_w, conv_b, conv_b, norm_g)


def _out_proj_kernel(na_ref, ml_ref, x_ref, w_ref, nag_ref, fg_ref, wr_ref, xo_ref, h_ref, aff_ref):
    na = na_ref[...].astype(F32)
    ms = jnp.mean(na * na, axis=-1, keepdims=True)
    na_n = (na * lax.rsqrt(ms + EPS) * nag_ref[...]).astype(BF16)
    upd = jnp.dot(na_n, w_ref[0:NA_WIDTH, :], preferred_element_type=F32)
    upd = upd + jnp.dot(ml_ref[...], w_ref[NA_WIDTH:, :], preferred_element_type=F32)
    xn = x_ref[...] + upd
    xo_ref[...] = xn
    ms2 = jnp.mean(xn * xn, axis=-1, keepdims=True)
    h = xn * lax.rsqrt(ms2 + EPS) * fg_ref[...]
    h_ref[...] = h
    logits = jnp.dot(h.astype(BF16), wr_ref[...], preferred_element_type=F32)
    lane = lax.broadcasted_iota(jnp.int32, logits.shape, 1)
    logits = jnp.where(lane < N_EXPERTS, logits, NEG)
    m = jnp.max(logits, axis=-1, keepdims=True)
    e = jnp.exp(logits - m)
    aff_ref[...] = e / jnp.sum(e, axis=-1, keepdims=True)


def _out_proj(na_o, ml_o, x2d, w_out, na_g, ffn_g, w_router, tm=512):
    T, D = x2d.shape
    return pl.pallas_call(
        _out_proj_kernel,
        grid=(T // tm,),
        in_specs=[
            pl.BlockSpec((tm, NA_WIDTH), lambda i: (i, 0)),
            pl.BlockSpec((tm, ML_V_WIDTH), lambda i: (i, 0)),
            pl.BlockSpec((tm, D), lambda i: (i, 0)),
            pl.BlockSpec((NA_WIDTH + ML_V_WIDTH, D), lambda i: (0, 0)),
            pl.BlockSpec((1, NA_WIDTH), lambda i: (0, 0)),
            pl.BlockSpec((1, D), lambda i: (0, 0)),
            pl.BlockSpec((D, LANES), lambda i: (0, 0)),
        ],
        out_specs=[
            pl.BlockSpec((tm, D), lambda i: (i, 0)),
            pl.BlockSpec((tm, D), lambda i: (i, 0)),
            pl.BlockSpec((tm, LANES), lambda i: (i, 0)),
        ],
        out_shape=[
            jax.ShapeDtypeStruct((T, D), F32),
            jax.ShapeDtypeStruct((T, D), F32),
            jax.ShapeDtypeStruct((T, LANES), F32),
        ],
        compiler_params=_cparams(("parallel",)),
        name="out_proj",
    )(na_o, ml_o, x2d, w_out, na_g, ffn_g, w_router)


I32 = jnp.int32
SLOT_LO = 32
TOK_LO = 64
CHUNK = 256


def _route_kernel(aff_ref, tok_ref, dst_ref, gate_ref, off_ref, cnt_ref, a_scr, sel_scr, pos_scr, tmp_scr,
                  *, S, cap):
    b = pl.program_id(0)
    E = N_EXPERTS
    nchunk = S // CHUNK
    zrows = E * cap
    n_hi = cap // SLOT_LO

    TB = 512
    for c in range(S // TB):
        a_scr[:, c * TB:(c + 1) * TB] = aff_ref[c * TB:(c + 1) * TB, :].T[0:E, :]
    A = a_scr[...]

    def thr_body(i, thr):
        cand = thr | jnp.left_shift(jnp.int32(1), 30 - i)
        cand_f = lax.bitcast_convert_type(cand, F32)
        cnt = jnp.sum((a_scr[...] >= cand_f).astype(F32), axis=1, keepdims=True)
        return jnp.where(cnt >= cap, cand, thr)

    thr = lax.fori_loop(0, 31, thr_body, jnp.zeros((E, 1), I32))
    thr_f = lax.bitcast_convert_type(thr, F32)
    gt = A > thr_f
    eq = A == thr_f
    need = cap - jnp.sum(gt.astype(F32), axis=1, keepdims=True)
    idx = lax.broadcasted_iota(I32, (E, S), 1)

    nbits = S.bit_length() - 1

    def tie_body(i, jt):
        cand = jt | jnp.left_shift(jnp.int32(1), nbits - 1 - i)
        c = jnp.sum((eq & (idx < cand)).astype(F32), axis=1, keepdims=True)
        return jnp.where(c < need, cand, jt)

    jt = lax.fori_loop(0, nbits, tie_body, jnp.zeros((E, 1), I32))
    sel = (gt | (eq & (idx <= jt))).astype(F32)
    sel_scr[...] = sel

    r_i = lax.broadcasted_iota(I32, (CHUNK, CHUNK), 0)
    c_i = lax.broadcasted_iota(I32, (CHUNK, CHUNK), 1)
    strict_upper = (r_i < c_i).astype(BF16)
    e_r = lax.broadcasted_iota(I32, (E, E), 0)
    e_c = lax.broadcasted_iota(I32, (E, E), 1)
    strict_lower = (e_c < e_r).astype(BF16)
    run = jnp.zeros((E, 1), F32)
    run_t = jnp.zeros((1, 1), F32)
    for c in range(nchunk):
        sc = sel[:, c * CHUNK:(c + 1) * CHUNK]
        scb = sc.astype(BF16)
        pos_scr[:, c * CHUNK:(c + 1) * CHUNK] = jnp.dot(scb, strict_upper, preferred_element_type=F32) + run
        run = run + jnp.sum(sc, axis=1, keepdims=True)
        cnt_c = jnp.sum(sc, axis=0, keepdims=True)
        cnt8 = jnp.broadcast_to(cnt_c, (8, CHUNK)).astype(BF16)
        base_c = jnp.dot(cnt8, strict_upper, preferred_element_type=F32)[0:1] + run_t
        run_t = run_t + jnp.sum(cnt_c, axis=1, keepdims=True)
        rank_c = jnp.dot(strict_lower, scb, preferred_element_type=F32)
        zoff = (b * zrows).astype(F32)
        tmp_scr[:, c * CHUNK:(c + 1) * CHUNK] = base_c + rank_c + zoff
        off_ref[:, c * CHUNK:(c + 1) * CHUNK] = (base_c + zoff).astype(I32)
        cnt_ref[:, c * CHUNK:(c + 1) * CHUNK] = cnt_c.astype(I32)

    lo_id = lax.broadcasted_iota(I32, (SLOT_LO, S), 0).astype(F32)
    hi_id = lax.broadcasted_iota(I32, (n_hi, S), 0).astype(F32)
    t_row = lax.broadcasted_iota(I32, (1, S), 1)
    t_hi = (t_row // TOK_LO).astype(F32)
    t_lo = (t_row % TOK_LO).astype(F32)

    def split3(x):
        hi = x.astype(BF16).astype(F32)
        mid = (x - hi).astype(BF16).astype(F32)
        lo = (x - hi - mid).astype(BF16).astype(F32)
        return hi, mid, lo

    for e in range(E):
        sel_e = sel_scr[e:e + 1, :]
        pos_e = pos_scr[e:e + 1, :]
        p_hi = jnp.floor(pos_e * (1.0 / SLOT_LO))
        p_lo = pos_e - p_hi * SLOT_LO
        oh_lo = jnp.where((lo_id == p_lo) & (sel_e > 0), 1.0, 0.0).astype(BF16)
        oh_hi = jnp.where(hi_id == p_hi, 1.0, 0.0)
        d = tmp_scr[e:e + 1, :]
        d2 = jnp.floor(d * (1.0 / 1024.0))
        d1 = jnp.floor((d - d2 * 1024.0) * (1.0 / 32.0))
        d0 = d - d2 * 1024.0 - d1 * 32.0
        g_hi, g_mid, g_lo = split3(a_scr[e:e + 1, :])
        vals = (t_hi, t_lo, d2, d1, d0, g_hi, g_mid, g_lo)
        stack = jnp.concatenate([oh_hi * v for v in vals], axis=0).astype(BF16)
        r = lax.dot_general(stack, oh_lo, (((1,), (1,)), ((), ())), preferred_element_type=F32)
        rr = [r[k * n_hi:(k + 1) * n_hi] for k in range(8)]
        tok_ref[e] = (rr[0] * TOK_LO + rr[1]).astype(I32) + b * S
        dst_ref[e] = (rr[2] * 1024.0 + rr[3] * 32.0 + rr[4]).astype(I32)
        gate_ref[e] = rr[5] + rr[6] + rr[7]


def _route(aff, B, S):
    cap = CAPACITY_FACTOR * S // N_EXPERTS
    n_hi = cap // SLOT_LO
    E = N_EXPERTS
    kern = functools.partial(_route_kernel, S=S, cap=cap)
    lists = jax.ShapeDtypeStruct((B, E, n_hi, SLOT_LO), I32)
    return pl.pallas_call(
        kern,
        grid=(B,),
        in_specs=[pl.BlockSpec((S, LANES), lambda b: (b, 0))],
        out_specs=[
            pl.BlockSpec((None, E, n_hi, SLOT_LO), lambda b: (b, 0, 0, 0)),
            pl.BlockSpec((None, E, n_hi, SLOT_LO), lambda b: (b, 0, 0, 0)),
            pl.BlockSpec((None, E, n_hi, SLOT_LO), lambda b: (b, 0, 0, 0)),
            pl.BlockSpec((None, 1, S), lambda b: (b, 0, 0)),
            pl.BlockSpec((None, 1, S), lambda b: (b, 0, 0)),
        ],
        out_shape=[lists, lists, jax.ShapeDtypeStruct((B, E, n_hi, SLOT_LO), F32),
                   jax.ShapeDtypeStruct((B, 1, S), I32), jax.ShapeDtypeStruct((B, 1, S), I32)],
        scratch_shapes=[pltpu.VMEM((E, S), F32)] * 4,
        compiler_params=_cparams(("parallel",)),
        name="route",
    )(aff)


FFN_ROWS = 128
FFN_STEPS = 4


def _ffn_kernel(tok_ref, dst_ref, h_hbm, wg_ref, wu_ref, wd_ref, gate_ref, z_hbm, xg, xb, act, yacc, sem_g, sem_s,
                *, tm):
    s = pl.program_id(2)
    q = pl.program_id(0) * pl.num_programs(1) + pl.program_id(1)
    nq = pl.num_programs(0) * pl.num_programs(1)
    tn = yacc.shape[1] // FFN_STEPS

    def gather_row(r):
        return pltpu.make_async_copy(h_hbm.at[pl.ds(tok_ref[0, r], 1), :], xg.at[pl.ds(r, 1), :], sem_g)

    def scatter_row(r):
        return pltpu.make_async_copy(yacc.at[pl.ds(r, 1), :], z_hbm.at[pl.ds(dst_ref[0, r], 1), :], sem_s)

    def for_rows(fn):
        def body(r, c):
            fn(r)
            return c
        lax.fori_loop(0, tm, body, 0, unroll=8)

    @pl.when(s == 0)
    def _():
        for_rows(lambda r: gather_row(r).start())
        for_rows(lambda r: gather_row(r).wait())

        def cast(j, c):
            rows = pl.ds(pl.multiple_of(j * FFN_ROWS, FFN_ROWS), FFN_ROWS)
            xb[rows, :] = xg[rows, :].astype(BF16)
            return c
        lax.fori_loop(0, tm // FFN_ROWS, cast, 0)

    @pl.when(s < FFN_STEPS)
    def _():
        x = xb[...]
        a = jnp.dot(x, wg_ref[...].astype(BF16), preferred_element_type=F32)
        u = jnp.dot(x, wu_ref[...].astype(BF16), preferred_element_type=F32)
        act[s] = (a * _sigmoid(a) * u).astype(BF16)

    @pl.when((s == FFN_STEPS) & (q > 0))
    def _():
        for_rows(lambda r: scatter_row(r).wait())

    @pl.when(s >= FFN_STEPS)
    def _():
        a_all = jnp.concatenate([act[f] for f in range(FFN_STEPS)], axis=1)
        y = jnp.dot(a_all, wd_ref[...].astype(BF16), preferred_element_type=F32)
        col = pl.multiple_of((s - FFN_STEPS) * tn, tn)
        yacc[:, pl.ds(col, tn)] = y * gate_ref[...]

    @pl.when(s == 2 * FFN_STEPS - 1)
    def _():
        for_rows(lambda r: scatter_row(r).start())

    @pl.when((s == 2 * FFN_STEPS - 1) & (q == nq - 1))
    def _():
        for_rows(lambda r: scatter_row(r).wait())


def _expert_ffn(h2, tok, dst, gate, w_gate, w_up, w_down, layer, tm=1024):
    E, M = tok.shape
    T, D = h2.shape
    F = w_gate.shape[-1]
    tm = min(tm, M)
    nm = M // tm
    tf = F // FFN_STEPS
    tn = D // FFN_STEPS
    last = FFN_STEPS - 1
    tok3 = tok.reshape(E * nm, 1, tm)
    dst3 = dst.reshape(E * nm, 1, tm)
    smem_spec = pl.BlockSpec((None, 1, tm), lambda e, m, s: (e * nm + m, 0, 0), memory_space=pltpu.SMEM)
    kern = functools.partial(_ffn_kernel, tm=tm)
    return pl.pallas_call(
        kern,
        grid=(E, nm, 2 * FFN_STEPS),
        in_specs=[
            smem_spec,
            smem_spec,
            pl.BlockSpec(memory_space=pl.ANY),
            pl.BlockSpec((None, None, D, tf), lambda e, m, s: (layer, e, 0, jnp.minimum(s, last))),
            pl.BlockSpec((None, None, D, tf), lambda e, m, s: (layer, e, 0, jnp.minimum(s, last))),
            pl.BlockSpec((None, None, F, tn), lambda e, m, s: (layer, e, 0, jnp.maximum(s - FFN_STEPS, 0))),
            pl.BlockSpec((None, tm, 1), lambda e, m, s: (e, m, 0)),
        ],
        out_specs=pl.BlockSpec(memory_space=pl.ANY),
        out_shape=jax.ShapeDtypeStruct((E * M, D), F32),
        scratch_shapes=[
            pltpu.VMEM((tm, D), F32),
            pltpu.VMEM((tm, D), BF16),
            pltpu.VMEM((FFN_STEPS, tm, tf), BF16),
            pltpu.VMEM((tm, D), F32),
            pltpu.SemaphoreType.DMA(()),
            pltpu.SemaphoreType.DMA(()),
        ],
        compiler_params=_cparams(("arbitrary", "arbitrary", "arbitrary")),
        name="expert_ffn",
    )(tok3, dst3, h2, w_gate, w_up, w_down, gate)


COMB_TOK = 512
COMB_ROWS = 256
COMB_SLOTS = 6
COMB_AHEAD = 4


def _combine_kernel(start_ref, nch_ref, x_ref, off_ref, cnt_ref, g_ref, z_hbm, o_ref, zbuf, sem, st,
                    *, zrows, final):
    i = pl.program_id(0)
    nt = pl.num_programs(0)
    start = start_ref[i]
    n = nch_ref[i]
    i_next = jnp.minimum(i + 1, nt - 1)
    n_next = jnp.where(i + 1 < nt, nch_ref[i_next], 0)

    @pl.when(i == 0)
    def _():
        st[0] = 0
        st[1] = 0

    gc = st[0]
    pre = st[1]
    o_ref[...] = x_ref[...]
    off = off_ref[...]
    end = off + cnt_ref[...]

    def chunk_start(tile, k):
        return pl.multiple_of(jnp.minimum(start_ref[tile] + k * COMB_ROWS, zrows - COMB_ROWS), 8)

    def copy(tile, k, number):
        slot = lax.rem(number, COMB_SLOTS)
        return pltpu.make_async_copy(z_hbm.at[pl.ds(chunk_start(tile, k), COMB_ROWS), :], zbuf.at[slot],
                                     sem.at[slot])

    for j in range(COMB_AHEAD):
        @pl.when((j >= pre) & (j < n))
        def _():
            copy(i, j, gc + j).start()

    cross = n >= COMB_AHEAD

    def body(k, c):
        copy(i, k, gc + k).wait()
        ahead = k + COMB_AHEAD

        @pl.when(ahead < n)
        def _():
            copy(i, ahead, gc + ahead).start()

        @pl.when((ahead >= n) & cross & (ahead - n < n_next))
        def _():
            copy(i_next, ahead - n, gc + ahead).start()

        rows = chunk_start(i, k) + lax.broadcasted_iota(I32, (1, COMB_ROWS), 1)
        lo = jnp.maximum(off, start + k * COMB_ROWS)
        seg = jnp.where((rows >= lo) & (rows < end), 1.0, 0.0).astype(BF16)
        o_ref[...] += jnp.dot(seg, zbuf[lax.rem(gc + k, COMB_SLOTS)].astype(BF16), preferred_element_type=F32)
        return c

    lax.fori_loop(0, n, body, 0)
    st[0] = gc + n
    st[1] = jnp.where(cross, jnp.minimum(COMB_AHEAD, n_next), 0)

    if final:
        x = o_ref[...]
        ms = jnp.mean(x * x, axis=-1, keepdims=True)
        o_ref[...] = x * lax.rsqrt(ms + EPS) * g_ref[...]


def _combine(x2d, z, off, cnt, gain, final):
    T, D = x2d.shape
    zrows = z.shape[0]
    tt = min(COMB_TOK, T)
    nt = T // tt
    start = (off[::tt] // 8) * 8
    last = (off + cnt)[tt - 1::tt]
    nch = (last - start + COMB_ROWS - 1) // COMB_ROWS
    kern = functools.partial(_combine_kernel, zrows=zrows, final=final)
    grid_spec = pltpu.PrefetchScalarGridSpec(
        num_scalar_prefetch=2,
        grid=(nt,),
        in_specs=[
            pl.BlockSpec((tt, D), lambda i, s, n: (i, 0)),
            pl.BlockSpec((tt, 1), lambda i, s, n: (i, 0)),
            pl.BlockSpec((tt, 1), lambda i, s, n: (i, 0)),
            pl.BlockSpec((1, D), lambda i, s, n: (0, 0)),
            pl.BlockSpec(memory_space=pl.ANY),
        ],
        out_specs=pl.BlockSpec((tt, D), lambda i, s, n: (i, 0)),
        scratch_shapes=[pltpu.VMEM((COMB_SLOTS, COMB_ROWS, D), F32), pltpu.SemaphoreType.DMA((COMB_SLOTS,)),
                        pltpu.SMEM((2,), I32)],
    )
    return pl.pallas_call(
        kern,
        grid_spec=grid_spec,
        out_shape=jax.ShapeDtypeStruct((T, D), F32),
        compiler_params=_cparams(("arbitrary",)),
        name="combine",
    )(start.astype(I32), nch.astype(I32), x2d, off[:, None], cnt[:, None], gain, z)


def moe(x2d, h2, aff, w_gate, w_up, w_down, layer, B, S, gain, final):
    E = N_EXPERTS
    cap = CAPACITY_FACTOR * S // E
    tok, dst, gate, off, cnt = _route(aff, B, S)
    to_em = lambda a: jnp.swapaxes(a.reshape(B, E, cap), 0, 1).reshape(E, B * cap)
    z = _expert_ffn(h2, to_em(tok), to_em(dst), to_em(gate)[:, :, None], w_gate, w_up, w_down, layer)
    return _combine(x2d, z, off.reshape(B * S), cnt.reshape(B * S), gain, final)


def kernel(x, norm_mix_g, w_in, b_gates, conv_w, conv_b, na_rpb, na_norm_g, ml_norm_g, w_out,
           norm_ffn_g, w_router, w_gate, w_up, w_down, final_norm_g):
    B, S, D = x.shape
    T = B * S
    depth = w_in.shape[0]
    x2d = x.reshape(T, D)

    for l in range(depth):
        w_main = w_in[l, :, :N_MAIN].astype(BF16)
        w_gates = jnp.pad(w_in[l, :, N_MAIN:], ((0, 0), (0, LANES - N_GATES))).astype(BF16)
        proj, gates = _in_proj(x2d, norm_mix_g[l][None, :], w_main, w_gates)

        tb = _na_bias_table(na_rpb[l])
        na_o = _na_attention(proj, tb, B, S)

        g4 = gates[:, :N_GATES].reshape(B, S, 4, ML_HEADS)
        g_rows = jnp.pad(jnp.transpose(g4, (0, 3, 2, 1)), ((0, 0), (0, 0), (0, 4), (0, 0)))
        bg_rows = jnp.pad(b_gates[l].reshape(4, ML_HEADS).T, ((0, 0), (0, 4)))[:, :, None]
        ml_o = _mlstm(proj, g_rows, bg_rows.astype(F32), conv_w[l], conv_b[l][None, :],
                      ml_norm_g[l][None, :], B, S)

        w_r = jnp.pad(w_router[l], ((0, 0), (0, LANES - N_EXPERTS))).astype(BF16)
        x2d, h2, aff = _out_proj(na_o, ml_o, x2d, w_out[l].astype(BF16), na_norm_g[l][None, :],
                                 norm_ffn_g[l][None, :], w_r)

        x2d = moe(x2d, h2, aff, w_gate, w_up, w_down, l, B, S, final_norm_g[None, :], l == depth - 1)

    return x2d.reshape(B, S, D)
```

```python
import functools

import numpy as np
import jax
import jax.numpy as jnp
from jax import lax
from jax.experimental import pallas as pl
from jax.experimental.pallas import tpu as pltpu

F32 = jnp.float32
BF16 = jnp.bfloat16

EPS = 1e-6
GRID_W = 64
NA_HEADS = 8
NA_HEAD_DIM = 128
NA_KR = 8
NA_KC = 16
NA_WIDTH = NA_HEADS * NA_HEAD_DIM
NA_LOOKAHEAD = 6
ML_HEADS = 4
ML_QK_DIM = 128
ML_V_DIM = 256
ML_QK_WIDTH = ML_HEADS * ML_QK_DIM
ML_V_WIDTH = ML_HEADS * ML_V_DIM
CONV_W = 5
N_GATES = 4 * ML_HEADS
N_EXPERTS = 16
CAPACITY_FACTOR = 2
N_MAIN = 3 * NA_WIDTH + 2 * ML_QK_WIDTH + 2 * ML_V_WIDTH
LANES = 128
NEG = -1e30
ML_CHUNK = 256
VMEM_LIMIT = 56 * 1024 * 1024


def _cparams(sem):
    return pltpu.CompilerParams(dimension_semantics=sem, vmem_limit_bytes=VMEM_LIMIT)


def _in_proj_kernel(x_ref, g_ref, w_ref, wg_ref, o_ref, og_ref, h_scr):
    @pl.when(pl.program_id(1) == 0)
    def _():
        x = x_ref[...]
        ms = jnp.mean(x * x, axis=-1, keepdims=True)
        h = (x * lax.rsqrt(ms + EPS) * g_ref[...]).astype(BF16)
        h_scr[...] = h
        og_ref[...] = jnp.dot(h, wg_ref[...], preferred_element_type=F32)

    o_ref[...] = jnp.dot(h_scr[...], w_ref[...].astype(BF16), preferred_element_type=F32).astype(o_ref.dtype)


def _in_proj(x2d, g, w_in, w_gates, layer, tm=1024, tn=1024):
    T, D = x2d.shape
    N = N_MAIN
    return pl.pallas_call(
        _in_proj_kernel,
        grid=(T // tm, N // tn),
        in_specs=[
            pl.BlockSpec((tm, D), lambda i, j: (i, 0)),
            pl.BlockSpec((1, D), lambda i, j: (0, 0)),
            pl.BlockSpec((None, D, tn), lambda i, j: (layer, 0, j)),
            pl.BlockSpec((D, LANES), lambda i, j: (0, 0)),
        ],
        out_specs=[
            pl.BlockSpec((tm, tn), lambda i, j: (i, j)),
            pl.BlockSpec((tm, LANES), lambda i, j: (i, 0)),
        ],
        out_shape=[
            jax.ShapeDtypeStruct((T, N), BF16),
            jax.ShapeDtypeStruct((T, LANES), F32),
        ],
        scratch_shapes=[pltpu.VMEM((tm, D), BF16)],
        compiler_params=_cparams(("parallel", "arbitrary")),
        name="in_proj",
    )(x2d, g, w_in, w_gates)


def _na_bias_table(rpb):
    n_dc = 2 * NA_KC - 1
    off = np.arange(NA_KR)[:, None]
    a = np.arange(NA_KR)[None, :]
    dr = a - off + (NA_KR - 1)
    c = np.arange(GRID_W)
    col_start = np.clip(c - NA_KC // 2, 0, GRID_W - NA_KC)
    col_in = (c[None, :] >= col_start[:, None]) & (c[None, :] < col_start[:, None] + NA_KC)
    dc = np.clip(c[None, :] - c[:, None] + (NA_KC - 1), 0, n_dc - 1)
    sel_c = jnp.asarray(dc[:, :, None] == np.arange(n_dc), F32)
    x = jnp.einsum("hij,qkj->hiqk", rpb.astype(F32), sel_c, precision=lax.Precision.HIGHEST)
    tb = jnp.transpose(x[:, dr], (0, 1, 3, 2, 4))
    tb = jnp.where(col_in[None, None, :, None, :], tb, NEG)
    return tb.reshape(rpb.shape[0], NA_KR, GRID_W, NA_KR * GRID_W)


def _na_kernel(q_ref, k_ref, v_ref, tb_ref, o_ref, *, rb, rows):
    i = pl.program_id(2)
    scale = NA_HEAD_DIM ** -0.5
    win = NA_KR * GRID_W

    def scores(rr):
        r = i * rb + rr
        rs = jnp.clip(r - NA_KR // 2, 0, rows - NA_KR)
        start = pl.multiple_of(rs * GRID_W, GRID_W)
        q = q_ref[rr * GRID_W:(rr + 1) * GRID_W, :]
        kw = k_ref[pl.ds(start, win), :]
        s = lax.dot_general(q, kw, (((1,), (1,)), ((), ())), preferred_element_type=F32)
        return s, r - rs, start

    queue = [scores(rr) for rr in range(min(NA_LOOKAHEAD, rb))]
    for rr in range(rb):
        s, off, start = queue.pop(0)
        if rr + NA_LOOKAHEAD < rb:
            queue.append(scores(rr + NA_LOOKAHEAD))
        vw = v_ref[pl.ds(start, win), :]
        s = s * scale + tb_ref[off]
        m = jnp.max(s, axis=-1, keepdims=True)
        p = jnp.exp(s - m)
        l = jnp.sum(p, axis=-1, keepdims=True)
        o = jnp.dot(p.astype(BF16), vw, preferred_element_type=F32)
        o_ref[rr * GRID_W:(rr + 1) * GRID_W, :] = (o / l).astype(o_ref.dtype)


def _na_attention(proj, tb, B, S, rb=64):
    T = B * S
    rows = S // GRID_W
    rb = min(rb, rows)
    assert rows >= NA_KR and rows % rb == 0
    nblk = rows // rb
    tq = rb * GRID_W
    kern = functools.partial(_na_kernel, rb=rb, rows=rows)
    return pl.pallas_call(
        kern,
        grid=(B, NA_HEADS, nblk),
        in_specs=[
            pl.BlockSpec((tq, NA_HEAD_DIM), lambda b, h, i: (b * nblk + i, h)),
            pl.BlockSpec((S, NA_HEAD_DIM), lambda b, h, i: (b, NA_HEADS + h)),
            pl.BlockSpec((S, NA_HEAD_DIM), lambda b, h, i: (b, 2 * NA_HEADS + h)),
            pl.BlockSpec((None, NA_KR, GRID_W, NA_KR * GRID_W), lambda b, h, i: (h, 0, 0, 0)),
        ],
        out_specs=pl.BlockSpec((tq, NA_HEAD_DIM), lambda b, h, i: (b * nblk + i, h)),
        out_shape=jax.ShapeDtypeStruct((T, NA_WIDTH), BF16),
        compiler_params=_cparams(("parallel", "parallel", "arbitrary")),
        name="na_attention",
    )(proj, proj, proj, tb)


def _log_sigmoid(x):
    return jnp.minimum(x, 0.0) - jnp.log(1.0 + jnp.exp(-jnp.abs(x)))


def _sigmoid(x):
    return 1.0 / (1.0 + jnp.exp(-x))


def _mlstm_kernel(q_ref, k_ref, v_ref, og_ref, g_ref, bg_ref, cwq_ref, cwk_ref, cbq_ref, cbk_ref,
                  ng_ref, out_ref, qs, kst, rrow, bcol, hf, hb, cst, nst, *, S, L):
    nc = S // L
    halo = 8

    def conv_silu(src_ref, cw, cbv, c, scale):
        base = pl.multiple_of(c * L, L)
        lo = jnp.maximum(base - halo, 0)
        hi = jnp.minimum(base + L, S - halo)
        prev = src_ref[pl.ds(pl.multiple_of(lo, halo), halo), :].astype(F32)
        nxt = src_ref[pl.ds(pl.multiple_of(hi, halo), halo), :].astype(F32)
        prev = jnp.where(c > 0, prev, 0.0)
        nxt = jnp.where(c < nc - 1, nxt, 0.0)
        main = src_ref[pl.ds(base, L), :].astype(F32)
        blk = jnp.concatenate([prev, main, nxt], axis=0)
        n = L + 2 * halo
        acc = jnp.zeros((L, LANES), F32) + cbv
        for w in range(CONV_W):
            sh = (CONV_W // 2 - w) % n
            xs = blk if sh == 0 else pltpu.roll(blk, sh, 0)
            acc = acc + xs[halo:halo + L, :] * cw[w:w + 1, :]
        return acc * _sigmoid(acc) * scale

    cwq, cbq, cwk, cbk = cwq_ref[...], cbq_ref[...], cwk_ref[...], cbk_ref[...]

    def conv_body(c, carry):
        base = pl.multiple_of(c * L, L)
        qs[pl.ds(base, L), :] = conv_silu(q_ref, cwq, cbq, c, ML_QK_DIM ** -0.5).astype(BF16)
        kst[:, pl.ds(base, L)] = conv_silu(k_ref, cwk, cbk, c, 1.0).T.astype(BF16)
        return carry

    lax.fori_loop(0, nc, conv_body, 0)

    t_idx = lax.broadcasted_iota(jnp.int32, (L, L), 0)
    j_idx = lax.broadcasted_iota(jnp.int32, (L, L), 1)
    upper = (t_idx <= j_idx).astype(BF16)
    lower = (t_idx >= j_idx).astype(BF16)

    def split3(x):
        hi = x.astype(BF16).astype(F32)
        mid = (x - hi).astype(BF16).astype(F32)
        lo = (x - hi - mid).astype(BF16).astype(F32)
        return hi, mid, lo

    def gate_body(c, carry):
        base = pl.multiple_of(c * L, L)
        g = g_ref[:, pl.ds(base, L)] + bg_ref[...]
        hi, mid, lo = split3(_log_sigmoid(g))
        lf3 = jnp.concatenate([hi, mid, lo, jnp.zeros_like(hi)], axis=0).astype(BF16)
        pre = jnp.dot(lf3, upper, preferred_element_type=F32)
        suf = jnp.dot(lf3, lower, preferred_element_type=F32)
        pre = pre[0:8] + pre[8:16] + pre[16:24]
        suf = suf[0:8] + suf[8:16] + suf[16:24]
        cum_f, cum_b = pre[1:2], suf[3:4]
        gx_f = cum_f[:, L - 1:L] - cum_f + g[0:1]
        gx_b = cum_b[:, 0:1] - cum_b + g[2:3]
        r = jnp.concatenate([g[0:1], cum_f, g[2:3], cum_b, gx_f, gx_b, g[0:1] - cum_f, g[2:3] - cum_b],
                            axis=0)
        rrow[:, pl.ds(base, L)] = r
        return carry

    lax.fori_loop(0, nc, gate_body, 0, unroll=4)

    r_all = rrow[...]
    row_all = lax.broadcasted_iota(jnp.int32, (8, S), 0)
    lane_all = lax.broadcasted_iota(jnp.int32, (8, S), 1) & (L - 1)
    pm = r_all
    sm = r_all
    sh = 1
    while sh < L:
        pm = jnp.maximum(pm, jnp.where(lane_all >= sh, pltpu.roll(pm, sh, 1), NEG))
        sm = jnp.maximum(sm, jnp.where(lane_all < L - sh, pltpu.roll(sm, S - sh, 1), NEG))
        sh *= 2
    mi = jnp.where(row_all == 6, pm + pltpu.roll(r_all, 5, 0), sm + pltpu.roll(r_all, 4, 0))
    rrow[...] = jnp.where(row_all >= 6, mi, r_all)

    pick_r = lax.broadcasted_iota(jnp.int32, (LANES, 2 * LANES), 0)
    pick_l = lax.broadcasted_iota(jnp.int32, (LANES, 2 * LANES), 1)

    def picker(k_lo, k_hi):
        want = jnp.where(pick_l < LANES, k_lo, k_hi)
        return ((pick_r < 24) & ((pick_r & 7) == want)).astype(BF16)

    pairs = ((1, 6, 0, 1), (4, 3, 2, 3), (7, 5, 4, 5))

    def bcol_body(c, carry):
        base = pl.multiple_of(c * L, L)
        hi, mid, lo = split3(rrow[:, pl.ds(base, L)])
        x = jnp.concatenate([hi, mid, lo, jnp.zeros((LANES - 24, L), F32)], axis=0)
        xt = x.T.astype(BF16)
        for k_lo, k_hi, d_lo, d_hi in pairs:
            y = jnp.dot(xt, picker(k_lo, k_hi), preferred_element_type=F32)
            bcol[d_lo, pl.ds(base, L), :] = y[:, :LANES]
            bcol[d_hi, pl.ds(base, L), :] = y[:, LANES:]
        return carry

    lax.fori_loop(0, nc, bcol_body, 0, unroll=4)

    fwd_mask = j_idx <= t_idx
    bwd_mask = j_idx >= t_idx
    ones_b = jnp.ones((L, LANES), BF16)

    def wide(x, n):
        return jnp.concatenate([x] * n, axis=1)

    def load(c, d, total_row):
        base = pl.multiple_of(c * L, L)
        qc = qs[pl.ds(base, L), :]
        C = cst[d]
        cum = bcol[3 * d, pl.ds(base, L), :]
        return dict(
            base=base, qc=qc, vc=v_ref[pl.ds(base, L), :], kt=kst[:, pl.ds(base, L)],
            cum=cum, mi=bcol[3 * d + 1, pl.ds(base, L), :], gx=bcol[3 * d + 2, pl.ds(base, L), :],
            i_row=rrow[2 * d:2 * d + 1, pl.ds(base, L)],
            cum_row=rrow[2 * d + 1:2 * d + 2, pl.ds(base, L)],
            total=cum[total_row:total_row + 1, :], C=C,
            s=jnp.dot(qc, kst[:, pl.ds(base, L)], preferred_element_type=F32),
            qC=jnp.dot(qc, C.astype(BF16), preferred_element_type=F32),
            qn=jnp.dot(qc, nst[d].astype(BF16), preferred_element_type=F32))

    def weights(x, mask, m_st):
        m_inter = x["cum"] + m_st
        m_j = jnp.maximum(m_inter, x["mi"])
        dmat = jnp.where(mask, wide(x["cum"], L // LANES) - x["cum_row"] + x["i_row"], NEG)
        qkw = (x["s"] * jnp.exp(dmat - wide(m_j, L // LANES))).astype(BF16)
        inter = jnp.exp(m_inter - m_j)
        m_new = jnp.maximum(x["total"] + m_st, jnp.max(x["gx"], axis=0, keepdims=True))
        wk = jnp.exp(x["gx"] - m_new)
        decay = jnp.exp(x["total"] + m_st - m_new)
        vw = (wide(wk, ML_V_DIM // LANES) * x["vc"].astype(F32)).astype(BF16)
        return qkw, inter, m_j, wk.astype(BF16), vw, decay, m_new

    def finish(x, d, qkw, inter, m_j, wk, vw, decay, h_ref):
        base = x["base"]
        den = inter * x["qn"] + jnp.dot(qkw, ones_b, preferred_element_type=F32)
        rden = 1.0 / jnp.maximum(jnp.abs(den), jnp.exp(-m_j))
        num = wide(inter, ML_V_DIM // LANES) * x["qC"] + jnp.dot(qkw, x["vc"], preferred_element_type=F32)
        h_ref[pl.ds(base, L), :] = num * wide(rden, ML_V_DIM // LANES)
        cst[d] = wide(decay, ML_V_DIM // LANES) * x["C"] + jnp.dot(x["kt"], vw, preferred_element_type=F32)
        nst[d] = decay * nst[d] + jnp.dot(x["kt"], wk, preferred_element_type=F32)

    cst[...] = jnp.zeros(cst.shape, F32)
    nst[...] = jnp.zeros(nst.shape, F32)

    def scan_body(t, carry):
        m_f, m_b = carry
        xf = load(t, 0, L - 1)
        xb = load(nc - 1 - t, 1, 0)
        wf = weights(xf, fwd_mask, m_f)
        finish(xf, 0, *wf[:-1], hf)
        wb = weights(xb, bwd_mask, m_b)
        finish(xb, 1, *wb[:-1], hb)
        return wf[-1], wb[-1]

    z_m = jnp.zeros((1, LANES), F32)
    lax.fori_loop(0, nc, scan_body, (z_m, z_m))

    ng = ng_ref[...]

    def fin_body(c, carry):
        base = pl.multiple_of(c * L, L)
        h = hf[pl.ds(base, L), :] + hb[pl.ds(base, L), :]
        ms = jnp.mean(h * h, axis=-1, keepdims=True)
        y = h * lax.rsqrt(ms + EPS) * ng
        y = y * _sigmoid(og_ref[pl.ds(base, L), :].astype(F32))
        out_ref[pl.ds(base, L), :] = y.astype(out_ref.dtype)
        return carry

    lax.fori_loop(0, nc, fin_body, 0)


def _mlstm(proj, gates_rows, bg_rows, conv_w, conv_b, norm_g, B, S):
    T = B * S
    L = min(ML_CHUNK, S)
    assert S % L == 0 and L % LANES == 0
    q0 = 3 * NA_WIDTH // ML_QK_DIM
    k0 = q0 + ML_HEADS
    v0 = (3 * NA_WIDTH + 2 * ML_QK_WIDTH) // ML_V_DIM
    o0 = v0 + ML_HEADS
    kern = functools.partial(_mlstm_kernel, S=S, L=L)
    return pl.pallas_call(
        kern,
        grid=(B, ML_HEADS),
        in_specs=[
            pl.BlockSpec((S, ML_QK_DIM), lambda b, h: (b, q0 + h)),
            pl.BlockSpec((S, ML_QK_DIM), lambda b, h: (b, k0 + h)),
            pl.BlockSpec((S, ML_V_DIM), lambda b, h: (b, v0 + h)),
            pl.BlockSpec((S, ML_V_DIM), lambda b, h: (b, o0 + h)),
            pl.BlockSpec((None, None, 8, S), lambda b, h: (b, h, 0, 0)),
            pl.BlockSpec((None, 8, 1), lambda b, h: (h, 0, 0)),
            pl.BlockSpec((CONV_W, ML_QK_DIM), lambda b, h: (0, h)),
            pl.BlockSpec((CONV_W, ML_QK_DIM), lambda b, h: (0, ML_HEADS + h)),
            pl.BlockSpec((1, ML_QK_DIM), lambda b, h: (0, h)),
            pl.BlockSpec((1, ML_QK_DIM), lambda b, h: (0, ML_HEADS + h)),
            pl.BlockSpec((1, ML_V_DIM), lambda b, h: (0, h)),
        ],
        out_specs=pl.BlockSpec((S, ML_V_DIM), lambda b, h: (b, h)),
        out_shape=jax.ShapeDtypeStruct((T, ML_V_WIDTH), BF16),
        scratch_shapes=[
            pltpu.VMEM((S, ML_QK_DIM), BF16),
            pltpu.VMEM((ML_QK_DIM, S), BF16),
            pltpu.VMEM((8, S), F32),
            pltpu.VMEM((6, S, LANES), F32),
            pltpu.VMEM((S, ML_V_DIM), F32),
            pltpu.VMEM((S, ML_V_DIM), F32),
            pltpu.VMEM((2, ML_QK_DIM, ML_V_DIM), F32),
            pltpu.VMEM((2, ML_QK_DIM, LANES), F32),
        ],
        compiler_params=_cparams(("parallel", "arbitrary")),
        name="mlstm",
    )(proj, proj, proj, proj, gates_rows, bg_rows, conv_w, conv_w, conv_b, conv_b, norm_g)


def _out_proj_kernel(na_ref, ml_ref, x_ref, w_ref, nag_ref, fg_ref, wr_ref, xo_ref, h_ref, aff_ref):
    na = na_ref[...].astype(F32)
    ms = jnp.mean(na * na, axis=-1, keepdims=True)
    na_n = (na * lax.rsqrt(ms + EPS) * nag_ref[...]).astype(BF16)
    upd = jnp.dot(na_n, w_ref[0:NA_WIDTH, :], preferred_element_type=F32)
    upd = upd + jnp.dot(ml_ref[...], w_ref[NA_WIDTH:, :], preferred_element_type=F32)
    xn = x_ref[...] + upd
    xo_ref[...] = xn
    ms2 = jnp.mean(xn * xn, axis=-1, keepdims=True)
    h = xn * lax.rsqrt(ms2 + EPS) * fg_ref[...]
    h_ref[...] = h
    logits = jnp.dot(h.astype(BF16), wr_ref[...], preferred_element_type=F32)
    lane = lax.broadcasted_iota(jnp.int32, logits.shape, 1)
    logits = jnp.where(lane < N_EXPERTS, logits, NEG)
    m = jnp.max(logits, axis=-1, keepdims=True)
    e = jnp.exp(logits - m)
    aff_ref[...] = e / jnp.sum(e, axis=-1, keepdims=True)


def _out_proj(na_o, ml_o, x2d, w_out, na_g, ffn_g, w_router, tm=512):
    T, D = x2d.shape
    return pl.pallas_call(
        _out_proj_kernel,
        grid=(T // tm,),
        in_specs=[
            pl.BlockSpec((tm, NA_WIDTH), lambda i: (i, 0)),
            pl.BlockSpec((tm, ML_V_WIDTH), lambda i: (i, 0)),
            pl.BlockSpec((tm, D), lambda i: (i, 0)),
            pl.BlockSpec((NA_WIDTH + ML_V_WIDTH, D), lambda i: (0, 0)),
            pl.BlockSpec((1, NA_WIDTH), lambda i: (0, 0)),
            pl.BlockSpec((1, D), lambda i: (0, 0)),
            pl.BlockSpec((D, LANES), lambda i: (0, 0)),
        ],
        out_specs=[
            pl.BlockSpec((tm, D), lambda i: (i, 0)),
            pl.BlockSpec((tm, D), lambda i: (i, 0)),
            pl.BlockSpec((tm, LANES), lambda i: (i, 0)),
        ],
        out_shape=[
            jax.ShapeDtypeStruct((T, D), F32),
            jax.ShapeDtypeStruct((T, D), F32),
            jax.ShapeDtypeStruct((T, LANES), F32),
        ],
        compiler_params=_cparams(("parallel",)),
        name="out_proj",
    )(na_o, ml_o, x2d, w_out, na_g, ffn_g, w_router)


I32 = jnp.int32
SLOT_LO = 32
TOK_LO = 64
CHUNK = 256


def _route_kernel(aff_ref, tok_ref, dst_ref, gate_ref, off_ref, cnt_ref, a_scr, sel_scr, pos_scr, tmp_scr,
                  *, S, cap):
    b = pl.program_id(0)
    E = N_EXPERTS
    nchunk = S // CHUNK
    zrows = E * cap
    n_hi = cap // SLOT_LO

    TB = 512
    for c in range(S // TB):
        a_scr[:, c * TB:(c + 1) * TB] = aff_ref[c * TB:(c + 1) * TB, :].T[0:E, :]
    A = a_scr[...]

    def thr_body(i, thr):
        cand = thr | jnp.left_shift(jnp.int32(1), 30 - i)
        cand_f = lax.bitcast_convert_type(cand, F32)
        cnt = jnp.sum((a_scr[...] >= cand_f).astype(F32), axis=1, keepdims=True)
        return jnp.where(cnt >= cap, cand, thr)

    thr = lax.fori_loop(0, 31, thr_body, jnp.zeros((E, 1), I32))
    thr_f = lax.bitcast_convert_type(thr, F32)
    gt = A > thr_f
    eq = A == thr_f
    need = cap - jnp.sum(gt.astype(F32), axis=1, keepdims=True)
    idx = lax.broadcasted_iota(I32, (E, S), 1)

    nbits = S.bit_length() - 1

    def tie_body(i, jt):
        cand = jt | jnp.left_shift(jnp.int32(1), nbits - 1 - i)
        c = jnp.sum((eq & (idx < cand)).astype(F32), axis=1, keepdims=True)
        return jnp.where(c < need, cand, jt)

    jt = lax.fori_loop(0, nbits, tie_body, jnp.zeros((E, 1), I32))
    sel = (gt | (eq & (idx <= jt))).astype(F32)
    sel_scr[...] = sel

    r_i = lax.broadcasted_iota(I32, (CHUNK, CHUNK), 0)
    c_i = lax.broadcasted_iota(I32, (CHUNK, CHUNK), 1)
    strict_upper = (r_i < c_i).astype(BF16)
    e_r = lax.broadcasted_iota(I32, (E, E), 0)
    e_c = lax.broadcasted_iota(I32, (E, E), 1)
    strict_lower = (e_c < e_r).astype(BF16)
    run = jnp.zeros((E, 1), F32)
    run_t = jnp.zeros((1, 1), F32)
    for c in range(nchunk):
        sc = sel[:, c * CHUNK:(c + 1) * CHUNK]
        scb = sc.astype(BF16)
        pos_scr[:, c * CHUNK:(c + 1) * CHUNK] = jnp.dot(scb, strict_upper, preferred_element_type=F32) + run
        run = run + jnp.sum(sc, axis=1, keepdims=True)
        cnt_c = jnp.sum(sc, axis=0, keepdims=True)
        cnt8 = jnp.broadcast_to(cnt_c, (8, CHUNK)).astype(BF16)
        base_c = jnp.dot(cnt8, strict_upper, preferred_element_type=F32)[0:1] + run_t
        run_t = run_t + jnp.sum(cnt_c, axis=1, keepdims=True)
        rank_c = jnp.dot(strict_lower, scb, preferred_element_type=F32)
        zoff = (b * zrows).astype(F32)
        tmp_scr[:, c * CHUNK:(c + 1) * CHUNK] = base_c + rank_c + zoff
        off_ref[:, c * CHUNK:(c + 1) * CHUNK] = (base_c + zoff).astype(I32)
        cnt_ref[:, c * CHUNK:(c + 1) * CHUNK] = cnt_c.astype(I32)

    lo_id = lax.broadcasted_iota(I32, (SLOT_LO, S), 0).astype(F32)
    hi_id = lax.broadcasted_iota(I32, (n_hi, S), 0).astype(F32)
    t_row = lax.broadcasted_iota(I32, (1, S), 1)
    t_hi = (t_row // TOK_LO).astype(F32)
    t_lo = (t_row % TOK_LO).astype(F32)

    def split3(x):
        hi = x.astype(BF16).astype(F32)
        mid = (x - hi).astype(BF16).astype(F32)
        lo = (x - hi - mid).astype(BF16).astype(F32)
        return hi, mid, lo

    for e in range(E):
        sel_e = sel_scr[e:e + 1, :]
        pos_e = pos_scr[e:e + 1, :]
        p_hi = jnp.floor(pos_e * (1.0 / SLOT_LO))
        p_lo = pos_e - p_hi * SLOT_LO
        oh_lo = jnp.where((lo_id == p_lo) & (sel_e > 0), 1.0, 0.0).astype(BF16)
        oh_hi = jnp.where(hi_id == p_hi, 1.0, 0.0)
        d = tmp_scr[e:e + 1, :]
        d2 = jnp.floor(d * (1.0 / 1024.0))
        d1 = jnp.floor((d - d2 * 1024.0) * (1.0 / 32.0))
        d0 = d - d2 * 1024.0 - d1 * 32.0
        g_hi, g_mid, g_lo = split3(a_scr[e:e + 1, :])
        vals = (t_hi, t_lo, d2, d1, d0, g_hi, g_mid, g_lo)
        stack = jnp.concatenate([oh_hi * v for v in vals], axis=0).astype(BF16)
        r = lax.dot_general(stack, oh_lo, (((1,), (1,)), ((), ())), preferred_element_type=F32)
        rr = [r[k * n_hi:(k + 1) * n_hi] for k in range(8)]
        tok_ref[e] = (rr[0] * TOK_LO + rr[1]).astype(I32) + b * S
        dst_ref[e] = (rr[2] * 1024.0 + rr[3] * 32.0 + rr[4]).astype(I32)
        gate_ref[e] = rr[5] + rr[6] + rr[7]


def _route(aff, B, S):
    cap = CAPACITY_FACTOR * S // N_EXPERTS
    n_hi = cap // SLOT_LO
    E = N_EXPERTS
    kern = functools.partial(_route_kernel, S=S, cap=cap)
    lists = jax.ShapeDtypeStruct((B, E, n_hi, SLOT_LO), I32)
    return pl.pallas_call(
        kern,
        grid=(B,),
        in_specs=[pl.BlockSpec((S, LANES), lambda b: (b, 0))],
        out_specs=[
            pl.BlockSpec((None, E, n_hi, SLOT_LO), lambda b: (b, 0, 0, 0)),
            pl.BlockSpec((None, E, n_hi, SLOT_LO), lambda b: (b, 0, 0, 0)),
            pl.BlockSpec((None, E, n_hi, SLOT_LO), lambda b: (b, 0, 0, 0)),
            pl.BlockSpec((None, 1, S), lambda b: (b, 0, 0)),
            pl.BlockSpec((None, 1, S), lambda b: (b, 0, 0)),
        ],
        out_shape=[lists, lists, jax.ShapeDtypeStruct((B, E, n_hi, SLOT_LO), F32),
                   jax.ShapeDtypeStruct((B, 1, S), I32), jax.ShapeDtypeStruct((B, 1, S), I32)],
        scratch_shapes=[pltpu.VMEM((E, S), F32)] * 4,
        compiler_params=_cparams(("parallel",)),
        name="route",
    )(aff)


FFN_ROWS = 128
FFN_STEPS = 4


def _ffn_kernel(tok_ref, dst_ref, h_hbm, wg_ref, wu_ref, wd_ref, gate_ref, z_hbm, xg, xb, act, yacc, gcol,
                sem_g, sem_s, *, tm):
    s = pl.program_id(2)
    q = pl.program_id(0) * pl.num_programs(1) + pl.program_id(1)
    nq = pl.num_programs(0) * pl.num_programs(1)
    tn = yacc.shape[1] // FFN_STEPS

    def gather_row(r):
        return pltpu.make_async_copy(h_hbm.at[pl.ds(tok_ref[0, r], 1), :], xg.at[pl.ds(r, 1), :], sem_g)

    def scatter_row(r):
        return pltpu.make_async_copy(yacc.at[pl.ds(r, 1), :], z_hbm.at[pl.ds(dst_ref[0, r], 1), :], sem_s)

    def for_rows(fn):
        def body(r, c):
            fn(r)
            return c
        lax.fori_loop(0, tm, body, 0, unroll=8)

    @pl.when(s == 0)
    def _():
        for_rows(lambda r: gather_row(r).start())
        for_rows(lambda r: gather_row(r).wait())

        def cast(j, c):
            rows = pl.ds(pl.multiple_of(j * FFN_ROWS, FFN_ROWS), FFN_ROWS)
            xb[rows, :] = xg[rows, :].astype(BF16)
            return c
        lax.fori_loop(0, tm // FFN_ROWS, cast, 0)

    @pl.when(s < FFN_STEPS)
    def _():
        x = xb[...]
        a = jnp.dot(x, wg_ref[...].astype(BF16), preferred_element_type=F32)
        u = jnp.dot(x, wu_ref[...].astype(BF16), preferred_element_type=F32)
        act[s] = (a * _sigmoid(a) * u).astype(BF16)

    @pl.when((s == FFN_STEPS) & (q > 0))
    def _():
        for_rows(lambda r: scatter_row(r).wait())

    @pl.when(s == FFN_STEPS)
    def _():
        gcol[...] = jnp.broadcast_to(gate_ref[...], (LANES, tm)).T

    @pl.when(s >= FFN_STEPS)
    def _():
        a_all = jnp.concatenate([act[f] for f in range(FFN_STEPS)], axis=1)
        y = jnp.dot(a_all, wd_ref[...].astype(BF16), preferred_element_type=F32)
        col = pl.multiple_of((s - FFN_STEPS) * tn, tn)
        yacc[:, pl.ds(col, tn)] = y * jnp.concatenate([gcol[...]] * (tn // LANES), axis=1)

    @pl.when(s == 2 * FFN_STEPS - 1)
    def _():
        for_rows(lambda r: scatter_row(r).start())

    @pl.when((s == 2 * FFN_STEPS - 1) & (q == nq - 1))
    def _():
        for_rows(lambda r: scatter_row(r).wait())


def _expert_ffn(h2, tok, dst, gate, w_gate, w_up, w_down, layer, tm=1024):
    E, M = tok.shape
    T, D = h2.shape
    F = w_gate.shape[-1]
    tm = min(tm, M)
    nm = M // tm
    tf = F // FFN_STEPS
    tn = D // FFN_STEPS
    last = FFN_STEPS - 1
    tok3 = tok.reshape(E * nm, 1, tm)
    dst3 = dst.reshape(E * nm, 1, tm)
    gate3 = gate.reshape(E * nm, 1, tm)
    smem_spec = pl.BlockSpec((None, 1, tm), lambda e, m, s: (e * nm + m, 0, 0), memory_space=pltpu.SMEM)
    kern = functools.partial(_ffn_kernel, tm=tm)
    return pl.pallas_call(
        kern,
        grid=(E, nm, 2 * FFN_STEPS),
        in_specs=[
            smem_spec,
            smem_spec,
            pl.BlockSpec(memory_space=pl.ANY),
            pl.BlockSpec((None, None, D, tf), lambda e, m, s: (layer, e, 0, jnp.minimum(s, last))),
            pl.BlockSpec((None, None, D, tf), lambda e, m, s: (layer, e, 0, jnp.minimum(s, last))),
            pl.BlockSpec((None, None, F, tn), lambda e, m, s: (layer, e, 0, jnp.maximum(s - FFN_STEPS, 0))),
            pl.BlockSpec((None, 1, tm), lambda e, m, s: (e * nm + m, 0, 0)),
        ],
        out_specs=pl.BlockSpec(memory_space=pl.ANY),
        out_shape=jax.ShapeDtypeStruct((E * M, D), F32),
        scratch_shapes=[
            pltpu.VMEM((tm, D), F32),
            pltpu.VMEM((tm, D), BF16),
            pltpu.VMEM((FFN_STEPS, tm, tf), BF16),
            pltpu.VMEM((tm, D), F32),
            pltpu.VMEM((tm, LANES), F32),
            pltpu.SemaphoreType.DMA(()),
            pltpu.SemaphoreType.DMA(()),
        ],
        compiler_params=_cparams(("arbitrary", "arbitrary", "arbitrary")),
        name="expert_ffn",
    )(tok3, dst3, h2, w_gate, w_up, w_down, gate3)


COMB_TOK = 512
COMB_ROWS = 256
COMB_SLOTS = 6
COMB_AHEAD = 4


def _combine_kernel(start_ref, nch_ref, x_ref, off_ref, cnt_ref, g_ref, z_hbm, o_ref, zbuf, sem, st,
                    *, zrows, final):
    i = pl.program_id(0)
    nt = pl.num_programs(0)
    start = start_ref[i]
    n = nch_ref[i]
    i_next = jnp.minimum(i + 1, nt - 1)
    n_next = jnp.where(i + 1 < nt, nch_ref[i_next], 0)

    @pl.when(i == 0)
    def _():
        st[0] = 0
        st[1] = 0

    gc = st[0]
    pre = st[1]
    o_ref[...] = x_ref[...]
    off = off_ref[...]
    end = off + cnt_ref[...]

    def chunk_start(tile, k):
        return pl.multiple_of(jnp.minimum(start_ref[tile] + k * COMB_ROWS, zrows - COMB_ROWS), 8)

    def copy(tile, k, number):
        slot = lax.rem(number, COMB_SLOTS)
        return pltpu.make_async_copy(z_hbm.at[pl.ds(chunk_start(tile, k), COMB_ROWS), :], zbuf.at[slot],
                                     sem.at[slot])

    for j in range(COMB_AHEAD):
        @pl.when((j >= pre) & (j < n))
        def _():
            copy(i, j, gc + j).start()

    cross = n >= COMB_AHEAD

    def body(k, c):
        copy(i, k, gc + k).wait()
        ahead = k + COMB_AHEAD

        @pl.when(ahead < n)
        def _():
            copy(i, ahead, gc + ahead).start()

        @pl.when((ahead >= n) & cross & (ahead - n < n_next))
        def _():
            copy(i_next, ahead - n, gc + ahead).start()

        rows = chunk_start(i, k) + lax.broadcasted_iota(I32, (1, COMB_ROWS), 1)
        lo = jnp.maximum(off, start + k * COMB_ROWS)
        seg = jnp.where((rows >= lo) & (rows < end), 1.0, 0.0).astype(BF16)
        o_ref[...] += jnp.dot(seg, zbuf[lax.rem(gc + k, COMB_SLOTS)].astype(BF16), preferred_element_type=F32)
        return c

    lax.fori_loop(0, n, body, 0)
    st[0] = gc + n
    st[1] = jnp.where(cross, jnp.minimum(COMB_AHEAD, n_next), 0)

    if final:
        x = o_ref[...]
        ms = jnp.mean(x * x, axis=-1, keepdims=True)
        o_ref[...] = x * lax.rsqrt(ms + EPS) * g_ref[...]


def _combine(x2d, z, off, cnt, gain, final):
    T, D = x2d.shape
    zrows = z.shape[0]
    tt = min(COMB_TOK, T)
    nt = T // tt
    start = (off[::tt] // 8) * 8
    last = (off + cnt)[tt - 1::tt]
    nch = (last - start + COMB_ROWS - 1) // COMB_ROWS
    kern = functools.partial(_combine_kernel, zrows=zrows, final=final)
    grid_spec = pltpu.PrefetchScalarGridSpec(
        num_scalar_prefetch=2,
        grid=(nt,),
        in_specs=[
            pl.BlockSpec((tt, D), lambda i, s, n: (i, 0)),
            pl.BlockSpec((tt, 1), lambda i, s, n: (i, 0)),
            pl.BlockSpec((tt, 1), lambda i, s, n: (i, 0)),
            pl.BlockSpec((1, D), lambda i, s, n: (0, 0)),
            pl.BlockSpec(memory_space=pl.ANY),
        ],
        out_specs=pl.BlockSpec((tt, D), lambda i, s, n: (i, 0)),
        scratch_shapes=[pltpu.VMEM((COMB_SLOTS, COMB_ROWS, D), F32), pltpu.SemaphoreType.DMA((COMB_SLOTS,)),
                        pltpu.SMEM((2,), I32)],
    )
    return pl.pallas_call(
        kern,
        grid_spec=grid_spec,
        out_shape=jax.ShapeDtypeStruct((T, D), F32),
        compiler_params=_cparams(("arbitrary",)),
        name="combine",
    )(start.astype(I32), nch.astype(I32), x2d, off[:, None], cnt[:, None], gain, z)


def moe(x2d, h2, aff, w_gate, w_up, w_down, layer, B, S, gain, final):
    E = N_EXPERTS
    cap = CAPACITY_FACTOR * S // E
    tok, dst, gate, off, cnt = _route(aff, B, S)
    to_em = lambda a: jnp.swapaxes(a.reshape(B, E, cap), 0, 1).reshape(E, B * cap)
    z = _expert_ffn(h2, to_em(tok), to_em(dst), to_em(gate), w_gate, w_up, w_down, layer)
    return _combine(x2d, z, off.reshape(B * S), cnt.reshape(B * S), gain, final)


def kernel(x, norm_mix_g, w_in, b_gates, conv_w, conv_b, na_rpb, na_norm_g, ml_norm_g, w_out,
           norm_ffn_g, w_router, w_gate, w_up, w_down, final_norm_g):
    B, S, D = x.shape
    T = B * S
    depth = w_in.shape[0]
    x2d = x.reshape(T, D)

    for l in range(depth):
        w_gates = jnp.pad(w_in[l, :, N_MAIN:], ((0, 0), (0, LANES - N_GATES))).astype(BF16)
        proj, gates = _in_proj(x2d, norm_mix_g[l][None, :], w_in, w_gates, l)

        tb = _na_bias_table(na_rpb[l])
        na_o = _na_attention(proj, tb, B, S)

        g4 = gates[:, :N_GATES].reshape(B, S, 4, ML_HEADS)
        g_rows = jnp.pad(jnp.transpose(g4, (0, 3, 2, 1)), ((0, 0), (0, 0), (0, 4), (0, 0)))
        bg_rows = jnp.pad(b_gates[l].reshape(4, ML_HEADS).T, ((0, 0), (0, 4)))[:, :, None]
        ml_o = _mlstm(proj, g_rows, bg_rows.astype(F32), conv_w[l], conv_b[l][None, :],
                      ml_norm_g[l][None, :], B, S)

        w_r = jnp.pad(w_router[l], ((0, 0), (0, LANES - N_EXPERTS))).astype(BF16)
        x2d, h2, aff = _out_proj(na_o, ml_o, x2d, w_out[l].astype(BF16), na_norm_g[l][None, :],
                                 norm_ffn_g[l][None, :], w_r)

        x2d = moe(x2d, h2, aff, w_gate, w_up, w_down, l, B, S, final_norm_g[None, :], l == depth - 1)

    return x2d.reshape(B, S, D)
```

```python
import functools

import numpy as np
import jax
import jax.numpy as jnp
from jax import lax
from jax.experimental import pallas as pl
from jax.experimental.pallas import tpu as pltpu

F32 = jnp.float32
BF16 = jnp.bfloat16

EPS = 1e-6
GRID_W = 64
NA_HEADS = 8
NA_HEAD_DIM = 128
NA_KR = 8
NA_KC = 16
NA_WIDTH = NA_HEADS * NA_HEAD_DIM
NA_LOOKAHEAD = 6
ML_HEADS = 4
ML_QK_DIM = 128
ML_V_DIM = 256
ML_QK_WIDTH = ML_HEADS * ML_QK_DIM
ML_V_WIDTH = ML_HEADS * ML_V_DIM
CONV_W = 5
N_GATES = 4 * ML_HEADS
N_EXPERTS = 16
CAPACITY_FACTOR = 2
N_MAIN = 3 * NA_WIDTH + 2 * ML_QK_WIDTH + 2 * ML_V_WIDTH
LANES = 128
NEG = -1e30
ML_CHUNK = 256
VMEM_LIMIT = 56 * 1024 * 1024


def _cparams(sem):
    return pltpu.CompilerParams(dimension_semantics=sem, vmem_limit_bytes=VMEM_LIMIT)


_NT = (((1,), (1,)), ((), ()))


def _in_proj_kernel(x_ref, g_ref, w_ref, wg_ref, o_ref, og_ref, h_scr):
    @pl.when(pl.program_id(1) == 0)
    def _():
        x = x_ref[...]
        ms = jnp.mean(x * x, axis=-1, keepdims=True)
        h = (x * lax.rsqrt(ms + EPS) * g_ref[...]).astype(BF16)
        h_scr[...] = h
        og_ref[...] = lax.dot_general(h, wg_ref[...].astype(BF16), _NT, preferred_element_type=F32)

    o_ref[...] = lax.dot_general(h_scr[...], w_ref[...], _NT, preferred_element_type=F32).astype(o_ref.dtype)


def _cast_kernel(w_ref, o_ref):
    o_ref[...] = w_ref[...].astype(o_ref.dtype)


def _cast_main_weights(w_t, tn=1024):
    depth, _, D = w_t.shape
    return pl.pallas_call(
        _cast_kernel,
        grid=(depth, N_MAIN // tn),
        in_specs=[pl.BlockSpec((None, tn, D), lambda l, j: (l, j, 0))],
        out_specs=pl.BlockSpec((None, tn, D), lambda l, j: (l, j, 0)),
        out_shape=jax.ShapeDtypeStruct((depth, N_MAIN, D), BF16),
        compiler_params=_cparams(("parallel", "parallel")),
        name="cast_weights",
    )(w_t)


def _in_proj(x2d, g, w_main_t, w_t, layer, tm=1024, tn=2048):
    T, D = x2d.shape
    N = N_MAIN
    return pl.pallas_call(
        _in_proj_kernel,
        grid=(T // tm, N // tn),
        in_specs=[
            pl.BlockSpec((tm, D), lambda i, j: (i, 0)),
            pl.BlockSpec((1, D), lambda i, j: (0, 0)),
            pl.BlockSpec((None, tn, D), lambda i, j: (layer, j, 0)),
            pl.BlockSpec((None, N_GATES, D), lambda i, j: (layer, N_MAIN // N_GATES, 0)),
        ],
        out_specs=[
            pl.BlockSpec((tm, tn), lambda i, j: (i, j)),
            pl.BlockSpec((tm, N_GATES), lambda i, j: (i, 0)),
        ],
        out_shape=[
            jax.ShapeDtypeStruct((T, N), BF16),
            jax.ShapeDtypeStruct((T, N_GATES), F32),
        ],
        scratch_shapes=[pltpu.VMEM((tm, D), BF16)],
        compiler_params=_cparams(("parallel", "arbitrary")),
        name="in_proj",
    )(x2d, g, w_main_t, w_t)


def _na_bias_table(rpb):
    n_dc = 2 * NA_KC - 1
    off = np.arange(NA_KR)[:, None]
    a = np.arange(NA_KR)[None, :]
    dr = a - off + (NA_KR - 1)
    c = np.arange(GRID_W)
    col_start = np.clip(c - NA_KC // 2, 0, GRID_W - NA_KC)
    col_in = (c[None, :] >= col_start[:, None]) & (c[None, :] < col_start[:, None] + NA_KC)
    dc = np.clip(c[None, :] - c[:, None] + (NA_KC - 1), 0, n_dc - 1)
    sel_c = jnp.asarray(dc[:, :, None] == np.arange(n_dc), F32)
    x = jnp.einsum("hij,qkj->hiqk", rpb.astype(F32), sel_c, precision=lax.Precision.HIGHEST)
    tb = jnp.transpose(x[:, dr], (0, 1, 3, 2, 4))
    tb = jnp.where(col_in[None, None, :, None, :], tb, NEG)
    return tb.reshape(rpb.shape[0], NA_KR, GRID_W, NA_KR * GRID_W)


def _na_kernel(q_ref, k_ref, v_ref, tb_ref, o_ref, *, rb, rows):
    i = pl.program_id(2)
    scale = NA_HEAD_DIM ** -0.5
    win = NA_KR * GRID_W

    def scores(rr):
        r = i * rb + rr
        rs = jnp.clip(r - NA_KR // 2, 0, rows - NA_KR)
        start = pl.multiple_of(rs * GRID_W, GRID_W)
        q = q_ref[rr * GRID_W:(rr + 1) * GRID_W, :]
        kw = k_ref[pl.ds(start, win), :]
        s = lax.dot_general(q, kw, (((1,), (1,)), ((), ())), preferred_element_type=F32)
        return s, r - rs, start

    queue = [scores(rr) for rr in range(min(NA_LOOKAHEAD, rb))]
    for rr in range(rb):
        s, off, start = queue.pop(0)
        if rr + NA_LOOKAHEAD < rb:
            queue.append(scores(rr + NA_LOOKAHEAD))
        vw = v_ref[pl.ds(start, win), :]
        s = s * scale + tb_ref[off]
        m = jnp.max(s, axis=-1, keepdims=True)
        p = jnp.exp(s - m)
        l = jnp.sum(p, axis=-1, keepdims=True)
        o = jnp.dot(p.astype(BF16), vw, preferred_element_type=F32)
        o_ref[rr * GRID_W:(rr + 1) * GRID_W, :] = (o / l).astype(o_ref.dtype)


def _na_attention(proj, tb, B, S, rb=64):
    T = B * S
    rows = S // GRID_W
    rb = min(rb, rows)
    assert rows >= NA_KR and rows % rb == 0
    nblk = rows // rb
    tq = rb * GRID_W
    kern = functools.partial(_na_kernel, rb=rb, rows=rows)
    return pl.pallas_call(
        kern,
        grid=(B, NA_HEADS, nblk),
        in_specs=[
            pl.BlockSpec((tq, NA_HEAD_DIM), lambda b, h, i: (b * nblk + i, h)),
            pl.BlockSpec((S, NA_HEAD_DIM), lambda b, h, i: (b, NA_HEADS + h)),
            pl.BlockSpec((S, NA_HEAD_DIM), lambda b, h, i: (b, 2 * NA_HEADS + h)),
            pl.BlockSpec((None, NA_KR, GRID_W, NA_KR * GRID_W), lambda b, h, i: (h, 0, 0, 0)),
        ],
        out_specs=pl.BlockSpec((tq, NA_HEAD_DIM), lambda b, h, i: (b * nblk + i, h)),
        out_shape=jax.ShapeDtypeStruct((T, NA_WIDTH), BF16),
        compiler_params=_cparams(("parallel", "parallel", "arbitrary")),
        name="na_attention",
    )(proj, proj, proj, tb)


def _log_sigmoid(x):
    return jnp.minimum(x, 0.0) - jnp.log(1.0 + jnp.exp(-jnp.abs(x)))


def _sigmoid(x):
    return 1.0 / (1.0 + jnp.exp(-x))


def _mlstm_kernel(q_ref, k_ref, v_ref, og_ref, g_ref, bg_ref, cwq_ref, cwk_ref, cbq_ref, cbk_ref,
                  ng_ref, out_ref, qs, kst, rrow, bcol, hf, hb, cst, nst, *, S, L):
    nc = S // L
    halo = 8

    def conv_silu(src_ref, cw, cbv, c, scale):
        base = pl.multiple_of(c * L, L)
        lo = jnp.maximum(base - halo, 0)
        hi = jnp.minimum(base + L, S - halo)
        prev = src_ref[pl.ds(pl.multiple_of(lo, halo), halo), :].astype(F32)
        nxt = src_ref[pl.ds(pl.multiple_of(hi, halo), halo), :].astype(F32)
        prev = jnp.where(c > 0, prev, 0.0)
        nxt = jnp.where(c < nc - 1, nxt, 0.0)
        main = src_ref[pl.ds(base, L), :].astype(F32)
        blk = jnp.concatenate([prev, main, nxt], axis=0)
        n = L + 2 * halo
        acc = jnp.zeros((L, LANES), F32) + cbv
        for w in range(CONV_W):
            sh = (CONV_W // 2 - w) % n
            xs = blk if sh == 0 else pltpu.roll(blk, sh, 0)
            acc = acc + xs[halo:halo + L, :] * cw[w:w + 1, :]
        return acc * _sigmoid(acc) * scale

    cwq, cbq, cwk, cbk = cwq_ref[...], cbq_ref[...], cwk_ref[...], cbk_ref[...]

    def conv_body(c, carry):
        base = pl.multiple_of(c * L, L)
        qs[pl.ds(base, L), :] = conv_silu(q_ref, cwq, cbq, c, ML_QK_DIM ** -0.5).astype(BF16)
        kst[:, pl.ds(base, L)] = conv_silu(k_ref, cwk, cbk, c, 1.0).T.astype(BF16)
        return carry

    lax.fori_loop(0, nc, conv_body, 0)

    t_idx = lax.broadcasted_iota(jnp.int32, (L, L), 0)
    j_idx = lax.broadcasted_iota(jnp.int32, (L, L), 1)
    upper = (t_idx <= j_idx).astype(BF16)
    lower = (t_idx >= j_idx).astype(BF16)

    def split3(x):
        hi = x.astype(BF16).astype(F32)
        mid = (x - hi).astype(BF16).astype(F32)
        lo = (x - hi - mid).astype(BF16).astype(F32)
        return hi, mid, lo

    def gate_body(c, carry):
        base = pl.multiple_of(c * L, L)
        g = g_ref[:, pl.ds(base, L)] + bg_ref[...]
        hi, mid, lo = split3(_log_sigmoid(g))
        lf3 = jnp.concatenate([hi, mid, lo, jnp.zeros_like(hi)], axis=0).astype(BF16)
        pre = jnp.dot(lf3, upper, preferred_element_type=F32)
        suf = jnp.dot(lf3, lower, preferred_element_type=F32)
        pre = pre[0:8] + pre[8:16] + pre[16:24]
        suf = suf[0:8] + suf[8:16] + suf[16:24]
        cum_f, cum_b = pre[1:2], suf[3:4]
        gx_f = cum_f[:, L - 1:L] - cum_f + g[0:1]
        gx_b = cum_b[:, 0:1] - cum_b + g[2:3]
        r = jnp.concatenate([g[0:1], cum_f, g[2:3], cum_b, gx_f, gx_b, g[0:1] - cum_f, g[2:3] - cum_b],
                            axis=0)
        rrow[:, pl.ds(base, L)] = r
        return carry

    lax.fori_loop(0, nc, gate_body, 0, unroll=4)

    r_all = rrow[...]
    row_all = lax.broadcasted_iota(jnp.int32, (8, S), 0)
    lane_all = lax.broadcasted_iota(jnp.int32, (8, S), 1) & (L - 1)
    pm = r_all
    sm = r_all
    sh = 1
    while sh < L:
        pm = jnp.maximum(pm, jnp.where(lane_all >= sh, pltpu.roll(pm, sh, 1), NEG))
        sm = jnp.maximum(sm, jnp.where(lane_all < L - sh, pltpu.roll(sm, S - sh, 1), NEG))
        sh *= 2
    mi = jnp.where(row_all == 6, pm + pltpu.roll(r_all, 5, 0), sm + pltpu.roll(r_all, 4, 0))
    rrow[...] = jnp.where(row_all >= 6, mi, r_all)

    pick_r = lax.broadcasted_iota(jnp.int32, (LANES, 2 * LANES), 0)
    pick_l = lax.broadcasted_iota(jnp.int32, (LANES, 2 * LANES), 1)

    def picker(k_lo, k_hi):
        want = jnp.where(pick_l < LANES, k_lo, k_hi)
        return ((pick_r < 24) & ((pick_r & 7) == want)).astype(BF16)

    pairs = ((1, 6, 0, 1), (4, 3, 2, 3), (7, 5, 4, 5))

    def bcol_body(c, carry):
        base = pl.multiple_of(c * L, L)
        hi, mid, lo = split3(rrow[:, pl.ds(base, L)])
        x = jnp.concatenate([hi, mid, lo, jnp.zeros((LANES - 24, L), F32)], axis=0)
        xt = x.T.astype(BF16)
        for k_lo, k_hi, d_lo, d_hi in pairs:
            y = jnp.dot(xt, picker(k_lo, k_hi), preferred_element_type=F32)
            bcol[d_lo, pl.ds(base, L), :] = y[:, :LANES]
            bcol[d_hi, pl.ds(base, L), :] = y[:, LANES:]
        return carry

    lax.fori_loop(0, nc, bcol_body, 0, unroll=4)

    fwd_mask = j_idx <= t_idx
    bwd_mask = j_idx >= t_idx
    ones_b = jnp.ones((L, LANES), BF16)

    def wide(x, n):
        return jnp.concatenate([x] * n, axis=1)

    def load(c, d, total_row):
        base = pl.multiple_of(c * L, L)
        qc = qs[pl.ds(base, L), :]
        C = cst[d]
        cum = bcol[3 * d, pl.ds(base, L), :]
        return dict(
            base=base, qc=qc, vc=v_ref[pl.ds(base, L), :], kt=kst[:, pl.ds(base, L)],
            cum=cum, mi=bcol[3 * d + 1, pl.ds(base, L), :], gx=bcol[3 * d + 2, pl.ds(base, L), :],
            i_row=rrow[2 * d:2 * d + 1, pl.ds(base, L)],
            cum_row=rrow[2 * d + 1:2 * d + 2, pl.ds(base, L)],
            total=cum[total_row:total_row + 1, :], C=C,
            s=jnp.dot(qc, kst[:, pl.ds(base, L)], preferred_element_type=F32),
            qC=jnp.dot(qc, C.astype(BF16), preferred_element_type=F32),
            qn=jnp.dot(qc, nst[d].astype(BF16), preferred_element_type=F32))

    def weights(x, mask, m_st):
        m_inter = x["cum"] + m_st
        m_j = jnp.maximum(m_inter, x["mi"])
        dmat = jnp.where(mask, wide(x["cum"], L // LANES) - x["cum_row"] + x["i_row"], NEG)
        qkw = (x["s"] * jnp.exp(dmat - wide(m_j, L // LANES))).astype(BF16)
        inter = jnp.exp(m_inter - m_j)
        m_new = jnp.maximum(x["total"] + m_st, jnp.max(x["gx"], axis=0, keepdims=True))
        wk = jnp.exp(x["gx"] - m_new)
        decay = jnp.exp(x["total"] + m_st - m_new)
        vw = (wide(wk, ML_V_DIM // LANES) * x["vc"].astype(F32)).astype(BF16)
        return qkw, inter, m_j, wk.astype(BF16), vw, decay, m_new

    def finish(x, d, qkw, inter, m_j, wk, vw, decay, h_ref):
        base = x["base"]
        den = inter * x["qn"] + jnp.dot(qkw, ones_b, preferred_element_type=F32)
        rden = 1.0 / jnp.maximum(jnp.abs(den), jnp.exp(-m_j))
        num = wide(inter, ML_V_DIM // LANES) * x["qC"] + jnp.dot(qkw, x["vc"], preferred_element_type=F32)
        h_ref[pl.ds(base, L), :] = num * wide(rden, ML_V_DIM // LANES)
        cst[d] = wide(decay, ML_V_DIM // LANES) * x["C"] + jnp.dot(x["kt"], vw, preferred_element_type=F32)
        nst[d] = decay * nst[d] + jnp.dot(x["kt"], wk, preferred_element_type=F32)

    cst[...] = jnp.zeros(cst.shape, F32)
    nst[...] = jnp.zeros(nst.shape, F32)

    def scan_body(t, carry):
        m_f, m_b = carry
        xf = load(t, 0, L - 1)
        xb = load(nc - 1 - t, 1, 0)
        wf = weights(xf, fwd_mask, m_f)
        finish(xf, 0, *wf[:-1], hf)
        wb = weights(xb, bwd_mask, m_b)
        finish(xb, 1, *wb[:-1], hb)
        return wf[-1], wb[-1]

    z_m = jnp.zeros((1, LANES), F32)
    lax.fori_loop(0, nc, scan_body, (z_m, z_m))

    ng = ng_ref[...]

    def fin_body(c, carry):
        base = pl.multiple_of(c * L, L)
        h = hf[pl.ds(base, L), :] + hb[pl.ds(base, L), :]
        ms = jnp.mean(h * h, axis=-1, keepdims=True)
        y = h * lax.rsqrt(ms + EPS) * ng
        y = y * _sigmoid(og_ref[pl.ds(base, L), :].astype(F32))
        out_ref[pl.ds(base, L), :] = y.astype(out_ref.dtype)
        return carry

    lax.fori_loop(0, nc, fin_body, 0)


def _mlstm(proj, gates_rows, bg_rows, conv_w, conv_b, norm_g, B, S):
    T = B * S
    L = min(ML_CHUNK, S)
    assert S % L == 0 and L % LANES == 0
    q0 = 3 * NA_WIDTH // ML_QK_DIM
    k0 = q0 + ML_HEADS
    v0 = (3 * NA_WIDTH + 2 * ML_QK_WIDTH) // ML_V_DIM
    o0 = v0 + ML_HEADS
    kern = functools.partial(_mlstm_kernel, S=S, L=L)
    return pl.pallas_call(
        kern,
        grid=(B, ML_HEADS),
        in_specs=[
            pl.BlockSpec((S, ML_QK_DIM), lambda b, h: (b, q0 + h)),
            pl.BlockSpec((S, ML_QK_DIM), lambda b, h: (b, k0 + h)),
            pl.BlockSpec((S, ML_V_DIM), lambda b, h: (b, v0 + h)),
            pl.BlockSpec((S, ML_V_DIM), lambda b, h: (b, o0 + h)),
            pl.BlockSpec((None, None, 8, S), lambda b, h: (b, h, 0, 0)),
            pl.BlockSpec((None, 8, 1), lambda b, h: (h, 0, 0)),
            pl.BlockSpec((CONV_W, ML_QK_DIM), lambda b, h: (0, h)),
            pl.BlockSpec((CONV_W, ML_QK_DIM), lambda b, h: (0, ML_HEADS + h)),
            pl.BlockSpec((1, ML_QK_DIM), lambda b, h: (0, h)),
            pl.BlockSpec((1, ML_QK_DIM), lambda b, h: (0, ML_HEADS + h)),
            pl.BlockSpec((1, ML_V_DIM), lambda b, h: (0, h)),
        ],
        out_specs=pl.BlockSpec((S, ML_V_DIM), lambda b, h: (b, h)),
        out_shape=jax.ShapeDtypeStruct((T, ML_V_WIDTH), BF16),
        scratch_shapes=[
            pltpu.VMEM((S, ML_QK_DIM), BF16),
            pltpu.VMEM((ML_QK_DIM, S), BF16),
            pltpu.VMEM((8, S), F32),
            pltpu.VMEM((6, S, LANES), F32),
            pltpu.VMEM((S, ML_V_DIM), F32),
            pltpu.VMEM((S, ML_V_DIM), F32),
            pltpu.VMEM((2, ML_QK_DIM, ML_V_DIM), F32),
            pltpu.VMEM((2, ML_QK_DIM, LANES), F32),
        ],
        compiler_params=_cparams(("parallel", "arbitrary")),
        name="mlstm",
    )(proj, proj, proj, proj, gates_rows, bg_rows, conv_w, conv_w, conv_b, conv_b, norm_g)


def _out_proj_kernel(na_ref, ml_ref, x_ref, w_ref, nag_ref, fg_ref, wr_ref, xo_ref, h_ref, aff_ref):
    na = na_ref[...].astype(F32)
    ms = jnp.mean(na * na, axis=-1, keepdims=True)
    na_n = (na * lax.rsqrt(ms + EPS) * nag_ref[...]).astype(BF16)
    upd = jnp.dot(na_n, w_ref[0:NA_WIDTH, :], preferred_element_type=F32)
    upd = upd + jnp.dot(ml_ref[...], w_ref[NA_WIDTH:, :], preferred_element_type=F32)
    xn = x_ref[...] + upd
    xo_ref[...] = xn
    ms2 = jnp.mean(xn * xn, axis=-1, keepdims=True)
    h = xn * lax.rsqrt(ms2 + EPS) * fg_ref[...]
    h_ref[...] = h
    logits = jnp.dot(h.astype(BF16), wr_ref[...], preferred_element_type=F32)
    lane = lax.broadcasted_iota(jnp.int32, logits.shape, 1)
    logits = jnp.where(lane < N_EXPERTS, logits, NEG)
    m = jnp.max(logits, axis=-1, keepdims=True)
    e = jnp.exp(logits - m)
    aff_ref[...] = e / jnp.sum(e, axis=-1, keepdims=True)


def _out_proj(na_o, ml_o, x2d, w_out, na_g, ffn_g, w_router, tm=512):
    T, D = x2d.shape
    return pl.pallas_call(
        _out_proj_kernel,
        grid=(T // tm,),
        in_specs=[
            pl.BlockSpec((tm, NA_WIDTH), lambda i: (i, 0)),
            pl.BlockSpec((tm, ML_V_WIDTH), lambda i: (i, 0)),
            pl.BlockSpec((tm, D), lambda i: (i, 0)),
            pl.BlockSpec((NA_WIDTH + ML_V_WIDTH, D), lambda i: (0, 0)),
            pl.BlockSpec((1, NA_WIDTH), lambda i: (0, 0)),
            pl.BlockSpec((1, D), lambda i: (0, 0)),
            pl.BlockSpec((D, LANES), lambda i: (0, 0)),
        ],
        out_specs=[
            pl.BlockSpec((tm, D), lambda i: (i, 0)),
            pl.BlockSpec((tm, D), lambda i: (i, 0)),
            pl.BlockSpec((tm, LANES), lambda i: (i, 0)),
        ],
        out_shape=[
            jax.ShapeDtypeStruct((T, D), F32),
            jax.ShapeDtypeStruct((T, D), F32),
            jax.ShapeDtypeStruct((T, LANES), F32),
        ],
        compiler_params=_cparams(("parallel",)),
        name="out_proj",
    )(na_o, ml_o, x2d, w_out, na_g, ffn_g, w_router)


I32 = jnp.int32
SLOT_LO = 32
TOK_LO = 64
CHUNK = 256


def _route_kernel(aff_ref, tok_ref, dst_ref, gate_ref, off_ref, cnt_ref, a_scr, sel_scr, pos_scr, tmp_scr,
                  *, S, cap):
    b = pl.program_id(0)
    E = N_EXPERTS
    nchunk = S // CHUNK
    zrows = E * cap
    n_hi = cap // SLOT_LO

    TB = 512
    for c in range(S // TB):
        a_scr[:, c * TB:(c + 1) * TB] = aff_ref[c * TB:(c + 1) * TB, :].T[0:E, :]
    A = a_scr[...]

    def thr_body(i, thr):
        cand = thr | jnp.left_shift(jnp.int32(1), 30 - i)
        cand_f = lax.bitcast_convert_type(cand, F32)
        cnt = jnp.sum((a_scr[...] >= cand_f).astype(F32), axis=1, keepdims=True)
        return jnp.where(cnt >= cap, cand, thr)

    thr = lax.fori_loop(0, 31, thr_body, jnp.zeros((E, 1), I32))
    thr_f = lax.bitcast_convert_type(thr, F32)
    gt = A > thr_f
    eq = A == thr_f
    need = cap - jnp.sum(gt.astype(F32), axis=1, keepdims=True)
    idx = lax.broadcasted_iota(I32, (E, S), 1)

    nbits = S.bit_length() - 1

    def tie_body(i, jt):
        cand = jt | jnp.left_shift(jnp.int32(1), nbits - 1 - i)
        c = jnp.sum((eq & (idx < cand)).astype(F32), axis=1, keepdims=True)
        return jnp.where(c < need, cand, jt)

    jt = lax.fori_loop(0, nbits, tie_body, jnp.zeros((E, 1), I32))
    sel = (gt | (eq & (idx <= jt))).astype(F32)
    sel_scr[...] = sel

    r_i = lax.broadcasted_iota(I32, (CHUNK, CHUNK), 0)
    c_i = lax.broadcasted_iota(I32, (CHUNK, CHUNK), 1)
    strict_upper = (r_i < c_i).astype(BF16)
    e_r = lax.broadcasted_iota(I32, (E, E), 0)
    e_c = lax.broadcasted_iota(I32, (E, E), 1)
    strict_lower = (e_c < e_r).astype(BF16)
    run = jnp.zeros((E, 1), F32)
    run_t = jnp.zeros((1, 1), F32)
    for c in range(nchunk):
        sc = sel[:, c * CHUNK:(c + 1) * CHUNK]
        scb = sc.astype(BF16)
        pos_scr[:, c * CHUNK:(c + 1) * CHUNK] = jnp.dot(scb, strict_upper, preferred_element_type=F32) + run
        run = run + jnp.sum(sc, axis=1, keepdims=True)
        cnt_c = jnp.sum(sc, axis=0, keepdims=True)
        cnt8 = jnp.broadcast_to(cnt_c, (8, CHUNK)).astype(BF16)
        base_c = jnp.dot(cnt8, strict_upper, preferred_element_type=F32)[0:1] + run_t
        run_t = run_t + jnp.sum(cnt_c, axis=1, keepdims=True)
        rank_c = jnp.dot(strict_lower, scb, preferred_element_type=F32)
        zoff = (b * zrows).astype(F32)
        tmp_scr[:, c * CHUNK:(c + 1) * CHUNK] = base_c + rank_c + zoff
        off_ref[:, c * CHUNK:(c + 1) * CHUNK] = (base_c + zoff).astype(I32)
        cnt_ref[:, c * CHUNK:(c + 1) * CHUNK] = cnt_c.astype(I32)

    lo_id = lax.broadcasted_iota(I32, (SLOT_LO, S), 0).astype(F32)
    hi_id = lax.broadcasted_iota(I32, (n_hi, S), 0).astype(F32)
    t_row = lax.broadcasted_iota(I32, (1, S), 1)
    t_hi = (t_row // TOK_LO).astype(F32)
    t_lo = (t_row % TOK_LO).astype(F32)

    def split3(x):
        hi = x.astype(BF16).astype(F32)
        mid = (x - hi).astype(BF16).astype(F32)
        lo = (x - hi - mid).astype(BF16).astype(F32)
        return hi, mid, lo

    for e in range(E):
        sel_e = sel_scr[e:e + 1, :]
        pos_e = pos_scr[e:e + 1, :]
        p_hi = jnp.floor(pos_e * (1.0 / SLOT_LO))
        p_lo = pos_e - p_hi * SLOT_LO
        oh_lo = jnp.where((lo_id == p_lo) & (sel_e > 0), 1.0, 0.0).astype(BF16)
        oh_hi = jnp.where(hi_id == p_hi, 1.0, 0.0)
        d = tmp_scr[e:e + 1, :]
        d2 = jnp.floor(d * (1.0 / 1024.0))
        d1 = jnp.floor((d - d2 * 1024.0) * (1.0 / 32.0))
        d0 = d - d2 * 1024.0 - d1 * 32.0
        g_hi, g_mid, g_lo = split3(a_scr[e:e + 1, :])
        vals = (t_hi, t_lo, d2, d1, d0, g_hi, g_mid, g_lo)
        stack = jnp.concatenate([oh_hi * v for v in vals], axis=0).astype(BF16)
        r = lax.dot_general(stack, oh_lo, (((1,), (1,)), ((), ())), preferred_element_type=F32)
        rr = [r[k * n_hi:(k + 1) * n_hi] for k in range(8)]
        tok_ref[e] = (rr[0] * TOK_LO + rr[1]).astype(I32) + b * S
        dst_ref[e] = (rr[2] * 1024.0 + rr[3] * 32.0 + rr[4]).astype(I32)
        gate_ref[e] = rr[5] + rr[6] + rr[7]


def _route(aff, B, S):
    cap = CAPACITY_FACTOR * S // N_EXPERTS
    n_hi = cap // SLOT_LO
    E = N_EXPERTS
    kern = functools.partial(_route_kernel, S=S, cap=cap)
    lists = jax.ShapeDtypeStruct((B, E, n_hi, SLOT_LO), I32)
    return pl.pallas_call(
        kern,
        grid=(B,),
        in_specs=[pl.BlockSpec((S, LANES), lambda b: (b, 0))],
        out_specs=[
            pl.BlockSpec((None, E, n_hi, SLOT_LO), lambda b: (b, 0, 0, 0)),
            pl.BlockSpec((None, E, n_hi, SLOT_LO), lambda b: (b, 0, 0, 0)),
            pl.BlockSpec((None, E, n_hi, SLOT_LO), lambda b: (b, 0, 0, 0)),
            pl.BlockSpec((None, 1, S), lambda b: (b, 0, 0)),
            pl.BlockSpec((None, 1, S), lambda b: (b, 0, 0)),
        ],
        out_shape=[lists, lists, jax.ShapeDtypeStruct((B, E, n_hi, SLOT_LO), F32),
                   jax.ShapeDtypeStruct((B, 1, S), I32), jax.ShapeDtypeStruct((B, 1, S), I32)],
        scratch_shapes=[pltpu.VMEM((E, S), F32)] * 4,
        compiler_params=_cparams(("parallel",)),
        name="route",
    )(aff)


FFN_ROWS = 128
FFN_STEPS = 4


def _ffn_kernel(tok_ref, dst_ref, h_hbm, wg_ref, wu_ref, wd_ref, gate_ref, z_hbm, xg, xb, act, yacc, sem_g, sem_s,
                *, tm):
    s = pl.program_id(2)
    q = pl.program_id(0) * pl.num_programs(1) + pl.program_id(1)
    nq = pl.num_programs(0) * pl.num_programs(1)
    tn = yacc.shape[1] // FFN_STEPS

    def gather_row(r):
        return pltpu.make_async_copy(h_hbm.at[pl.ds(tok_ref[0, r], 1), :], xg.at[pl.ds(r, 1), :], sem_g)

    def scatter_row(r):
        return pltpu.make_async_copy(yacc.at[pl.ds(r, 1), :], z_hbm.at[pl.ds(dst_ref[0, r], 1), :], sem_s)

    def for_rows(fn):
        def body(r, c):
            fn(r)
            return c
        lax.fori_loop(0, tm, body, 0, unroll=8)

    @pl.when(s == 0)
    def _():
        for_rows(lambda r: gather_row(r).start())
        for_rows(lambda r: gather_row(r).wait())

        def cast(j, c):
            rows = pl.ds(pl.multiple_of(j * FFN_ROWS, FFN_ROWS), FFN_ROWS)
            xb[rows, :] = xg[rows, :].astype(BF16)
            return c
        lax.fori_loop(0, tm // FFN_ROWS, cast, 0)

    @pl.when(s < FFN_STEPS)
    def _():
        x = xb[...]
        a = jnp.dot(x, wg_ref[...].astype(BF16), preferred_element_type=F32)
        u = jnp.dot(x, wu_ref[...].astype(BF16), preferred_element_type=F32)
        act[s] = (a * _sigmoid(a) * u).astype(BF16)

    @pl.when((s == FFN_STEPS) & (q > 0))
    def _():
        for_rows(lambda r: scatter_row(r).wait())

    @pl.when(s >= FFN_STEPS)
    def _():
        a_all = jnp.concatenate([act[f] for f in range(FFN_STEPS)], axis=1)
        y = jnp.dot(a_all, wd_ref[...].astype(BF16), preferred_element_type=F32)
        col = pl.multiple_of((s - FFN_STEPS) * tn, tn)
        yacc[:, pl.ds(col, tn)] = y * gate_ref[...]

    @pl.when(s == 2 * FFN_STEPS - 1)
    def _():
        for_rows(lambda r: scatter_row(r).start())

    @pl.when((s == 2 * FFN_STEPS - 1) & (q == nq - 1))
    def _():
        for_rows(lambda r: scatter_row(r).wait())


def _expert_ffn(h2, tok, dst, gate, w_gate, w_up, w_down, layer, tm=1024):
    E, M = tok.shape
    T, D = h2.shape
    F = w_gate.shape[-1]
    tm = min(tm, M)
    nm = M // tm
    tf = F // FFN_STEPS
    tn = D // FFN_STEPS
    last = FFN_STEPS - 1
    tok3 = tok.reshape(E * nm, 1, tm)
    dst3 = dst.reshape(E * nm, 1, tm)
    smem_spec = pl.BlockSpec((None, 1, tm), lambda e, m, s: (e * nm + m, 0, 0), memory_space=pltpu.SMEM)
    kern = functools.partial(_ffn_kernel, tm=tm)
    return pl.pallas_call(
        kern,
        grid=(E, nm, 2 * FFN_STEPS),
        in_specs=[
            smem_spec,
            smem_spec,
            pl.BlockSpec(memory_space=pl.ANY),
            pl.BlockSpec((None, None, D, tf), lambda e, m, s: (layer, e, 0, jnp.minimum(s, last))),
            pl.BlockSpec((None, None, D, tf), lambda e, m, s: (layer, e, 0, jnp.minimum(s, last))),
            pl.BlockSpec((None, None, F, tn), lambda e, m, s: (layer, e, 0, jnp.maximum(s - FFN_STEPS, 0))),
            pl.BlockSpec((None, tm, 1), lambda e, m, s: (e, m, 0)),
        ],
        out_specs=pl.BlockSpec(memory_space=pl.ANY),
        out_shape=jax.ShapeDtypeStruct((E * M, D), F32),
        scratch_shapes=[
            pltpu.VMEM((tm, D), F32),
            pltpu.VMEM((tm, D), BF16),
            pltpu.VMEM((FFN_STEPS, tm, tf), BF16),
            pltpu.VMEM((tm, D), F32),
            pltpu.SemaphoreType.DMA(()),
            pltpu.SemaphoreType.DMA(()),
        ],
        compiler_params=_cparams(("arbitrary", "arbitrary", "arbitrary")),
        name="expert_ffn",
    )(tok3, dst3, h2, w_gate, w_up, w_down, gate)


COMB_TOK = 512
COMB_ROWS = 256
COMB_SLOTS = 6
COMB_AHEAD = 4


def _combine_kernel(start_ref, nch_ref, x_ref, off_ref, cnt_ref, g_ref, z_hbm, o_ref, zbuf, sem, st,
                    *, zrows, final):
    i = pl.program_id(0)
    nt = pl.num_programs(0)
    start = start_ref[i]
    n = nch_ref[i]
    i_next = jnp.minimum(i + 1, nt - 1)
    n_next = jnp.where(i + 1 < nt, nch_ref[i_next], 0)

    @pl.when(i == 0)
    def _():
        st[0] = 0
        st[1] = 0

    gc = st[0]
    pre = st[1]
    o_ref[...] = x_ref[...]
    off = off_ref[...]
    end = off + cnt_ref[...]

    def chunk_start(tile, k):
        return pl.multiple_of(jnp.minimum(start_ref[tile] + k * COMB_ROWS, zrows - COMB_ROWS), 8)

    def copy(tile, k, number):
        slot = lax.rem(number, COMB_SLOTS)
        return pltpu.make_async_copy(z_hbm.at[pl.ds(chunk_start(tile, k), COMB_ROWS), :], zbuf.at[slot],
                                     sem.at[slot])

    for j in range(COMB_AHEAD):
        @pl.when((j >= pre) & (j < n))
        def _():
            copy(i, j, gc + j).start()

    cross = n >= COMB_AHEAD

    def body(k, c):
        copy(i, k, gc + k).wait()
        ahead = k + COMB_AHEAD

        @pl.when(ahead < n)
        def _():
            copy(i, ahead, gc + ahead).start()

        @pl.when((ahead >= n) & cross & (ahead - n < n_next))
        def _():
            copy(i_next, ahead - n, gc + ahead).start()

        rows = chunk_start(i, k) + lax.broadcasted_iota(I32, (1, COMB_ROWS), 1)
        lo = jnp.maximum(off, start + k * COMB_ROWS)
        seg = jnp.where((rows >= lo) & (rows < end), 1.0, 0.0).astype(BF16)
        o_ref[...] += jnp.dot(seg, zbuf[lax.rem(gc + k, COMB_SLOTS)].astype(BF16), preferred_element_type=F32)
        return c

    lax.fori_loop(0, n, body, 0)
    st[0] = gc + n
    st[1] = jnp.where(cross, jnp.minimum(COMB_AHEAD, n_next), 0)

    if final:
        x = o_ref[...]
        ms = jnp.mean(x * x, axis=-1, keepdims=True)
        o_ref[...] = x * lax.rsqrt(ms + EPS) * g_ref[...]


def _combine(x2d, z, off, cnt, gain, final):
    T, D = x2d.shape
    zrows = z.shape[0]
    tt = min(COMB_TOK, T)
    nt = T // tt
    start = (off[::tt] // 8) * 8
    last = (off + cnt)[tt - 1::tt]
    nch = (last - start + COMB_ROWS - 1) // COMB_ROWS
    kern = functools.partial(_combine_kernel, zrows=zrows, final=final)
    grid_spec = pltpu.PrefetchScalarGridSpec(
        num_scalar_prefetch=2,
        grid=(nt,),
        in_specs=[
            pl.BlockSpec((tt, D), lambda i, s, n: (i, 0)),
            pl.BlockSpec((tt, 1), lambda i, s, n: (i, 0)),
            pl.BlockSpec((tt, 1), lambda i, s, n: (i, 0)),
            pl.BlockSpec((1, D), lambda i, s, n: (0, 0)),
            pl.BlockSpec(memory_space=pl.ANY),
        ],
        out_specs=pl.BlockSpec((tt, D), lambda i, s, n: (i, 0)),
        scratch_shapes=[pltpu.VMEM((COMB_SLOTS, COMB_ROWS, D), F32), pltpu.SemaphoreType.DMA((COMB_SLOTS,)),
                        pltpu.SMEM((2,), I32)],
    )
    return pl.pallas_call(
        kern,
        grid_spec=grid_spec,
        out_shape=jax.ShapeDtypeStruct((T, D), F32),
        compiler_params=_cparams(("arbitrary",)),
        name="combine",
    )(start.astype(I32), nch.astype(I32), x2d, off[:, None], cnt[:, None], gain, z)


def moe(x2d, h2, aff, w_gate, w_up, w_down, layer, B, S, gain, final):
    E = N_EXPERTS
    cap = CAPACITY_FACTOR * S // E
    tok, dst, gate, off, cnt = _route(aff, B, S)
    to_em = lambda a: jnp.swapaxes(a.reshape(B, E, cap), 0, 1).reshape(E, B * cap)
    z = _expert_ffn(h2, to_em(tok), to_em(dst), to_em(gate)[:, :, None], w_gate, w_up, w_down, layer)
    return _combine(x2d, z, off.reshape(B * S), cnt.reshape(B * S), gain, final)


def kernel(x, norm_mix_g, w_in, b_gates, conv_w, conv_b, na_rpb, na_norm_g, ml_norm_g, w_out,
           norm_ffn_g, w_router, w_gate, w_up, w_down, final_norm_g):
    B, S, D = x.shape
    T = B * S
    depth = w_in.shape[0]
    x2d = x.reshape(T, D)
    w_t = jnp.swapaxes(w_in, 1, 2)
    w_main_t = _cast_main_weights(w_t)

    for l in range(depth):
        proj, gates = _in_proj(x2d, norm_mix_g[l][None, :], w_main_t, w_t, l)

        tb = _na_bias_table(na_rpb[l])
        na_o = _na_attention(proj, tb, B, S)

        g4 = gates.reshape(B, S, 4, ML_HEADS)
        g_rows = jnp.pad(jnp.transpose(g4, (0, 3, 2, 1)), ((0, 0), (0, 0), (0, 4), (0, 0)))
        bg_rows = jnp.pad(b_gates[l].reshape(4, ML_HEADS).T, ((0, 0), (0, 4)))[:, :, None]
        ml_o = _mlstm(proj, g_rows, bg_rows.astype(F32), conv_w[l], conv_b[l][None, :],
                      ml_norm_g[l][None, :], B, S)

        w_r = jnp.pad(w_router[l], ((0, 0), (0, LANES - N_EXPERTS))).astype(BF16)
        x2d, h2, aff = _out_proj(na_o, ml_o, x2d, w_out[l].astype(BF16), na_norm_g[l][None, :],
                                 norm_ffn_g[l][None, :], w_r)

        x2d = moe(x2d, h2, aff, w_gate, w_up, w_down, l, B, S, final_norm_g[None, :], l == depth - 1)

    return x2d.reshape(B, S, D)
```

```python
import functools

import numpy as np
import jax
import jax.numpy as jnp
from jax import lax
from jax.experimental import pallas as pl
from jax.experimental.pallas import tpu as pltpu

F32 = jnp.float32
BF16 = jnp.bfloat16

EPS = 1e-6
GRID_W = 64
NA_HEADS = 8
NA_HEAD_DIM = 128
NA_KR = 8
NA_KC = 16
NA_WIDTH = NA_HEADS * NA_HEAD_DIM
NA_LOOKAHEAD = 6
ML_HEADS = 4
ML_QK_DIM = 128
ML_V_DIM = 256
ML_QK_WIDTH = ML_HEADS * ML_QK_DIM
ML_V_WIDTH = ML_HEADS * ML_V_DIM
CONV_W = 5
N_GATES = 4 * ML_HEADS
N_EXPERTS = 16
CAPACITY_FACTOR = 2
N_MAIN = 3 * NA_WIDTH + 2 * ML_QK_WIDTH + 2 * ML_V_WIDTH
LANES = 128
NEG = -1e30
ML_CHUNK = 256
VMEM_LIMIT = 56 * 1024 * 1024


def _cparams(sem):
    return pltpu.CompilerParams(dimension_semantics=sem, vmem_limit_bytes=VMEM_LIMIT)


_NT = (((1,), (1,)), ((), ()))


def _in_proj_kernel(x_ref, g_ref, w_ref, wg_ref, o_ref, og_ref, h_scr):
    @pl.when(pl.program_id(1) == 0)
    def _():
        x = x_ref[...]
        ms = jnp.mean(x * x, axis=-1, keepdims=True)
        h = (x * lax.rsqrt(ms + EPS) * g_ref[...]).astype(BF16)
        h_scr[...] = h
        og_ref[...] = lax.dot_general(h, wg_ref[...].astype(BF16), _NT, preferred_element_type=F32)

    o_ref[...] = lax.dot_general(h_scr[...], w_ref[...], _NT, preferred_element_type=F32).astype(o_ref.dtype)


def _cast_kernel(w_ref, o_ref):
    o_ref[...] = w_ref[...].astype(o_ref.dtype)


def _cast_main_weights(w_t, tn=1024):
    depth, _, D = w_t.shape
    return pl.pallas_call(
        _cast_kernel,
        grid=(depth, N_MAIN // tn),
        in_specs=[pl.BlockSpec((None, tn, D), lambda l, j: (l, j, 0))],
        out_specs=pl.BlockSpec((None, tn, D), lambda l, j: (l, j, 0)),
        out_shape=jax.ShapeDtypeStruct((depth, N_MAIN, D), BF16),
        compiler_params=_cparams(("parallel", "parallel")),
        name="cast_weights",
    )(w_t)


def _in_proj(x2d, g, w_main_t, w_t, layer, tm=1024, tn=2048):
    T, D = x2d.shape
    N = N_MAIN
    return pl.pallas_call(
        _in_proj_kernel,
        grid=(T // tm, N // tn),
        in_specs=[
            pl.BlockSpec((tm, D), lambda i, j: (i, 0)),
            pl.BlockSpec((1, D), lambda i, j: (0, 0)),
            pl.BlockSpec((None, tn, D), lambda i, j: (layer, j, 0)),
            pl.BlockSpec((None, N_GATES, D), lambda i, j: (layer, N_MAIN // N_GATES, 0)),
        ],
        out_specs=[
            pl.BlockSpec((tm, tn), lambda i, j: (i, j)),
            pl.BlockSpec((tm, N_GATES), lambda i, j: (i, 0)),
        ],
        out_shape=[
            jax.ShapeDtypeStruct((T, N), BF16),
            jax.ShapeDtypeStruct((T, N_GATES), F32),
        ],
        scratch_shapes=[pltpu.VMEM((tm, D), BF16)],
        compiler_params=_cparams(("parallel", "arbitrary")),
        name="in_proj",
    )(x2d, g, w_main_t, w_t)


def _na_bias_table(rpb):
    n_dc = 2 * NA_KC - 1
    off = np.arange(NA_KR)[:, None]
    a = np.arange(NA_KR)[None, :]
    dr = a - off + (NA_KR - 1)
    c = np.arange(GRID_W)
    col_start = np.clip(c - NA_KC // 2, 0, GRID_W - NA_KC)
    col_in = (c[None, :] >= col_start[:, None]) & (c[None, :] < col_start[:, None] + NA_KC)
    dc = np.clip(c[None, :] - c[:, None] + (NA_KC - 1), 0, n_dc - 1)
    sel_c = jnp.asarray(dc[:, :, None] == np.arange(n_dc), F32)
    x = jnp.einsum("hij,qkj->hiqk", rpb.astype(F32), sel_c, precision=lax.Precision.HIGHEST)
    tb = jnp.transpose(x[:, dr], (0, 1, 3, 2, 4))
    tb = tb.reshape(rpb.shape[0], NA_KR, GRID_W, NA_KR * GRID_W)
    keep = np.tile(col_in, (1, NA_KR))
    return jnp.where(keep[None, None], tb, NEG)


def _na_kernel(q_ref, k_ref, v_ref, tb_ref, o_ref, *, rb, rows):
    i = pl.program_id(2)
    scale = NA_HEAD_DIM ** -0.5
    win = NA_KR * GRID_W

    def scores(rr):
        r = i * rb + rr
        rs = jnp.clip(r - NA_KR // 2, 0, rows - NA_KR)
        start = pl.multiple_of(rs * GRID_W, GRID_W)
        q = q_ref[rr * GRID_W:(rr + 1) * GRID_W, :]
        kw = k_ref[pl.ds(start, win), :]
        s = lax.dot_general(q, kw, (((1,), (1,)), ((), ())), preferred_element_type=F32)
        return s, r - rs, start

    queue = [scores(rr) for rr in range(min(NA_LOOKAHEAD, rb))]
    for rr in range(rb):
        s, off, start = queue.pop(0)
        if rr + NA_LOOKAHEAD < rb:
            queue.append(scores(rr + NA_LOOKAHEAD))
        vw = v_ref[pl.ds(start, win), :]
        s = s * scale + tb_ref[off]
        m = jnp.max(s, axis=-1, keepdims=True)
        p = jnp.exp(s - m)
        l = jnp.sum(p, axis=-1, keepdims=True)
        o = jnp.dot(p.astype(BF16), vw, preferred_element_type=F32)
        o_ref[rr * GRID_W:(rr + 1) * GRID_W, :] = (o / l).astype(o_ref.dtype)


def _na_attention(proj, tb, B, S, rb=64):
    T = B * S
    rows = S // GRID_W
    rb = min(rb, rows)
    assert rows >= NA_KR and rows % rb == 0
    nblk = rows // rb
    tq = rb * GRID_W
    kern = functools.partial(_na_kernel, rb=rb, rows=rows)
    return pl.pallas_call(
        kern,
        grid=(B, NA_HEADS, nblk),
        in_specs=[
            pl.BlockSpec((tq, NA_HEAD_DIM), lambda b, h, i: (b * nblk + i, h)),
            pl.BlockSpec((S, NA_HEAD_DIM), lambda b, h, i: (b, NA_HEADS + h)),
            pl.BlockSpec((S, NA_HEAD_DIM), lambda b, h, i: (b, 2 * NA_HEADS + h)),
            pl.BlockSpec((None, NA_KR, GRID_W, NA_KR * GRID_W), lambda b, h, i: (h, 0, 0, 0)),
        ],
        out_specs=pl.BlockSpec((tq, NA_HEAD_DIM), lambda b, h, i: (b * nblk + i, h)),
        out_shape=jax.ShapeDtypeStruct((T, NA_WIDTH), BF16),
        compiler_params=_cparams(("parallel", "parallel", "arbitrary")),
        name="na_attention",
    )(proj, proj, proj, tb)


def _log_sigmoid(x):
    return jnp.minimum(x, 0.0) - jnp.log(1.0 + jnp.exp(-jnp.abs(x)))


def _sigmoid(x):
    return 1.0 / (1.0 + jnp.exp(-x))


def _mlstm_kernel(q_ref, k_ref, v_ref, og_ref, g_ref, bg_ref, cwq_ref, cwk_ref, cbq_ref, cbk_ref,
                  ng_ref, out_ref, qs, kst, rrow, bcol, hf, hb, cst, nst, *, S, L):
    nc = S // L
    halo = 8

    def conv_silu(src_ref, cw, cbv, c, scale):
        base = pl.multiple_of(c * L, L)
        lo = jnp.maximum(base - halo, 0)
        hi = jnp.minimum(base + L, S - halo)
        prev = src_ref[pl.ds(pl.multiple_of(lo, halo), halo), :].astype(F32)
        nxt = src_ref[pl.ds(pl.multiple_of(hi, halo), halo), :].astype(F32)
        prev = jnp.where(c > 0, prev, 0.0)
        nxt = jnp.where(c < nc - 1, nxt, 0.0)
        main = src_ref[pl.ds(base, L), :].astype(F32)
        blk = jnp.concatenate([prev, main, nxt], axis=0)
        n = L + 2 * halo
        acc = jnp.zeros((L, LANES), F32) + cbv
        for w in range(CONV_W):
            sh = (CONV_W // 2 - w) % n
            xs = blk if sh == 0 else pltpu.roll(blk, sh, 0)
            acc = acc + xs[halo:halo + L, :] * cw[w:w + 1, :]
        return acc * _sigmoid(acc) * scale

    cwq, cbq, cwk, cbk = cwq_ref[...], cbq_ref[...], cwk_ref[...], cbk_ref[...]

    def conv_body(c, carry):
        base = pl.multiple_of(c * L, L)
        qs[pl.ds(base, L), :] = conv_silu(q_ref, cwq, cbq, c, ML_QK_DIM ** -0.5).astype(BF16)
        kst[:, pl.ds(base, L)] = conv_silu(k_ref, cwk, cbk, c, 1.0).T.astype(BF16)
        return carry

    lax.fori_loop(0, nc, conv_body, 0)

    t_idx = lax.broadcasted_iota(jnp.int32, (L, L), 0)
    j_idx = lax.broadcasted_iota(jnp.int32, (L, L), 1)
    upper = (t_idx <= j_idx).astype(BF16)
    lower = (t_idx >= j_idx).astype(BF16)

    def split3(x):
        hi = x.astype(BF16).astype(F32)
        mid = (x - hi).astype(BF16).astype(F32)
        lo = (x - hi - mid).astype(BF16).astype(F32)
        return hi, mid, lo

    def gate_body(c, carry):
        base = pl.multiple_of(c * L, L)
        g = g_ref[:, pl.ds(base, L)] + bg_ref[...]
        hi, mid, lo = split3(_log_sigmoid(g))
        lf3 = jnp.concatenate([hi, mid, lo, jnp.zeros_like(hi)], axis=0).astype(BF16)
        pre = jnp.dot(lf3, upper, preferred_element_type=F32)
        suf = jnp.dot(lf3, lower, preferred_element_type=F32)
        pre = pre[0:8] + pre[8:16] + pre[16:24]
        suf = suf[0:8] + suf[8:16] + suf[16:24]
        cum_f, cum_b = pre[1:2], suf[3:4]
        gx_f = cum_f[:, L - 1:L] - cum_f + g[0:1]
        gx_b = cum_b[:, 0:1] - cum_b + g[2:3]
        r = jnp.concatenate([g[0:1], cum_f, g[2:3], cum_b, gx_f, gx_b, g[0:1] - cum_f, g[2:3] - cum_b],
                            axis=0)
        rrow[:, pl.ds(base, L)] = r
        return carry

    lax.fori_loop(0, nc, gate_body, 0, unroll=4)

    r_all = rrow[...]
    row_all = lax.broadcasted_iota(jnp.int32, (8, S), 0)
    lane_all = lax.broadcasted_iota(jnp.int32, (8, S), 1) & (L - 1)
    pm = r_all
    sm = r_all
    sh = 1
    while sh < L:
        pm = jnp.maximum(pm, jnp.where(lane_all >= sh, pltpu.roll(pm, sh, 1), NEG))
        sm = jnp.maximum(sm, jnp.where(lane_all < L - sh, pltpu.roll(sm, S - sh, 1), NEG))
        sh *= 2
    mi = jnp.where(row_all == 6, pm + pltpu.roll(r_all, 5, 0), sm + pltpu.roll(r_all, 4, 0))
    rrow[...] = jnp.where(row_all >= 6, mi, r_all)

    pick_r = lax.broadcasted_iota(jnp.int32, (LANES, 2 * LANES), 0)
    pick_l = lax.broadcasted_iota(jnp.int32, (LANES, 2 * LANES), 1)

    def picker(k_lo, k_hi):
        want = jnp.where(pick_l < LANES, k_lo, k_hi)
        return ((pick_r < 24) & ((pick_r & 7) == want)).astype(BF16)

    pairs = ((1, 6, 0, 1), (4, 3, 2, 3), (7, 5, 4, 5))

    def bcol_body(c, carry):
        base = pl.multiple_of(c * L, L)
        hi, mid, lo = split3(rrow[:, pl.ds(base, L)])
        x = jnp.concatenate([hi, mid, lo, jnp.zeros((LANES - 24, L), F32)], axis=0)
        xt = x.T.astype(BF16)
        for k_lo, k_hi, d_lo, d_hi in pairs:
            y = jnp.dot(xt, picker(k_lo, k_hi), preferred_element_type=F32)
            bcol[d_lo, pl.ds(base, L), :] = y[:, :LANES]
            bcol[d_hi, pl.ds(base, L), :] = y[:, LANES:]
        return carry

    lax.fori_loop(0, nc, bcol_body, 0, unroll=4)

    fwd_mask = j_idx <= t_idx
    bwd_mask = j_idx >= t_idx
    ones_b = jnp.ones((L, LANES), BF16)

    def wide(x, n):
        return jnp.concatenate([x] * n, axis=1)

    def load(c, d, total_row):
        base = pl.multiple_of(c * L, L)
        qc = qs[pl.ds(base, L), :]
        C = cst[d]
        cum = bcol[3 * d, pl.ds(base, L), :]
        return dict(
            base=base, qc=qc, vc=v_ref[pl.ds(base, L), :], kt=kst[:, pl.ds(base, L)],
            cum=cum, mi=bcol[3 * d + 1, pl.ds(base, L), :], gx=bcol[3 * d + 2, pl.ds(base, L), :],
            i_row=rrow[2 * d:2 * d + 1, pl.ds(base, L)],
            cum_row=rrow[2 * d + 1:2 * d + 2, pl.ds(base, L)],
            total=cum[total_row:total_row + 1, :], C=C,
            s=jnp.dot(qc, kst[:, pl.ds(base, L)], preferred_element_type=F32),
            qC=jnp.dot(qc, C.astype(BF16), preferred_element_type=F32),
            qn=jnp.dot(qc, nst[d].astype(BF16), preferred_element_type=F32))

    def weights(x, mask, m_st):
        m_inter = x["cum"] + m_st
        m_j = jnp.maximum(m_inter, x["mi"])
        dmat = jnp.where(mask, wide(x["cum"], L // LANES) - x["cum_row"] + x["i_row"], NEG)
        qkw = (x["s"] * jnp.exp(dmat - wide(m_j, L // LANES))).astype(BF16)
        inter = jnp.exp(m_inter - m_j)
        m_new = jnp.maximum(x["total"] + m_st, jnp.max(x["gx"], axis=0, keepdims=True))
        wk = jnp.exp(x["gx"] - m_new)
        decay = jnp.exp(x["total"] + m_st - m_new)
        vw = (wide(wk, ML_V_DIM // LANES) * x["vc"].astype(F32)).astype(BF16)
        return qkw, inter, m_j, wk.astype(BF16), vw, decay, m_new

    def finish(x, d, qkw, inter, m_j, wk, vw, decay, h_ref):
        base = x["base"]
        den = inter * x["qn"] + jnp.dot(qkw, ones_b, preferred_element_type=F32)
        rden = 1.0 / jnp.maximum(jnp.abs(den), jnp.exp(-m_j))
        num = wide(inter, ML_V_DIM // LANES) * x["qC"] + jnp.dot(qkw, x["vc"], preferred_element_type=F32)
        h_ref[pl.ds(base, L), :] = num * wide(rden, ML_V_DIM // LANES)
        cst[d] = wide(decay, ML_V_DIM // LANES) * x["C"] + jnp.dot(x["kt"], vw, preferred_element_type=F32)
        nst[d] = decay * nst[d] + jnp.dot(x["kt"], wk, preferred_element_type=F32)

    cst[...] = jnp.zeros(cst.shape, F32)
    nst[...] = jnp.zeros(nst.shape, F32)

    def scan_body(t, carry):
        m_f, m_b = carry
        xf = load(t, 0, L - 1)
        xb = load(nc - 1 - t, 1, 0)
        wf = weights(xf, fwd_mask, m_f)
        finish(xf, 0, *wf[:-1], hf)
        wb = weights(xb, bwd_mask, m_b)
        finish(xb, 1, *wb[:-1], hb)
        return wf[-1], wb[-1]

    z_m = jnp.zeros((1, LANES), F32)
    lax.fori_loop(0, nc, scan_body, (z_m, z_m))

    ng = ng_ref[...]

    def fin_body(c, carry):
        base = pl.multiple_of(c * L, L)
        h = hf[pl.ds(base, L), :] + hb[pl.ds(base, L), :]
        ms = jnp.mean(h * h, axis=-1, keepdims=True)
        y = h * lax.rsqrt(ms + EPS) * ng
        y = y * _sigmoid(og_ref[pl.ds(base, L), :].astype(F32))
        out_ref[pl.ds(base, L), :] = y.astype(out_ref.dtype)
        return carry

    lax.fori_loop(0, nc, fin_body, 0)


def _mlstm(proj, gates_rows, bg_rows, conv_w, conv_b, norm_g, B, S):
    T = B * S
    L = min(ML_CHUNK, S)
    assert S % L == 0 and L % LANES == 0
    q0 = 3 * NA_WIDTH // ML_QK_DIM
    k0 = q0 + ML_HEADS
    v0 = (3 * NA_WIDTH + 2 * ML_QK_WIDTH) // ML_V_DIM
    o0 = v0 + ML_HEADS
    kern = functools.partial(_mlstm_kernel, S=S, L=L)
    return pl.pallas_call(
        kern,
        grid=(B, ML_HEADS),
        in_specs=[
            pl.BlockSpec((S, ML_QK_DIM), lambda b, h: (b, q0 + h)),
            pl.BlockSpec((S, ML_QK_DIM), lambda b, h: (b, k0 + h)),
            pl.BlockSpec((S, ML_V_DIM), lambda b, h: (b, v0 + h)),
            pl.BlockSpec((S, ML_V_DIM), lambda b, h: (b, o0 + h)),
            pl.BlockSpec((None, None, 8, S), lambda b, h: (b, h, 0, 0)),
            pl.BlockSpec((None, 8, 1), lambda b, h: (h, 0, 0)),
            pl.BlockSpec((CONV_W, ML_QK_DIM), lambda b, h: (0, h)),
            pl.BlockSpec((CONV_W, ML_QK_DIM), lambda b, h: (0, ML_HEADS + h)),
            pl.BlockSpec((1, ML_QK_DIM), lambda b, h: (0, h)),
            pl.BlockSpec((1, ML_QK_DIM), lambda b, h: (0, ML_HEADS + h)),
            pl.BlockSpec((1, ML_V_DIM), lambda b, h: (0, h)),
        ],
        out_specs=pl.BlockSpec((S, ML_V_DIM), lambda b, h: (b, h)),
        out_shape=jax.ShapeDtypeStruct((T, ML_V_WIDTH), BF16),
        scratch_shapes=[
            pltpu.VMEM((S, ML_QK_DIM), BF16),
            pltpu.VMEM((ML_QK_DIM, S), BF16),
            pltpu.VMEM((8, S), F32),
            pltpu.VMEM((6, S, LANES), F32),
            pltpu.VMEM((S, ML_V_DIM), F32),
            pltpu.VMEM((S, ML_V_DIM), F32),
            pltpu.VMEM((2, ML_QK_DIM, ML_V_DIM), F32),
            pltpu.VMEM((2, ML_QK_DIM, LANES), F32),
        ],
        compiler_params=_cparams(("parallel", "arbitrary")),
        name="mlstm",
    )(proj, proj, proj, proj, gates_rows, bg_rows, conv_w, conv_w, conv_b, conv_b, norm_g)


def _out_proj_kernel(na_ref, ml_ref, x_ref, w_ref, nag_ref, fg_ref, wr_ref, xo_ref, h_ref, aff_ref):
    na = na_ref[...].astype(F32)
    ms = jnp.mean(na * na, axis=-1, keepdims=True)
    na_n = (na * lax.rsqrt(ms + EPS) * nag_ref[...]).astype(BF16)
    upd = jnp.dot(na_n, w_ref[0:NA_WIDTH, :], preferred_element_type=F32)
    upd = upd + jnp.dot(ml_ref[...], w_ref[NA_WIDTH:, :], preferred_element_type=F32)
    xn = x_ref[...] + upd
    xo_ref[...] = xn
    ms2 = jnp.mean(xn * xn, axis=-1, keepdims=True)
    h = xn * lax.rsqrt(ms2 + EPS) * fg_ref[...]
    h_ref[...] = h
    logits = jnp.dot(h.astype(BF16), wr_ref[...], preferred_element_type=F32)
    lane = lax.broadcasted_iota(jnp.int32, logits.shape, 1)
    logits = jnp.where(lane < N_EXPERTS, logits, NEG)
    m = jnp.max(logits, axis=-1, keepdims=True)
    e = jnp.exp(logits - m)
    aff_ref[...] = e / jnp.sum(e, axis=-1, keepdims=True)


def _out_proj(na_o, ml_o, x2d, w_out, na_g, ffn_g, w_router, tm=512):
    T, D = x2d.shape
    return pl.pallas_call(
        _out_proj_kernel,
        grid=(T // tm,),
        in_specs=[
            pl.BlockSpec((tm, NA_WIDTH), lambda i: (i, 0)),
            pl.BlockSpec((tm, ML_V_WIDTH), lambda i: (i, 0)),
            pl.BlockSpec((tm, D), lambda i: (i, 0)),
            pl.BlockSpec((NA_WIDTH + ML_V_WIDTH, D), lambda i: (0, 0)),
            pl.BlockSpec((1, NA_WIDTH), lambda i: (0, 0)),
            pl.BlockSpec((1, D), lambda i: (0, 0)),
            pl.BlockSpec((D, LANES), lambda i: (0, 0)),
        ],
        out_specs=[
            pl.BlockSpec((tm, D), lambda i: (i, 0)),
            pl.BlockSpec((tm, D), lambda i: (i, 0)),
            pl.BlockSpec((tm, LANES), lambda i: (i, 0)),
        ],
        out_shape=[
            jax.ShapeDtypeStruct((T, D), F32),
            jax.ShapeDtypeStruct((T, D), F32),
            jax.ShapeDtypeStruct((T, LANES), F32),
        ],
        compiler_params=_cparams(("parallel",)),
        name="out_proj",
    )(na_o, ml_o, x2d, w_out, na_g, ffn_g, w_router)


I32 = jnp.int32
SLOT_LO = 32
TOK_LO = 64
CHUNK = 256


def _route_kernel(aff_ref, tok_ref, dst_ref, gate_ref, off_ref, cnt_ref, a_scr, sel_scr, pos_scr, tmp_scr,
                  *, S, cap):
    b = pl.program_id(0)
    E = N_EXPERTS
    nchunk = S // CHUNK
    zrows = E * cap
    n_hi = cap // SLOT_LO

    TB = 512
    for c in range(S // TB):
        a_scr[:, c * TB:(c + 1) * TB] = aff_ref[c * TB:(c + 1) * TB, :].T[0:E, :]
    A = a_scr[...]

    def thr_body(i, thr):
        cand = thr | jnp.left_shift(jnp.int32(1), 30 - i)
        cand_f = lax.bitcast_convert_type(cand, F32)
        cnt = jnp.sum((a_scr[...] >= cand_f).astype(F32), axis=1, keepdims=True)
        return jnp.where(cnt >= cap, cand, thr)

    thr = lax.fori_loop(0, 31, thr_body, jnp.zeros((E, 1), I32))
    thr_f = lax.bitcast_convert_type(thr, F32)
    gt = A > thr_f
    eq = A == thr_f
    need = cap - jnp.sum(gt.astype(F32), axis=1, keepdims=True)
    idx = lax.broadcasted_iota(I32, (E, S), 1)

    nbits = S.bit_length() - 1

    def tie_body(i, jt):
        cand = jt | jnp.left_shift(jnp.int32(1), nbits - 1 - i)
        c = jnp.sum((eq & (idx < cand)).astype(F32), axis=1, keepdims=True)
        return jnp.where(c < need, cand, jt)

    jt = lax.fori_loop(0, nbits, tie_body, jnp.zeros((E, 1), I32))
    sel = (gt | (eq & (idx <= jt))).astype(F32)
    sel_scr[...] = sel

    r_i = lax.broadcasted_iota(I32, (CHUNK, CHUNK), 0)
    c_i = lax.broadcasted_iota(I32, (CHUNK, CHUNK), 1)
    strict_upper = (r_i < c_i).astype(BF16)
    e_r = lax.broadcasted_iota(I32, (E, E), 0)
    e_c = lax.broadcasted_iota(I32, (E, E), 1)
    strict_lower = (e_c < e_r).astype(BF16)
    run = jnp.zeros((E, 1), F32)
    run_t = jnp.zeros((1, 1), F32)
    for c in range(nchunk):
        sc = sel[:, c * CHUNK:(c + 1) * CHUNK]
        scb = sc.astype(BF16)
        pos_scr[:, c * CHUNK:(c + 1) * CHUNK] = jnp.dot(scb, strict_upper, preferred_element_type=F32) + run
        run = run + jnp.sum(sc, axis=1, keepdims=True)
        cnt_c = jnp.sum(sc, axis=0, keepdims=True)
        cnt8 = jnp.broadcast_to(cnt_c, (8, CHUNK)).astype(BF16)
        base_c = jnp.dot(cnt8, strict_upper, preferred_element_type=F32)[0:1] + run_t
        run_t = run_t + jnp.sum(cnt_c, axis=1, keepdims=True)
        rank_c = jnp.dot(strict_lower, scb, preferred_element_type=F32)
        zoff = (b * zrows).astype(F32)
        tmp_scr[:, c * CHUNK:(c + 1) * CHUNK] = base_c + rank_c + zoff
        off_ref[:, c * CHUNK:(c + 1) * CHUNK] = (base_c + zoff).astype(I32)
        cnt_ref[:, c * CHUNK:(c + 1) * CHUNK] = cnt_c.astype(I32)

    lo_id = lax.broadcasted_iota(I32, (SLOT_LO, S), 0).astype(F32)
    hi_id = lax.broadcasted_iota(I32, (n_hi, S), 0).astype(F32)
    t_row = lax.broadcasted_iota(I32, (1, S), 1)
    t_hi = (t_row // TOK_LO).astype(F32)
    t_lo = (t_row % TOK_LO).astype(F32)

    def split3(x):
        hi = x.astype(BF16).astype(F32)
        mid = (x - hi).astype(BF16).astype(F32)
        lo = (x - hi - mid).astype(BF16).astype(F32)
        return hi, mid, lo

    for e in range(E):
        sel_e = sel_scr[e:e + 1, :]
        pos_e = pos_scr[e:e + 1, :]
        p_hi = jnp.floor(pos_e * (1.0 / SLOT_LO))
        p_lo = pos_e - p_hi * SLOT_LO
        oh_lo = jnp.where((lo_id == p_lo) & (sel_e > 0), 1.0, 0.0).astype(BF16)
        oh_hi = jnp.where(hi_id == p_hi, 1.0, 0.0)
        d = tmp_scr[e:e + 1, :]
        d2 = jnp.floor(d * (1.0 / 1024.0))
        d1 = jnp.floor((d - d2 * 1024.0) * (1.0 / 32.0))
        d0 = d - d2 * 1024.0 - d1 * 32.0
        g_hi, g_mid, g_lo = split3(a_scr[e:e + 1, :])
        vals = (t_hi, t_lo, d2, d1, d0, g_hi, g_mid, g_lo)
        stack = jnp.concatenate([oh_hi * v for v in vals], axis=0).astype(BF16)
        r = lax.dot_general(stack, oh_lo, (((1,), (1,)), ((), ())), preferred_element_type=F32)
        rr = [r[k * n_hi:(k + 1) * n_hi] for k in range(8)]
        tok_ref[e] = (rr[0] * TOK_LO + rr[1]).astype(I32) + b * S
        dst_ref[e] = (rr[2] * 1024.0 + rr[3] * 32.0 + rr[4]).astype(I32)
        gate_ref[e] = rr[5] + rr[6] + rr[7]


def _route(aff, B, S):
    cap = CAPACITY_FACTOR * S // N_EXPERTS
    n_hi = cap // SLOT_LO
    E = N_EXPERTS
    kern = functools.partial(_route_kernel, S=S, cap=cap)
    lists = jax.ShapeDtypeStruct((B, E, n_hi, SLOT_LO), I32)
    return pl.pallas_call(
        kern,
        grid=(B,),
        in_specs=[pl.BlockSpec((S, LANES), lambda b: (b, 0))],
        out_specs=[
            pl.BlockSpec((None, E, n_hi, SLOT_LO), lambda b: (b, 0, 0, 0)),
            pl.BlockSpec((None, E, n_hi, SLOT_LO), lambda b: (b, 0, 0, 0)),
            pl.BlockSpec((None, E, n_hi, SLOT_LO), lambda b: (b, 0, 0, 0)),
            pl.BlockSpec((None, 1, S), lambda b: (b, 0, 0)),
            pl.BlockSpec((None, 1, S), lambda b: (b, 0, 0)),
        ],
        out_shape=[lists, lists, jax.ShapeDtypeStruct((B, E, n_hi, SLOT_LO), F32),
                   jax.ShapeDtypeStruct((B, 1, S), I32), jax.ShapeDtypeStruct((B, 1, S), I32)],
        scratch_shapes=[pltpu.VMEM((E, S), F32)] * 4,
        compiler_params=_cparams(("parallel",)),
        name="route",
    )(aff)


FFN_ROWS = 128
FFN_STEPS = 4
FFN_DOWN_STEPS = 2


def _ffn_kernel(tok_ref, dst_ref, h_hbm, wg_ref, wu_ref, wd_ref, gate_ref, z_hbm, xg, xb, act, yacc, sem_g, sem_s,
                *, tm):
    s = pl.program_id(2)
    q = pl.program_id(0) * pl.num_programs(1) + pl.program_id(1)
    nq = pl.num_programs(0) * pl.num_programs(1)
    tn = yacc.shape[1] // FFN_DOWN_STEPS
    last_step = FFN_STEPS + FFN_DOWN_STEPS - 1

    def gather_row(r):
        return pltpu.make_async_copy(h_hbm.at[pl.ds(tok_ref[0, r], 1), :], xg.at[pl.ds(r, 1), :], sem_g)

    def scatter_row(r):
        return pltpu.make_async_copy(yacc.at[pl.ds(r, 1), :], z_hbm.at[pl.ds(dst_ref[0, r], 1), :], sem_s)

    def for_rows(fn):
        def body(r, c):
            fn(r)
            return c
        lax.fori_loop(0, tm, body, 0, unroll=8)

    @pl.when(s == 0)
    def _():
        for_rows(lambda r: gather_row(r).start())
        for_rows(lambda r: gather_row(r).wait())

        def cast(j, c):
            rows = pl.ds(pl.multiple_of(j * FFN_ROWS, FFN_ROWS), FFN_ROWS)
            xb[rows, :] = xg[rows, :].astype(BF16)
            return c
        lax.fori_loop(0, tm // FFN_ROWS, cast, 0)

    @pl.when(s < FFN_STEPS)
    def _():
        x = xb[...]
        a = jnp.dot(x, wg_ref[...].astype(BF16), preferred_element_type=F32)
        u = jnp.dot(x, wu_ref[...].astype(BF16), preferred_element_type=F32)
        act[s] = (a * _sigmoid(a) * u).astype(BF16)

    @pl.when((s == FFN_STEPS) & (q > 0))
    def _():
        for_rows(lambda r: scatter_row(r).wait())

    @pl.when(s >= FFN_STEPS)
    def _():
        a_all = jnp.concatenate([act[f] for f in range(FFN_STEPS)], axis=1)
        y = jnp.dot(a_all, wd_ref[...].astype(BF16), preferred_element_type=F32)
        col = pl.multiple_of((s - FFN_STEPS) * tn, tn)
        yacc[:, pl.ds(col, tn)] = y * gate_ref[...]

    @pl.when(s == last_step)
    def _():
        for_rows(lambda r: scatter_row(r).start())

    @pl.when((s == last_step) & (q == nq - 1))
    def _():
        for_rows(lambda r: scatter_row(r).wait())


def _expert_ffn(h2, tok, dst, gate, w_gate, w_up, w_down, layer, tm=1024):
    E, M = tok.shape
    T, D = h2.shape
    F = w_gate.shape[-1]
    tm = min(tm, M)
    nm = M // tm
    tf = F // FFN_STEPS
    tn = D // FFN_DOWN_STEPS
    last = FFN_STEPS - 1
    tok3 = tok.reshape(E * nm, 1, tm)
    dst3 = dst.reshape(E * nm, 1, tm)
    smem_spec = pl.BlockSpec((None, 1, tm), lambda e, m, s: (e * nm + m, 0, 0), memory_space=pltpu.SMEM)
    kern = functools.partial(_ffn_kernel, tm=tm)
    return pl.pallas_call(
        kern,
        grid=(E, nm, FFN_STEPS + FFN_DOWN_STEPS),
        in_specs=[
            smem_spec,
            smem_spec,
            pl.BlockSpec(memory_space=pl.ANY),
            pl.BlockSpec((None, None, D, tf), lambda e, m, s: (layer, e, 0, jnp.minimum(s, last))),
            pl.BlockSpec((None, None, D, tf), lambda e, m, s: (layer, e, 0, jnp.minimum(s, last))),
            pl.BlockSpec((None, None, F, tn), lambda e, m, s: (layer, e, 0, jnp.maximum(s - FFN_STEPS, 0))),
            pl.BlockSpec((None, tm, 1), lambda e, m, s: (e, m, 0)),
        ],
        out_specs=pl.BlockSpec(memory_space=pl.ANY),
        out_shape=jax.ShapeDtypeStruct((E * M, D), F32),
        scratch_shapes=[
            pltpu.VMEM((tm, D), F32),
            pltpu.VMEM((tm, D), BF16),
            pltpu.VMEM((FFN_STEPS, tm, tf), BF16),
            pltpu.VMEM((tm, D), F32),
            pltpu.SemaphoreType.DMA(()),
            pltpu.SemaphoreType.DMA(()),
        ],
        compiler_params=_cparams(("arbitrary", "arbitrary", "arbitrary")),
        name="expert_ffn",
    )(tok3, dst3, h2, w_gate, w_up, w_down, gate)


COMB_TOK = 512
COMB_ROWS = 256
COMB_SLOTS = 6
COMB_AHEAD = 4


def _combine_kernel(start_ref, nch_ref, x_ref, off_ref, cnt_ref, g_ref, z_hbm, o_ref, zbuf, sem, st,
                    *, zrows, final):
    i = pl.program_id(0)
    nt = pl.num_programs(0)
    start = start_ref[i]
    n = nch_ref[i]
    i_next = jnp.minimum(i + 1, nt - 1)
    n_next = jnp.where(i + 1 < nt, nch_ref[i_next], 0)

    @pl.when(i == 0)
    def _():
        st[0] = 0
        st[1] = 0

    gc = st[0]
    pre = st[1]
    o_ref[...] = x_ref[...]
    off = off_ref[...]
    end = off + cnt_ref[...]

    def chunk_start(tile, k):
        return pl.multiple_of(jnp.minimum(start_ref[tile] + k * COMB_ROWS, zrows - COMB_ROWS), 8)

    def copy(tile, k, number):
        slot = lax.rem(number, COMB_SLOTS)
        return pltpu.make_async_copy(z_hbm.at[pl.ds(chunk_start(tile, k), COMB_ROWS), :], zbuf.at[slot],
                                     sem.at[slot])

    for j in range(COMB_AHEAD):
        @pl.when((j >= pre) & (j < n))
        def _():
            copy(i, j, gc + j).start()

    cross = n >= COMB_AHEAD

    def body(k, c):
        copy(i, k, gc + k).wait()
        ahead = k + COMB_AHEAD

        @pl.when(ahead < n)
        def _():
            copy(i, ahead, gc + ahead).start()

        @pl.when((ahead >= n) & cross & (ahead - n < n_next))
        def _():
            copy(i_next, ahead - n, gc + ahead).start()

        rows = chunk_start(i, k) + lax.broadcasted_iota(I32, (1, COMB_ROWS), 1)
        lo = jnp.maximum(off, start + k * COMB_ROWS)
        seg = jnp.where((rows >= lo) & (rows < end), 1.0, 0.0).astype(BF16)
        o_ref[...] += jnp.dot(seg, zbuf[lax.rem(gc + k, COMB_SLOTS)].astype(BF16), preferred_element_type=F32)
        return c

    lax.fori_loop(0, n, body, 0)
    st[0] = gc + n
    st[1] = jnp.where(cross, jnp.minimum(COMB_AHEAD, n_next), 0)

    if final:
        x = o_ref[...]
        ms = jnp.mean(x * x, axis=-1, keepdims=True)
        o_ref[...] = x * lax.rsqrt(ms + EPS) * g_ref[...]


def _combine(x2d, z, off, cnt, gain, final):
    T, D = x2d.shape
    zrows = z.shape[0]
    tt = min(COMB_TOK, T)
    nt = T // tt
    start = (off[::tt] // 8) * 8
    last = (off + cnt)[tt - 1::tt]
    nch = (last - start + COMB_ROWS - 1) // COMB_ROWS
    kern = functools.partial(_combine_kernel, zrows=zrows, final=final)
    grid_spec = pltpu.PrefetchScalarGridSpec(
        num_scalar_prefetch=2,
        grid=(nt,),
        in_specs=[
            pl.BlockSpec((tt, D), lambda i, s, n: (i, 0)),
            pl.BlockSpec((tt, 1), lambda i, s, n: (i, 0)),
            pl.BlockSpec((tt, 1), lambda i, s, n: (i, 0)),
            pl.BlockSpec((1, D), lambda i, s, n: (0, 0)),
            pl.BlockSpec(memory_space=pl.ANY),
        ],
        out_specs=pl.BlockSpec((tt, D), lambda i, s, n: (i, 0)),
        scratch_shapes=[pltpu.VMEM((COMB_SLOTS, COMB_ROWS, D), F32), pltpu.SemaphoreType.DMA((COMB_SLOTS,)),
                        pltpu.SMEM((2,), I32)],
    )
    return pl.pallas_call(
        kern,
        grid_spec=grid_spec,
        out_shape=jax.ShapeDtypeStruct((T, D), F32),
        compiler_params=_cparams(("arbitrary",)),
        name="combine",
    )(start.astype(I32), nch.astype(I32), x2d, off[:, None], cnt[:, None], gain, z)


def moe(x2d, h2, aff, w_gate, w_up, w_down, layer, B, S, gain, final):
    E = N_EXPERTS
    cap = CAPACITY_FACTOR * S // E
    tok, dst, gate, off, cnt = _route(aff, B, S)
    to_em = lambda a: jnp.swapaxes(a.reshape(B, E, cap), 0, 1).reshape(E, B * cap)
    z = _expert_ffn(h2, to_em(tok), to_em(dst), to_em(gate)[:, :, None], w_gate, w_up, w_down, layer)
    return _combine(x2d, z, off.reshape(B * S), cnt.reshape(B * S), gain, final)


def kernel(x, norm_mix_g, w_in, b_gates, conv_w, conv_b, na_rpb, na_norm_g, ml_norm_g, w_out,
           norm_ffn_g, w_router, w_gate, w_up, w_down, final_norm_g):
    B, S, D = x.shape
    T = B * S
    depth = w_in.shape[0]
    x2d = x.reshape(T, D)
    w_t = jnp.swapaxes(w_in, 1, 2)
    w_main_t = _cast_main_weights(w_t)

    for l in range(depth):
        proj, gates = _in_proj(x2d, norm_mix_g[l][None, :], w_main_t, w_t, l)

        tb = _na_bias_table(na_rpb[l])
        na_o = _na_attention(proj, tb, B, S)

        g4 = gates.reshape(B, S, 4, ML_HEADS)
        g_rows = jnp.pad(jnp.transpose(g4, (0, 3, 2, 1)), ((0, 0), (0, 0), (0, 4), (0, 0)))
        bg_rows = jnp.pad(b_gates[l].reshape(4, ML_HEADS).T, ((0, 0), (0, 4)))[:, :, None]
        ml_o = _mlstm(proj, g_rows, bg_rows.astype(F32), conv_w[l], conv_b[l][None, :],
                      ml_norm_g[l][None, :], B, S)

        w_r = jnp.pad(w_router[l], ((0, 0), (0, LANES - N_EXPERTS))).astype(BF16)
        x2d, h2, aff = _out_proj(na_o, ml_o, x2d, w_out[l].astype(BF16), na_norm_g[l][None, :],
                                 norm_ffn_g[l][None, :], w_r)

        x2d = moe(x2d, h2, aff, w_gate, w_up, w_down, l, B, S, final_norm_g[None, :], l == depth - 1)

    return x2d.reshape(B, S, D)
```

```python
import functools

import numpy as np
import jax
import jax.numpy as jnp
from jax import lax
from jax.experimental import pallas as pl
from jax.experimental.pallas import tpu as pltpu

F32 = jnp.float32
BF16 = jnp.bfloat16

EPS = 1e-6
GRID_W = 64
NA_HEADS = 8
NA_HEAD_DIM = 128
NA_KR = 8
NA_KC = 16
NA_WIDTH = NA_HEADS * NA_HEAD_DIM
NA_LOOKAHEAD = 6
ML_HEADS = 4
ML_QK_DIM = 128
ML_V_DIM = 256
ML_QK_WIDTH = ML_HEADS * ML_QK_DIM
ML_V_WIDTH = ML_HEADS * ML_V_DIM
CONV_W = 5
N_GATES = 4 * ML_HEADS
N_EXPERTS = 16
CAPACITY_FACTOR = 2
N_MAIN = 3 * NA_WIDTH + 2 * ML_QK_WIDTH + 2 * ML_V_WIDTH
LANES = 128
NEG = -1e30
ML_CHUNK = 256
VMEM_LIMIT = 56 * 1024 * 1024


def _cparams(sem):
    return pltpu.CompilerParams(dimension_semantics=sem, vmem_limit_bytes=VMEM_LIMIT)


_NT = (((1,), (1,)), ((), ()))


def _in_proj_kernel(x_ref, g_ref, w_ref, wg_ref, o_ref, og_ref, h_scr):
    @pl.when(pl.program_id(1) == 0)
    def _():
        x = x_ref[...]
        ms = jnp.mean(x * x, axis=-1, keepdims=True)
        h = (x * lax.rsqrt(ms + EPS) * g_ref[...]).astype(BF16)
        h_scr[...] = h
        og_ref[...] = lax.dot_general(h, wg_ref[...].astype(BF16), _NT, preferred_element_type=F32)

    o_ref[...] = lax.dot_general(h_scr[...], w_ref[...], _NT, preferred_element_type=F32).astype(o_ref.dtype)


def _cast_kernel(w_ref, o_ref):
    o_ref[...] = w_ref[...].astype(o_ref.dtype)


def _cast_main_weights(w_t, tn=1024):
    depth, _, D = w_t.shape
    return pl.pallas_call(
        _cast_kernel,
        grid=(depth, N_MAIN // tn),
        in_specs=[pl.BlockSpec((None, tn, D), lambda l, j: (l, j, 0))],
        out_specs=pl.BlockSpec((None, tn, D), lambda l, j: (l, j, 0)),
        out_shape=jax.ShapeDtypeStruct((depth, N_MAIN, D), BF16),
        compiler_params=_cparams(("parallel", "parallel")),
        name="cast_weights",
    )(w_t)


def _in_proj(x2d, g, w_main_t, w_t, layer, tm=1024, tn=2048):
    T, D = x2d.shape
    N = N_MAIN
    return pl.pallas_call(
        _in_proj_kernel,
        grid=(T // tm, N // tn),
        in_specs=[
            pl.BlockSpec((tm, D), lambda i, j: (i, 0)),
            pl.BlockSpec((1, D), lambda i, j: (0, 0)),
            pl.BlockSpec((None, tn, D), lambda i, j: (layer, j, 0)),
            pl.BlockSpec((None, N_GATES, D), lambda i, j: (layer, N_MAIN // N_GATES, 0)),
        ],
        out_specs=[
            pl.BlockSpec((tm, tn), lambda i, j: (i, j)),
            pl.BlockSpec((tm, N_GATES), lambda i, j: (i, 0)),
        ],
        out_shape=[
            jax.ShapeDtypeStruct((T, N), BF16),
            jax.ShapeDtypeStruct((T, N_GATES), F32),
        ],
        scratch_shapes=[pltpu.VMEM((tm, D), BF16)],
        compiler_params=_cparams(("parallel", "arbitrary")),
        name="in_proj",
    )(x2d, g, w_main_t, w_t)


def _na_table_kernel(x_ref, o_ref):
    q = lax.broadcasted_iota(jnp.int32, (GRID_W, GRID_W), 0)
    kcol = lax.broadcasted_iota(jnp.int32, (GRID_W, GRID_W), 1)
    col_start = jnp.clip(q - NA_KC // 2, 0, GRID_W - NA_KC)
    keep = (kcol >= col_start) & (kcol < col_start + NA_KC)
    for off in range(NA_KR):
        for a in range(NA_KR):
            piece = jnp.where(keep, x_ref[a - off + NA_KR - 1], NEG)
            o_ref[off, :, a * GRID_W:(a + 1) * GRID_W] = piece


def _na_bias_table(rpb):
    n_dc = 2 * NA_KC - 1
    H = rpb.shape[0]
    c = np.arange(GRID_W)
    dc = np.clip(c[None, :] - c[:, None] + (NA_KC - 1), 0, n_dc - 1)
    sel_c = jnp.asarray(dc[:, :, None] == np.arange(n_dc), F32)
    x = jnp.einsum("hij,qkj->hiqk", rpb.astype(F32), sel_c, precision=lax.Precision.HIGHEST)
    return pl.pallas_call(
        _na_table_kernel,
        grid=(H,),
        in_specs=[pl.BlockSpec((None, 2 * NA_KR - 1, GRID_W, GRID_W), lambda h: (h, 0, 0, 0))],
        out_specs=pl.BlockSpec((None, NA_KR, GRID_W, NA_KR * GRID_W), lambda h: (h, 0, 0, 0)),
        out_shape=jax.ShapeDtypeStruct((H, NA_KR, GRID_W, NA_KR * GRID_W), F32),
        compiler_params=_cparams(("parallel",)),
        name="na_table",
    )(x)


def _na_kernel(q_ref, k_ref, v_ref, tb_ref, o_ref, *, rb, rows):
    i = pl.program_id(2)
    scale = NA_HEAD_DIM ** -0.5
    win = NA_KR * GRID_W

    def scores(rr):
        r = i * rb + rr
        rs = jnp.clip(r - NA_KR // 2, 0, rows - NA_KR)
        start = pl.multiple_of(rs * GRID_W, GRID_W)
        q = q_ref[rr * GRID_W:(rr + 1) * GRID_W, :]
        kw = k_ref[pl.ds(start, win), :]
        s = lax.dot_general(q, kw, (((1,), (1,)), ((), ())), preferred_element_type=F32)
        return s, r - rs, start

    queue = [scores(rr) for rr in range(min(NA_LOOKAHEAD, rb))]
    for rr in range(rb):
        s, off, start = queue.pop(0)
        if rr + NA_LOOKAHEAD < rb:
            queue.append(scores(rr + NA_LOOKAHEAD))
        vw = v_ref[pl.ds(start, win), :]
        s = s * scale + tb_ref[off]
        m = jnp.max(s, axis=-1, keepdims=True)
        p = jnp.exp(s - m)
        l = jnp.sum(p, axis=-1, keepdims=True)
        o = jnp.dot(p.astype(BF16), vw, preferred_element_type=F32)
        o_ref[rr * GRID_W:(rr + 1) * GRID_W, :] = (o / l).astype(o_ref.dtype)


def _na_attention(proj, tb, B, S, rb=64):
    T = B * S
    rows = S // GRID_W
    rb = min(rb, rows)
    assert rows >= NA_KR and rows % rb == 0
    nblk = rows // rb
    tq = rb * GRID_W
    kern = functools.partial(_na_kernel, rb=rb, rows=rows)
    return pl.pallas_call(
        kern,
        grid=(B, NA_HEADS, nblk),
        in_specs=[
            pl.BlockSpec((tq, NA_HEAD_DIM), lambda b, h, i: (b * nblk + i, h)),
            pl.BlockSpec((S, NA_HEAD_DIM), lambda b, h, i: (b, NA_HEADS + h)),
            pl.BlockSpec((S, NA_HEAD_DIM), lambda b, h, i: (b, 2 * NA_HEADS + h)),
            pl.BlockSpec((None, NA_KR, GRID_W, NA_KR * GRID_W), lambda b, h, i: (h, 0, 0, 0)),
        ],
        out_specs=pl.BlockSpec((tq, NA_HEAD_DIM), lambda b, h, i: (b * nblk + i, h)),
        out_shape=jax.ShapeDtypeStruct((T, NA_WIDTH), BF16),
        compiler_params=_cparams(("parallel", "parallel", "arbitrary")),
        name="na_attention",
    )(proj, proj, proj, tb)


def _log_sigmoid(x):
    return jnp.minimum(x, 0.0) - jnp.log(1.0 + jnp.exp(-jnp.abs(x)))


def _sigmoid(x):
    return 1.0 / (1.0 + jnp.exp(-x))


def _mlstm_kernel(q_ref, k_ref, v_ref, og_ref, g_ref, bg_ref, cwq_ref, cwk_ref, cbq_ref, cbk_ref,
                  ng_ref, out_ref, qs, kst, rrow, bcol, hf, hb, cst, nst, *, S, L):
    nc = S // L
    halo = 8

    def conv_silu(src_ref, cw, cbv, c, scale):
        base = pl.multiple_of(c * L, L)
        lo = jnp.maximum(base - halo, 0)
        hi = jnp.minimum(base + L, S - halo)
        prev = src_ref[pl.ds(pl.multiple_of(lo, halo), halo), :].astype(F32)
        nxt = src_ref[pl.ds(pl.multiple_of(hi, halo), halo), :].astype(F32)
        prev = jnp.where(c > 0, prev, 0.0)
        nxt = jnp.where(c < nc - 1, nxt, 0.0)
        main = src_ref[pl.ds(base, L), :].astype(F32)
        blk = jnp.concatenate([prev, main, nxt], axis=0)
        n = L + 2 * halo
        acc = jnp.zeros((L, LANES), F32) + cbv
        for w in range(CONV_W):
            sh = (CONV_W // 2 - w) % n
            xs = blk if sh == 0 else pltpu.roll(blk, sh, 0)
            acc = acc + xs[halo:halo + L, :] * cw[w:w + 1, :]
        return acc * _sigmoid(acc) * scale

    cwq, cbq, cwk, cbk = cwq_ref[...], cbq_ref[...], cwk_ref[...], cbk_ref[...]

    def conv_body(c, carry):
        base = pl.multiple_of(c * L, L)
        qs[pl.ds(base, L), :] = conv_silu(q_ref, cwq, cbq, c, ML_QK_DIM ** -0.5).astype(BF16)
        kst[:, pl.ds(base, L)] = conv_silu(k_ref, cwk, cbk, c, 1.0).T.astype(BF16)
        return carry

    lax.fori_loop(0, nc, conv_body, 0)

    t_idx = lax.broadcasted_iota(jnp.int32, (L, L), 0)
    j_idx = lax.broadcasted_iota(jnp.int32, (L, L), 1)
    upper = (t_idx <= j_idx).astype(BF16)
    lower = (t_idx >= j_idx).astype(BF16)

    def split3(x):
        hi = x.astype(BF16).astype(F32)
        mid = (x - hi).astype(BF16).astype(F32)
        lo = (x - hi - mid).astype(BF16).astype(F32)
        return hi, mid, lo

    def gate_body(c, carry):
        base = pl.multiple_of(c * L, L)
        g = g_ref[:, pl.ds(base, L)] + bg_ref[...]
        hi, mid, lo = split3(_log_sigmoid(g))
        lf3 = jnp.concatenate([hi, mid, lo, jnp.zeros_like(hi)], axis=0).astype(BF16)
        pre = jnp.dot(lf3, upper, preferred_element_type=F32)
        suf = jnp.dot(lf3, lower, preferred_element_type=F32)
        pre = pre[0:8] + pre[8:16] + pre[16:24]
        suf = suf[0:8] + suf[8:16] + suf[16:24]
        cum_f, cum_b = pre[1:2], suf[3:4]
        gx_f = cum_f[:, L - 1:L] - cum_f + g[0:1]
        gx_b = cum_b[:, 0:1] - cum_b + g[2:3]
        r = jnp.concatenate([g[0:1], cum_f, g[2:3], cum_b, gx_f, gx_b, g[0:1] - cum_f, g[2:3] - cum_b],
                            axis=0)
        rrow[:, pl.ds(base, L)] = r
        return carry

    lax.fori_loop(0, nc, gate_body, 0, unroll=4)

    r_all = rrow[...]
    row_all = lax.broadcasted_iota(jnp.int32, (8, S), 0)
    lane_all = lax.broadcasted_iota(jnp.int32, (8, S), 1) & (L - 1)
    pm = r_all
    sm = r_all
    sh = 1
    while sh < L:
        pm = jnp.maximum(pm, jnp.where(lane_all >= sh, pltpu.roll(pm, sh, 1), NEG))
        sm = jnp.maximum(sm, jnp.where(lane_all < L - sh, pltpu.roll(sm, S - sh, 1), NEG))
        sh *= 2
    mi = jnp.where(row_all == 6, pm + pltpu.roll(r_all, 5, 0), sm + pltpu.roll(r_all, 4, 0))
    rrow[...] = jnp.where(row_all >= 6, mi, r_all)

    pick_r = lax.broadcasted_iota(jnp.int32, (LANES, 2 * LANES), 0)
    pick_l = lax.broadcasted_iota(jnp.int32, (LANES, 2 * LANES), 1)

    def picker(k_lo, k_hi):
        want = jnp.where(pick_l < LANES, k_lo, k_hi)
        return ((pick_r < 24) & ((pick_r & 7) == want)).astype(BF16)

    pairs = ((1, 6, 0, 1), (4, 3, 2, 3), (7, 5, 4, 5))

    def bcol_body(c, carry):
        base = pl.multiple_of(c * L, L)
        hi, mid, lo = split3(rrow[:, pl.ds(base, L)])
        x = jnp.concatenate([hi, mid, lo, jnp.zeros((LANES - 24, L), F32)], axis=0)
        xt = x.T.astype(BF16)
        for k_lo, k_hi, d_lo, d_hi in pairs:
            y = jnp.dot(xt, picker(k_lo, k_hi), preferred_element_type=F32)
            bcol[d_lo, pl.ds(base, L), :] = y[:, :LANES]
            bcol[d_hi, pl.ds(base, L), :] = y[:, LANES:]
        return carry

    lax.fori_loop(0, nc, bcol_body, 0, unroll=4)

    fwd_mask = j_idx <= t_idx
    bwd_mask = j_idx >= t_idx
    ones_b = jnp.ones((L, LANES), BF16)

    def wide(x, n):
        return jnp.concatenate([x] * n, axis=1)

    def load(c, d, total_row):
        base = pl.multiple_of(c * L, L)
        qc = qs[pl.ds(base, L), :]
        C = cst[d]
        cum = bcol[3 * d, pl.ds(base, L), :]
        return dict(
            base=base, qc=qc, vc=v_ref[pl.ds(base, L), :], kt=kst[:, pl.ds(base, L)],
            cum=cum, mi=bcol[3 * d + 1, pl.ds(base, L), :], gx=bcol[3 * d + 2, pl.ds(base, L), :],
            i_row=rrow[2 * d:2 * d + 1, pl.ds(base, L)],
            cum_row=rrow[2 * d + 1:2 * d + 2, pl.ds(base, L)],
            total=cum[total_row:total_row + 1, :], C=C,
            s=jnp.dot(qc, kst[:, pl.ds(base, L)], preferred_element_type=F32),
            qC=jnp.dot(qc, C.astype(BF16), preferred_element_type=F32),
            qn=jnp.dot(qc, nst[d].astype(BF16), preferred_element_type=F32))

    def weights(x, mask, m_st):
        m_inter = x["cum"] + m_st
        m_j = jnp.maximum(m_inter, x["mi"])
        dmat = jnp.where(mask, wide(x["cum"], L // LANES) - x["cum_row"] + x["i_row"], NEG)
        qkw = (x["s"] * jnp.exp(dmat - wide(m_j, L // LANES))).astype(BF16)
        inter = jnp.exp(m_inter - m_j)
        m_new = jnp.maximum(x["total"] + m_st, jnp.max(x["gx"], axis=0, keepdims=True))
        wk = jnp.exp(x["gx"] - m_new)
        decay = jnp.exp(x["total"] + m_st - m_new)
        vw = (wide(wk, ML_V_DIM // LANES) * x["vc"].astype(F32)).astype(BF16)
        return qkw, inter, m_j, wk.astype(BF16), vw, decay, m_new

    def finish(x, d, qkw, inter, m_j, wk, vw, decay, h_ref):
        base = x["base"]
        den = inter * x["qn"] + jnp.dot(qkw, ones_b, preferred_element_type=F32)
        rden = 1.0 / jnp.maximum(jnp.abs(den), jnp.exp(-m_j))
        num = wide(inter, ML_V_DIM // LANES) * x["qC"] + jnp.dot(qkw, x["vc"], preferred_element_type=F32)
        h_ref[pl.ds(base, L), :] = num * wide(rden, ML_V_DIM // LANES)
        cst[d] = wide(decay, ML_V_DIM // LANES) * x["C"] + jnp.dot(x["kt"], vw, preferred_element_type=F32)
        nst[d] = decay * nst[d] + jnp.dot(x["kt"], wk, preferred_element_type=F32)

    cst[...] = jnp.zeros(cst.shape, F32)
    nst[...] = jnp.zeros(nst.shape, F32)

    def scan_body(t, carry):
        m_f, m_b = carry
        xf = load(t, 0, L - 1)
        xb = load(nc - 1 - t, 1, 0)
        wf = weights(xf, fwd_mask, m_f)
        finish(xf, 0, *wf[:-1], hf)
        wb = weights(xb, bwd_mask, m_b)
        finish(xb, 1, *wb[:-1], hb)
        return wf[-1], wb[-1]

    z_m = jnp.zeros((1, LANES), F32)
    lax.fori_loop(0, nc, scan_body, (z_m, z_m))

    ng = ng_ref[...]

    def fin_body(c, carry):
        base = pl.multiple_of(c * L, L)
        h = hf[pl.ds(base, L), :] + hb[pl.ds(base, L), :]
        ms = jnp.mean(h * h, axis=-1, keepdims=True)
        y = h * lax.rsqrt(ms + EPS) * ng
        y = y * _sigmoid(og_ref[pl.ds(base, L), :].astype(F32))
        out_ref[pl.ds(base, L), :] = y.astype(out_ref.dtype)
        return carry

    lax.fori_loop(0, nc, fin_body, 0)


def _mlstm(proj, gates_rows, bg_rows, conv_w, conv_b, norm_g, B, S):
    T = B * S
    L = min(ML_CHUNK, S)
    assert S % L == 0 and L % LANES == 0
    q0 = 3 * NA_WIDTH // ML_QK_DIM
    k0 = q0 + ML_HEADS
    v0 = (3 * NA_WIDTH + 2 * ML_QK_WIDTH) // ML_V_DIM
    o0 = v0 + ML_HEADS
    kern = functools.partial(_mlstm_kernel, S=S, L=L)
    return pl.pallas_call(
        kern,
        grid=(B, ML_HEADS),
        in_specs=[
            pl.BlockSpec((S, ML_QK_DIM), lambda b, h: (b, q0 + h)),
            pl.BlockSpec((S, ML_QK_DIM), lambda b, h: (b, k0 + h)),
            pl.BlockSpec((S, ML_V_DIM), lambda b, h: (b, v0 + h)),
            pl.BlockSpec((S, ML_V_DIM), lambda b, h: (b, o0 + h)),
            pl.BlockSpec((None, None, 8, S), lambda b, h: (b, h, 0, 0)),
            pl.BlockSpec((None, 8, 1), lambda b, h: (h, 0, 0)),
            pl.BlockSpec((CONV_W, ML_QK_DIM), lambda b, h: (0, h)),
            pl.BlockSpec((CONV_W, ML_QK_DIM), lambda b, h: (0, ML_HEADS + h)),
            pl.BlockSpec((1, ML_QK_DIM), lambda b, h: (0, h)),
            pl.BlockSpec((1, ML_QK_DIM), lambda b, h: (0, ML_HEADS + h)),
            pl.BlockSpec((1, ML_V_DIM), lambda b, h: (0, h)),
        ],
        out_specs=pl.BlockSpec((S, ML_V_DIM), lambda b, h: (b, h)),
        out_shape=jax.ShapeDtypeStruct((T, ML_V_WIDTH), BF16),
        scratch_shapes=[
            pltpu.VMEM((S, ML_QK_DIM), BF16),
            pltpu.VMEM((ML_QK_DIM, S), BF16),
            pltpu.VMEM((8, S), F32),
            pltpu.VMEM((6, S, LANES), F32),
            pltpu.VMEM((S, ML_V_DIM), F32),
            pltpu.VMEM((S, ML_V_DIM), F32),
            pltpu.VMEM((2, ML_QK_DIM, ML_V_DIM), F32),
            pltpu.VMEM((2, ML_QK_DIM, LANES), F32),
        ],
        compiler_params=_cparams(("parallel", "arbitrary")),
        name="mlstm",
    )(proj, proj, proj, proj, gates_rows, bg_rows, conv_w, conv_w, conv_b, conv_b, norm_g)


def _out_proj_kernel(na_ref, ml_ref, x_ref, w_ref, nag_ref, fg_ref, wr_ref, xo_ref, h_ref, aff_ref):
    na = na_ref[...].astype(F32)
    ms = jnp.mean(na * na, axis=-1, keepdims=True)
    na_n = (na * lax.rsqrt(ms + EPS) * nag_ref[...]).astype(BF16)
    upd = jnp.dot(na_n, w_ref[0:NA_WIDTH, :], preferred_element_type=F32)
    upd = upd + jnp.dot(ml_ref[...], w_ref[NA_WIDTH:, :], preferred_element_type=F32)
    xn = x_ref[...] + upd
    xo_ref[...] = xn
    ms2 = jnp.mean(xn * xn, axis=-1, keepdims=True)
    h = xn * lax.rsqrt(ms2 + EPS) * fg_ref[...]
    h_ref[...] = h
    logits = jnp.dot(h.astype(BF16), wr_ref[...], preferred_element_type=F32)
    lane = lax.broadcasted_iota(jnp.int32, logits.shape, 1)
    logits = jnp.where(lane < N_EXPERTS, logits, NEG)
    m = jnp.max(logits, axis=-1, keepdims=True)
    e = jnp.exp(logits - m)
    aff_ref[...] = e / jnp.sum(e, axis=-1, keepdims=True)


def _out_proj(na_o, ml_o, x2d, w_out, na_g, ffn_g, w_router, tm=512):
    T, D = x2d.shape
    return pl.pallas_call(
        _out_proj_kernel,
        grid=(T // tm,),
        in_specs=[
            pl.BlockSpec((tm, NA_WIDTH), lambda i: (i, 0)),
            pl.BlockSpec((tm, ML_V_WIDTH), lambda i: (i, 0)),
            pl.BlockSpec((tm, D), lambda i: (i, 0)),
            pl.BlockSpec((NA_WIDTH + ML_V_WIDTH, D), lambda i: (0, 0)),
            pl.BlockSpec((1, NA_WIDTH), lambda i: (0, 0)),
            pl.BlockSpec((1, D), lambda i: (0, 0)),
            pl.BlockSpec((D, LANES), lambda i: (0, 0)),
        ],
        out_specs=[
            pl.BlockSpec((tm, D), lambda i: (i, 0)),
            pl.BlockSpec((tm, D), lambda i: (i, 0)),
            pl.BlockSpec((tm, LANES), lambda i: (i, 0)),
        ],
        out_shape=[
            jax.ShapeDtypeStruct((T, D), F32),
            jax.ShapeDtypeStruct((T, D), F32),
            jax.ShapeDtypeStruct((T, LANES), F32),
        ],
        compiler_params=_cparams(("parallel",)),
        name="out_proj",
    )(na_o, ml_o, x2d, w_out, na_g, ffn_g, w_router)


I32 = jnp.int32
SLOT_LO = 32
TOK_LO = 64
CHUNK = 256


def _route_kernel(aff_ref, tok_ref, dst_ref, gate_ref, off_ref, cnt_ref, a_scr, sel_scr, pos_scr, tmp_scr,
                  *, S, cap):
    b = pl.program_id(0)
    E = N_EXPERTS
    nchunk = S // CHUNK
    zrows = E * cap
    n_hi = cap // SLOT_LO

    TB = 512
    for c in range(S // TB):
        a_scr[:, c * TB:(c + 1) * TB] = aff_ref[c * TB:(c + 1) * TB, :].T[0:E, :]
    A = a_scr[...]

    def thr_body(i, thr):
        cand = thr | jnp.left_shift(jnp.int32(1), 30 - i)
        cand_f = lax.bitcast_convert_type(cand, F32)
        cnt = jnp.sum((a_scr[...] >= cand_f).astype(F32), axis=1, keepdims=True)
        return jnp.where(cnt >= cap, cand, thr)

    thr = lax.fori_loop(0, 31, thr_body, jnp.zeros((E, 1), I32))
    thr_f = lax.bitcast_convert_type(thr, F32)
    gt = A > thr_f
    eq = A == thr_f
    need = cap - jnp.sum(gt.astype(F32), axis=1, keepdims=True)
    idx = lax.broadcasted_iota(I32, (E, S), 1)

    nbits = S.bit_length() - 1

    def tie_body(i, jt):
        cand = jt | jnp.left_shift(jnp.int32(1), nbits - 1 - i)
        c = jnp.sum((eq & (idx < cand)).astype(F32), axis=1, keepdims=True)
        return jnp.where(c < need, cand, jt)

    jt = lax.fori_loop(0, nbits, tie_body, jnp.zeros((E, 1), I32))
    sel = (gt | (eq & (idx <= jt))).astype(F32)
    sel_scr[...] = sel

    r_i = lax.broadcasted_iota(I32, (CHUNK, CHUNK), 0)
    c_i = lax.broadcasted_iota(I32, (CHUNK, CHUNK), 1)
    strict_upper = (r_i < c_i).astype(BF16)
    e_r = lax.broadcasted_iota(I32, (E, E), 0)
    e_c = lax.broadcasted_iota(I32, (E, E), 1)
    strict_lower = (e_c < e_r).astype(BF16)
    run = jnp.zeros((E, 1), F32)
    run_t = jnp.zeros((1, 1), F32)
    for c in range(nchunk):
        sc = sel[:, c * CHUNK:(c + 1) * CHUNK]
        scb = sc.astype(BF16)
        pos_scr[:, c * CHUNK:(c + 1) * CHUNK] = jnp.dot(scb, strict_upper, preferred_element_type=F32) + run
        run = run + jnp.sum(sc, axis=1, keepdims=True)
        cnt_c = jnp.sum(sc, axis=0, keepdims=True)
        cnt8 = jnp.broadcast_to(cnt_c, (8, CHUNK)).astype(BF16)
        base_c = jnp.dot(cnt8, strict_upper, preferred_element_type=F32)[0:1] + run_t
        run_t = run_t + jnp.sum(cnt_c, axis=1, keepdims=True)
        rank_c = jnp.dot(strict_lower, scb, preferred_element_type=F32)
        zoff = (b * zrows).astype(F32)
        tmp_scr[:, c * CHUNK:(c + 1) * CHUNK] = base_c + rank_c + zoff
        off_ref[:, c * CHUNK:(c + 1) * CHUNK] = (base_c + zoff).astype(I32)
        cnt_ref[:, c * CHUNK:(c + 1) * CHUNK] = cnt_c.astype(I32)

    lo_id = lax.broadcasted_iota(I32, (SLOT_LO, S), 0).astype(F32)
    hi_id = lax.broadcasted_iota(I32, (n_hi, S), 0).astype(F32)
    t_row = lax.broadcasted_iota(I32, (1, S), 1)
    t_hi = (t_row // TOK_LO).astype(F32)
    t_lo = (t_row % TOK_LO).astype(F32)

    def split3(x):
        hi = x.astype(BF16).astype(F32)
        mid = (x - hi).astype(BF16).astype(F32)
        lo = (x - hi - mid).astype(BF16).astype(F32)
        return hi, mid, lo

    for e in range(E):
        sel_e = sel_scr[e:e + 1, :]
        pos_e = pos_scr[e:e + 1, :]
        p_hi = jnp.floor(pos_e * (1.0 / SLOT_LO))
        p_lo = pos_e - p_hi * SLOT_LO
        oh_lo = jnp.where((lo_id == p_lo) & (sel_e > 0), 1.0, 0.0).astype(BF16)
        oh_hi = jnp.where(hi_id == p_hi, 1.0, 0.0)
        d = tmp_scr[e:e + 1, :]
        d2 = jnp.floor(d * (1.0 / 1024.0))
        d1 = jnp.floor((d - d2 * 1024.0) * (1.0 / 32.0))
        d0 = d - d2 * 1024.0 - d1 * 32.0
        g_hi, g_mid, g_lo = split3(a_scr[e:e + 1, :])
        vals = (t_hi, t_lo, d2, d1, d0, g_hi, g_mid, g_lo)
        stack = jnp.concatenate([oh_hi * v for v in vals], axis=0).astype(BF16)
        r = lax.dot_general(stack, oh_lo, (((1,), (1,)), ((), ())), preferred_element_type=F32)
        rr = [r[k * n_hi:(k + 1) * n_hi] for k in range(8)]
        tok_ref[e] = (rr[0] * TOK_LO + rr[1]).astype(I32) + b * S
        dst_ref[e] = (rr[2] * 1024.0 + rr[3] * 32.0 + rr[4]).astype(I32)
        gate_ref[e] = rr[5] + rr[6] + rr[7]


def _route(aff, B, S):
    cap = CAPACITY_FACTOR * S // N_EXPERTS
    n_hi = cap // SLOT_LO
    E = N_EXPERTS
    kern = functools.partial(_route_kernel, S=S, cap=cap)
    lists = jax.ShapeDtypeStruct((B, E, n_hi, SLOT_LO), I32)
    return pl.pallas_call(
        kern,
        grid=(B,),
        in_specs=[pl.BlockSpec((S, LANES), lambda b: (b, 0))],
        out_specs=[
            pl.BlockSpec((None, E, n_hi, SLOT_LO), lambda b: (b, 0, 0, 0)),
            pl.BlockSpec((None, E, n_hi, SLOT_LO), lambda b: (b, 0, 0, 0)),
            pl.BlockSpec((None, E, n_hi, SLOT_LO), lambda b: (b, 0, 0, 0)),
            pl.BlockSpec((None, 1, S), lambda b: (b, 0, 0)),
            pl.BlockSpec((None, 1, S), lambda b: (b, 0, 0)),
        ],
        out_shape=[lists, lists, jax.ShapeDtypeStruct((B, E, n_hi, SLOT_LO), F32),
                   jax.ShapeDtypeStruct((B, 1, S), I32), jax.ShapeDtypeStruct((B, 1, S), I32)],
        scratch_shapes=[pltpu.VMEM((E, S), F32)] * 4,
        compiler_params=_cparams(("parallel",)),
        name="route",
    )(aff)


FFN_ROWS = 128
FFN_STEPS = 4
FFN_DOWN_STEPS = 2


def _ffn_kernel(tok_ref, dst_ref, h_hbm, wg_ref, wu_ref, wd_ref, gate_ref, z_hbm, xg, xb, act, yacc, sem_g, sem_s,
                *, tm):
    s = pl.program_id(2)
    q = pl.program_id(0) * pl.num_programs(1) + pl.program_id(1)
    nq = pl.num_programs(0) * pl.num_programs(1)
    tn = yacc.shape[1] // FFN_DOWN_STEPS
    last_step = FFN_STEPS + FFN_DOWN_STEPS - 1

    def gather_row(r):
        return pltpu.make_async_copy(h_hbm.at[pl.ds(tok_ref[0, r], 1), :], xg.at[pl.ds(r, 1), :], sem_g)

    def scatter_row(r):
        return pltpu.make_async_copy(yacc.at[pl.ds(r, 1), :], z_hbm.at[pl.ds(dst_ref[0, r], 1), :], sem_s)

    def for_rows(fn):
        def body(r, c):
            fn(r)
            return c
        lax.fori_loop(0, tm, body, 0, unroll=8)

    @pl.when(s == 0)
    def _():
        for_rows(lambda r: gather_row(r).start())
        for_rows(lambda r: gather_row(r).wait())

        def cast(j, c):
            rows = pl.ds(pl.multiple_of(j * FFN_ROWS, FFN_ROWS), FFN_ROWS)
            xb[rows, :] = xg[rows, :].astype(BF16)
            return c
        lax.fori_loop(0, tm // FFN_ROWS, cast, 0)

    @pl.when(s < FFN_STEPS)
    def _():
        x = xb[...]
        a = jnp.dot(x, wg_ref[...].astype(BF16), preferred_element_type=F32)
        u = jnp.dot(x, wu_ref[...].astype(BF16), preferred_element_type=F32)
        act[s] = (a * _sigmoid(a) * u).astype(BF16)

    @pl.when((s == FFN_STEPS) & (q > 0))
    def _():
        for_rows(lambda r: scatter_row(r).wait())

    @pl.when(s >= FFN_STEPS)
    def _():
        a_all = jnp.concatenate([act[f] for f in range(FFN_STEPS)], axis=1)
        y = jnp.dot(a_all, wd_ref[...].astype(BF16), preferred_element_type=F32)
        col = pl.multiple_of((s - FFN_STEPS) * tn, tn)
        yacc[:, pl.ds(col, tn)] = y * gate_ref[...]

    @pl.when(s == last_step)
    def _():
        for_rows(lambda r: scatter_row(r).start())

    @pl.when((s == last_step) & (q == nq - 1))
    def _():
        for_rows(lambda r: scatter_row(r).wait())


def _expert_ffn(h2, tok, dst, gate, w_gate, w_up, w_down, layer, tm=1024):
    E, M = tok.shape
    T, D = h2.shape
    F = w_gate.shape[-1]
    tm = min(tm, M)
    nm = M // tm
    tf = F // FFN_STEPS
    tn = D // FFN_DOWN_STEPS
    last = FFN_STEPS - 1
    tok3 = tok.reshape(E * nm, 1, tm)
    dst3 = dst.reshape(E * nm, 1, tm)
    smem_spec = pl.BlockSpec((None, 1, tm), lambda e, m, s: (e * nm + m, 0, 0), memory_space=pltpu.SMEM)
    kern = functools.partial(_ffn_kernel, tm=tm)
    return pl.pallas_call(
        kern,
        grid=(E, nm, FFN_STEPS + FFN_DOWN_STEPS),
        in_specs=[
            smem_spec,
            smem_spec,
            pl.BlockSpec(memory_space=pl.ANY),
            pl.BlockSpec((None, None, D, tf), lambda e, m, s: (layer, e, 0, jnp.minimum(s, last))),
            pl.BlockSpec((None, None, D, tf), lambda e, m, s: (layer, e, 0, jnp.minimum(s, last))),
            pl.BlockSpec((None, None, F, tn), lambda e, m, s: (layer, e, 0, jnp.maximum(s - FFN_STEPS, 0))),
            pl.BlockSpec((None, tm, 1), lambda e, m, s: (e, m, 0)),
        ],
        out_specs=pl.BlockSpec(memory_space=pl.ANY),
        out_shape=jax.ShapeDtypeStruct((E * M, D), F32),
        scratch_shapes=[
            pltpu.VMEM((tm, D), F32),
            pltpu.VMEM((tm, D), BF16),
            pltpu.VMEM((FFN_STEPS, tm, tf), BF16),
            pltpu.VMEM((tm, D), F32),
            pltpu.SemaphoreType.DMA(()),
            pltpu.SemaphoreType.DMA(()),
        ],
        compiler_params=_cparams(("arbitrary", "arbitrary", "arbitrary")),
        name="expert_ffn",
    )(tok3, dst3, h2, w_gate, w_up, w_down, gate)


COMB_TOK = 512
COMB_ROWS = 256
COMB_SLOTS = 6
COMB_AHEAD = 4


def _combine_kernel(start_ref, nch_ref, x_ref, off_ref, cnt_ref, g_ref, z_hbm, o_ref, zbuf, sem, st,
                    *, zrows, final):
    i = pl.program_id(0)
    nt = pl.num_programs(0)
    start = start_ref[i]
    n = nch_ref[i]
    i_next = jnp.minimum(i + 1, nt - 1)
    n_next = jnp.where(i + 1 < nt, nch_ref[i_next], 0)

    @pl.when(i == 0)
    def _():
        st[0] = 0
        st[1] = 0

    gc = st[0]
    pre = st[1]
    o_ref[...] = x_ref[...]
    off = off_ref[...]
    end = off + cnt_ref[...]

    def chunk_start(tile, k):
        return pl.multiple_of(jnp.minimum(start_ref[tile] + k * COMB_ROWS, zrows - COMB_ROWS), 8)

    def copy(tile, k, number):
        slot = lax.rem(number, COMB_SLOTS)
        return pltpu.make_async_copy(z_hbm.at[pl.ds(chunk_start(tile, k), COMB_ROWS), :], zbuf.at[slot],
                                     sem.at[slot])

    for j in range(COMB_AHEAD):
        @pl.when((j >= pre) & (j < n))
        def _():
            copy(i, j, gc + j).start()

    cross = n >= COMB_AHEAD

    def body(k, c):
        copy(i, k, gc + k).wait()
        ahead = k + COMB_AHEAD

        @pl.when(ahead < n)
        def _():
            copy(i, ahead, gc + ahead).start()

        @pl.when((ahead >= n) & cross & (ahead - n < n_next))
        def _():
            copy(i_next, ahead - n, gc + ahead).start()

        rows = chunk_start(i, k) + lax.broadcasted_iota(I32, (1, COMB_ROWS), 1)
        lo = jnp.maximum(off, start + k * COMB_ROWS)
        seg = jnp.where((rows >= lo) & (rows < end), 1.0, 0.0).astype(BF16)
        o_ref[...] += jnp.dot(seg, zbuf[lax.rem(gc + k, COMB_SLOTS)].astype(BF16), preferred_element_type=F32)
        return c

    lax.fori_loop(0, n, body, 0)
    st[0] = gc + n
    st[1] = jnp.where(cross, jnp.minimum(COMB_AHEAD, n_next), 0)

    if final:
        x = o_ref[...]
        ms = jnp.mean(x * x, axis=-1, keepdims=True)
        o_ref[...] = x * lax.rsqrt(ms + EPS) * g_ref[...]


def _combine(x2d, z, off, cnt, gain, final):
    T, D = x2d.shape
    zrows = z.shape[0]
    tt = min(COMB_TOK, T)
    nt = T // tt
    start = (off[::tt] // 8) * 8
    last = (off + cnt)[tt - 1::tt]
    nch = (last - start + COMB_ROWS - 1) // COMB_ROWS
    kern = functools.partial(_combine_kernel, zrows=zrows, final=final)
    grid_spec = pltpu.PrefetchScalarGridSpec(
        num_scalar_prefetch=2,
        grid=(nt,),
        in_specs=[
            pl.BlockSpec((tt, D), lambda i, s, n: (i, 0)),
            pl.BlockSpec((tt, 1), lambda i, s, n: (i, 0)),
            pl.BlockSpec((tt, 1), lambda i, s, n: (i, 0)),
            pl.BlockSpec((1, D), lambda i, s, n: (0, 0)),
            pl.BlockSpec(memory_space=pl.ANY),
        ],
        out_specs=pl.BlockSpec((tt, D), lambda i, s, n: (i, 0)),
        scratch_shapes=[pltpu.VMEM((COMB_SLOTS, COMB_ROWS, D), F32), pltpu.SemaphoreType.DMA((COMB_SLOTS,)),
                        pltpu.SMEM((2,), I32)],
    )
    return pl.pallas_call(
        kern,
        grid_spec=grid_spec,
        out_shape=jax.ShapeDtypeStruct((T, D), F32),
        compiler_params=_cparams(("arbitrary",)),
        name="combine",
    )(start.astype(I32), nch.astype(I32), x2d, off[:, None], cnt[:, None], gain, z)


def moe(x2d, h2, aff, w_gate, w_up, w_down, layer, B, S, gain, final):
    E = N_EXPERTS
    cap = CAPACITY_FACTOR * S // E
    tok, dst, gate, off, cnt = _route(aff, B, S)
    to_em = lambda a: jnp.swapaxes(a.reshape(B, E, cap), 0, 1).reshape(E, B * cap)
    z = _expert_ffn(h2, to_em(tok), to_em(dst), to_em(gate)[:, :, None], w_gate, w_up, w_down, layer)
    return _combine(x2d, z, off.reshape(B * S), cnt.reshape(B * S), gain, final)


def kernel(x, norm_mix_g, w_in, b_gates, conv_w, conv_b, na_rpb, na_norm_g, ml_norm_g, w_out,
           norm_ffn_g, w_router, w_gate, w_up, w_down, final_norm_g):
    B, S, D = x.shape
    T = B * S
    depth = w_in.shape[0]
    x2d = x.reshape(T, D)
    w_t = jnp.swapaxes(w_in, 1, 2)
    w_main_t = _cast_main_weights(w_t)

    for l in range(depth):
        proj, gates = _in_proj(x2d, norm_mix_g[l][None, :], w_main_t, w_t, l)

        tb = _na_bias_table(na_rpb[l])
        na_o = _na_attention(proj, tb, B, S)

        g4 = gates.reshape(B, S, 4, ML_HEADS)
        g_rows = jnp.pad(jnp.transpose(g4, (0, 3, 2, 1)), ((0, 0), (0, 0), (0, 4), (0, 0)))
        bg_rows = jnp.pad(b_gates[l].reshape(4, ML_HEADS).T, ((0, 0), (0, 4)))[:, :, None]
        ml_o = _mlstm(proj, g_rows, bg_rows.astype(F32), conv_w[l], conv_b[l][None, :],
                      ml_norm_g[l][None, :], B, S)

        w_r = jnp.pad(w_router[l], ((0, 0), (0, LANES - N_EXPERTS))).astype(BF16)
        x2d, h2, aff = _out_proj(na_o, ml_o, x2d, w_out[l].astype(BF16), na_norm_g[l][None, :],
                                 norm_ffn_g[l][None, :], w_r)

        x2d = moe(x2d, h2, aff, w_gate, w_up, w_down, l, B, S, final_norm_g[None, :], l == depth - 1)

    return x2d.reshape(B, S, D)
```

```python
import functools

import numpy as np
import jax
import jax.numpy as jnp
from jax import lax
from jax.experimental import pallas as pl
from jax.experimental.pallas import tpu as pltpu

F32 = jnp.float32
BF16 = jnp.bfloat16

EPS = 1e-6
GRID_W = 64
NA_HEADS = 8
NA_HEAD_DIM = 128
NA_KR = 8
NA_KC = 16
NA_WIDTH = NA_HEADS * NA_HEAD_DIM
NA_LOOKAHEAD = 10
ML_HEADS = 4
ML_QK_DIM = 128
ML_V_DIM = 256
ML_QK_WIDTH = ML_HEADS * ML_QK_DIM
ML_V_WIDTH = ML_HEADS * ML_V_DIM
CONV_W = 5
N_GATES = 4 * ML_HEADS
N_EXPERTS = 16
CAPACITY_FACTOR = 2
N_MAIN = 3 * NA_WIDTH + 2 * ML_QK_WIDTH + 2 * ML_V_WIDTH
LANES = 128
NEG = -1e30
ML_CHUNK = 256
VMEM_LIMIT = 56 * 1024 * 1024


def _cparams(sem):
    return pltpu.CompilerParams(dimension_semantics=sem, vmem_limit_bytes=VMEM_LIMIT)


_NT = (((1,), (1,)), ((), ()))


def _in_proj_kernel(x_ref, g_ref, w_ref, wg_ref, o_ref, og_ref, h_scr):
    @pl.when(pl.program_id(1) == 0)
    def _():
        x = x_ref[...]
        ms = jnp.mean(x * x, axis=-1, keepdims=True)
        h = (x * lax.rsqrt(ms + EPS) * g_ref[...]).astype(BF16)
        h_scr[...] = h
        og_ref[...] = lax.dot_general(h, wg_ref[...].astype(BF16), _NT, preferred_element_type=F32)

    o_ref[...] = lax.dot_general(h_scr[...], w_ref[...], _NT, preferred_element_type=F32).astype(o_ref.dtype)


def _cast_kernel(w_ref, o_ref):
    o_ref[...] = w_ref[...].astype(o_ref.dtype)


def _cast_main_weights(w_t, tn=1024):
    depth, _, D = w_t.shape
    return pl.pallas_call(
        _cast_kernel,
        grid=(depth, N_MAIN // tn),
        in_specs=[pl.BlockSpec((None, tn, D), lambda l, j: (l, j, 0))],
        out_specs=pl.BlockSpec((None, tn, D), lambda l, j: (l, j, 0)),
        out_shape=jax.ShapeDtypeStruct((depth, N_MAIN, D), BF16),
        compiler_params=_cparams(("parallel", "parallel")),
        name="cast_weights",
    )(w_t)


def _in_proj(x2d, g, w_main_t, w_t, layer, tm=1024, tn=2048):
    T, D = x2d.shape
    N = N_MAIN
    return pl.pallas_call(
        _in_proj_kernel,
        grid=(T // tm, N // tn),
        in_specs=[
            pl.BlockSpec((tm, D), lambda i, j: (i, 0)),
            pl.BlockSpec((1, D), lambda i, j: (0, 0)),
            pl.BlockSpec((None, tn, D), lambda i, j: (layer, j, 0)),
            pl.BlockSpec((None, N_GATES, D), lambda i, j: (layer, N_MAIN // N_GATES, 0)),
        ],
        out_specs=[
            pl.BlockSpec((tm, tn), lambda i, j: (i, j)),
            pl.BlockSpec((tm, N_GATES), lambda i, j: (i, 0)),
        ],
        out_shape=[
            jax.ShapeDtypeStruct((T, N), BF16),
            jax.ShapeDtypeStruct((T, N_GATES), F32),
        ],
        scratch_shapes=[pltpu.VMEM((tm, D), BF16)],
        compiler_params=_cparams(("parallel", "arbitrary")),
        name="in_proj",
    )(x2d, g, w_main_t, w_t)


def _na_table_kernel(x_ref, o_ref):
    q = lax.broadcasted_iota(jnp.int32, (GRID_W, GRID_W), 0)
    kcol = lax.broadcasted_iota(jnp.int32, (GRID_W, GRID_W), 1)
    col_start = jnp.clip(q - NA_KC // 2, 0, GRID_W - NA_KC)
    keep = (kcol >= col_start) & (kcol < col_start + NA_KC)
    for off in range(NA_KR):
        for a in range(NA_KR):
            piece = jnp.where(keep, x_ref[a - off + NA_KR - 1], NEG)
            o_ref[off, :, a * GRID_W:(a + 1) * GRID_W] = piece


def _na_bias_table(rpb):
    n_dc = 2 * NA_KC - 1
    H = rpb.shape[0]
    c = np.arange(GRID_W)
    dc = np.clip(c[None, :] - c[:, None] + (NA_KC - 1), 0, n_dc - 1)
    sel_c = jnp.asarray(dc[:, :, None] == np.arange(n_dc), F32)
    x = jnp.einsum("hij,qkj->hiqk", rpb.astype(F32), sel_c, precision=lax.Precision.HIGHEST)
    return pl.pallas_call(
        _na_table_kernel,
        grid=(H,),
        in_specs=[pl.BlockSpec((None, 2 * NA_KR - 1, GRID_W, GRID_W), lambda h: (h, 0, 0, 0))],
        out_specs=pl.BlockSpec((None, NA_KR, GRID_W, NA_KR * GRID_W), lambda h: (h, 0, 0, 0)),
        out_shape=jax.ShapeDtypeStruct((H, NA_KR, GRID_W, NA_KR * GRID_W), F32),
        compiler_params=_cparams(("parallel",)),
        name="na_table",
    )(x)


def _na_kernel(q_ref, k_ref, v_ref, tb_ref, o_ref, *, rb, rows):
    i = pl.program_id(2)
    scale = NA_HEAD_DIM ** -0.5
    win = NA_KR * GRID_W

    def scores(rr):
        r = i * rb + rr
        rs = jnp.clip(r - NA_KR // 2, 0, rows - NA_KR)
        start = pl.multiple_of(rs * GRID_W, GRID_W)
        q = q_ref[rr * GRID_W:(rr + 1) * GRID_W, :]
        kw = k_ref[pl.ds(start, win), :]
        s = lax.dot_general(q, kw, (((1,), (1,)), ((), ())), preferred_element_type=F32)
        return s, r - rs, start

    queue = [scores(rr) for rr in range(min(NA_LOOKAHEAD, rb))]
    for rr in range(rb):
        s, off, start = queue.pop(0)
        if rr + NA_LOOKAHEAD < rb:
            queue.append(scores(rr + NA_LOOKAHEAD))
        vw = v_ref[pl.ds(start, win), :]
        s = s * scale + tb_ref[off]
        m = jnp.max(s, axis=-1, keepdims=True)
        p = jnp.exp(s - m)
        l = jnp.sum(p, axis=-1, keepdims=True)
        o = jnp.dot(p.astype(BF16), vw, preferred_element_type=F32)
        o_ref[rr * GRID_W:(rr + 1) * GRID_W, :] = (o / l).astype(o_ref.dtype)


def _na_attention(proj, tb, B, S, rb=64):
    T = B * S
    rows = S // GRID_W
    rb = min(rb, rows)
    assert rows >= NA_KR and rows % rb == 0
    nblk = rows // rb
    tq = rb * GRID_W
    kern = functools.partial(_na_kernel, rb=rb, rows=rows)
    return pl.pallas_call(
        kern,
        grid=(B, NA_HEADS, nblk),
        in_specs=[
            pl.BlockSpec((tq, NA_HEAD_DIM), lambda b, h, i: (b * nblk + i, h)),
            pl.BlockSpec((S, NA_HEAD_DIM), lambda b, h, i: (b, NA_HEADS + h)),
            pl.BlockSpec((S, NA_HEAD_DIM), lambda b, h, i: (b, 2 * NA_HEADS + h)),
            pl.BlockSpec((None, NA_KR, GRID_W, NA_KR * GRID_W), lambda b, h, i: (h, 0, 0, 0)),
        ],
        out_specs=pl.BlockSpec((tq, NA_HEAD_DIM), lambda b, h, i: (b * nblk + i, h)),
        out_shape=jax.ShapeDtypeStruct((T, NA_WIDTH), BF16),
        compiler_params=_cparams(("parallel", "parallel", "arbitrary")),
        name="na_attention",
    )(proj, proj, proj, tb)


def _log_sigmoid(x):
    return jnp.minimum(x, 0.0) - jnp.log(1.0 + jnp.exp(-jnp.abs(x)))


def _sigmoid(x):
    return 1.0 / (1.0 + jnp.exp(-x))


def _mlstm_kernel(q_ref, k_ref, v_ref, og_ref, g_ref, bg_ref, cwq_ref, cwk_ref, cbq_ref, cbk_ref,
                  ng_ref, out_ref, qs, kst, rrow, bcol, hf, hb, cst, nst, *, S, L):
    nc = S // L
    halo = 8

    def conv_silu(src_ref, cw, cbv, c, scale):
        base = pl.multiple_of(c * L, L)
        lo = jnp.maximum(base - halo, 0)
        hi = jnp.minimum(base + L, S - halo)
        prev = src_ref[pl.ds(pl.multiple_of(lo, halo), halo), :].astype(F32)
        nxt = src_ref[pl.ds(pl.multiple_of(hi, halo), halo), :].astype(F32)
        prev = jnp.where(c > 0, prev, 0.0)
        nxt = jnp.where(c < nc - 1, nxt, 0.0)
        main = src_ref[pl.ds(base, L), :].astype(F32)
        blk = jnp.concatenate([prev, main, nxt], axis=0)
        n = L + 2 * halo
        acc = jnp.zeros((L, LANES), F32) + cbv
        for w in range(CONV_W):
            sh = (CONV_W // 2 - w) % n
            xs = blk if sh == 0 else pltpu.roll(blk, sh, 0)
            acc = acc + xs[halo:halo + L, :] * cw[w:w + 1, :]
        return acc * _sigmoid(acc) * scale

    cwq, cbq, cwk, cbk = cwq_ref[...], cbq_ref[...], cwk_ref[...], cbk_ref[...]

    def conv_body(c, carry):
        base = pl.multiple_of(c * L, L)
        qs[pl.ds(base, L), :] = conv_silu(q_ref, cwq, cbq, c, ML_QK_DIM ** -0.5).astype(BF16)
        kst[:, pl.ds(base, L)] = conv_silu(k_ref, cwk, cbk, c, 1.0).T.astype(BF16)
        return carry

    lax.fori_loop(0, nc, conv_body, 0)

    t_idx = lax.broadcasted_iota(jnp.int32, (L, L), 0)
    j_idx = lax.broadcasted_iota(jnp.int32, (L, L), 1)
    upper = (t_idx <= j_idx).astype(BF16)
    lower = (t_idx >= j_idx).astype(BF16)

    def split3(x):
        hi = x.astype(BF16).astype(F32)
        mid = (x - hi).astype(BF16).astype(F32)
        lo = (x - hi - mid).astype(BF16).astype(F32)
        return hi, mid, lo

    def gate_body(c, carry):
        base = pl.multiple_of(c * L, L)
        g = g_ref[:, pl.ds(base, L)] + bg_ref[...]
        hi, mid, lo = split3(_log_sigmoid(g))
        lf3 = jnp.concatenate([hi, mid, lo, jnp.zeros_like(hi)], axis=0).astype(BF16)
        pre = jnp.dot(lf3, upper, preferred_element_type=F32)
        suf = jnp.dot(lf3, lower, preferred_element_type=F32)
        pre = pre[0:8] + pre[8:16] + pre[16:24]
        suf = suf[0:8] + suf[8:16] + suf[16:24]
        cum_f, cum_b = pre[1:2], suf[3:4]
        gx_f = cum_f[:, L - 1:L] - cum_f + g[0:1]
        gx_b = cum_b[:, 0:1] - cum_b + g[2:3]
        r = jnp.concatenate([g[0:1], cum_f, g[2:3], cum_b, gx_f, gx_b, g[0:1] - cum_f, g[2:3] - cum_b],
                            axis=0)
        rrow[:, pl.ds(base, L)] = r
        return carry

    lax.fori_loop(0, nc, gate_body, 0, unroll=4)

    r_all = rrow[...]
    row_all = lax.broadcasted_iota(jnp.int32, (8, S), 0)
    lane_all = lax.broadcasted_iota(jnp.int32, (8, S), 1) & (L - 1)
    pm = r_all
    sm = r_all
    sh = 1
    while sh < L:
        pm = jnp.maximum(pm, jnp.where(lane_all >= sh, pltpu.roll(pm, sh, 1), NEG))
        sm = jnp.maximum(sm, jnp.where(lane_all < L - sh, pltpu.roll(sm, S - sh, 1), NEG))
        sh *= 2
    mi = jnp.where(row_all == 6, pm + pltpu.roll(r_all, 5, 0), sm + pltpu.roll(r_all, 4, 0))
    rrow[...] = jnp.where(row_all >= 6, mi, r_all)

    pick_r = lax.broadcasted_iota(jnp.int32, (LANES, 2 * LANES), 0)
    pick_l = lax.broadcasted_iota(jnp.int32, (LANES, 2 * LANES), 1)

    def picker(k_lo, k_hi):
        want = jnp.where(pick_l < LANES, k_lo, k_hi)
        return ((pick_r < 24) & ((pick_r & 7) == want)).astype(BF16)

    pairs = ((1, 6, 0, 1), (4, 3, 2, 3), (7, 5, 4, 5))

    def bcol_body(c, carry):
        base = pl.multiple_of(c * L, L)
        hi, mid, lo = split3(rrow[:, pl.ds(base, L)])
        x = jnp.concatenate([hi, mid, lo, jnp.zeros((LANES - 24, L), F32)], axis=0)
        xt = x.T.astype(BF16)
        for k_lo, k_hi, d_lo, d_hi in pairs:
            y = jnp.dot(xt, picker(k_lo, k_hi), preferred_element_type=F32)
            bcol[d_lo, pl.ds(base, L), :] = y[:, :LANES]
            bcol[d_hi, pl.ds(base, L), :] = y[:, LANES:]
        return carry

    lax.fori_loop(0, nc, bcol_body, 0, unroll=4)

    fwd_mask = j_idx <= t_idx
    bwd_mask = j_idx >= t_idx
    ones_b = jnp.ones((L, LANES), BF16)

    def wide(x, n):
        return jnp.concatenate([x] * n, axis=1)

    def load(c, d, total_row):
        base = pl.multiple_of(c * L, L)
        qc = qs[pl.ds(base, L), :]
        C = cst[d]
        cum = bcol[3 * d, pl.ds(base, L), :]
        return dict(
            base=base, qc=qc, vc=v_ref[pl.ds(base, L), :], kt=kst[:, pl.ds(base, L)],
            cum=cum, mi=bcol[3 * d + 1, pl.ds(base, L), :], gx=bcol[3 * d + 2, pl.ds(base, L), :],
            i_row=rrow[2 * d:2 * d + 1, pl.ds(base, L)],
            cum_row=rrow[2 * d + 1:2 * d + 2, pl.ds(base, L)],
            total=cum[total_row:total_row + 1, :], C=C,
            s=jnp.dot(qc, kst[:, pl.ds(base, L)], preferred_element_type=F32),
            qC=jnp.dot(qc, C.astype(BF16), preferred_element_type=F32),
            qn=jnp.dot(qc, nst[d].astype(BF16), preferred_element_type=F32))

    def weights(x, mask, m_st):
        m_inter = x["cum"] + m_st
        m_j = jnp.maximum(m_inter, x["mi"])
        dmat = jnp.where(mask, wide(x["cum"], L // LANES) - x["cum_row"] + x["i_row"], NEG)
        qkw = (x["s"] * jnp.exp(dmat - wide(m_j, L // LANES))).astype(BF16)
        inter = jnp.exp(m_inter - m_j)
        m_new = jnp.maximum(x["total"] + m_st, jnp.max(x["gx"], axis=0, keepdims=True))
        wk = jnp.exp(x["gx"] - m_new)
        decay = jnp.exp(x["total"] + m_st - m_new)
        vw = (wide(wk, ML_V_DIM // LANES) * x["vc"].astype(F32)).astype(BF16)
        return qkw, inter, m_j, wk.astype(BF16), vw, decay, m_new

    def finish(x, d, qkw, inter, m_j, wk, vw, decay, h_ref):
        base = x["base"]
        den = inter * x["qn"] + jnp.dot(qkw, ones_b, preferred_element_type=F32)
        rden = 1.0 / jnp.maximum(jnp.abs(den), jnp.exp(-m_j))
        num = wide(inter, ML_V_DIM // LANES) * x["qC"] + jnp.dot(qkw, x["vc"], preferred_element_type=F32)
        h_ref[pl.ds(base, L), :] = num * wide(rden, ML_V_DIM // LANES)
        cst[d] = wide(decay, ML_V_DIM // LANES) * x["C"] + jnp.dot(x["kt"], vw, preferred_element_type=F32)
        nst[d] = decay * nst[d] + jnp.dot(x["kt"], wk, preferred_element_type=F32)

    cst[...] = jnp.zeros(cst.shape, F32)
    nst[...] = jnp.zeros(nst.shape, F32)

    def scan_body(t, carry):
        m_f, m_b = carry
        xf = load(t, 0, L - 1)
        xb = load(nc - 1 - t, 1, 0)
        wf = weights(xf, fwd_mask, m_f)
        finish(xf, 0, *wf[:-1], hf)
        wb = weights(xb, bwd_mask, m_b)
        finish(xb, 1, *wb[:-1], hb)
        return wf[-1], wb[-1]

    z_m = jnp.zeros((1, LANES), F32)
    lax.fori_loop(0, nc, scan_body, (z_m, z_m), unroll=2)

    ng = ng_ref[...]

    def fin_body(c, carry):
        base = pl.multiple_of(c * L, L)
        h = hf[pl.ds(base, L), :] + hb[pl.ds(base, L), :]
        ms = jnp.mean(h * h, axis=-1, keepdims=True)
        y = h * lax.rsqrt(ms + EPS) * ng
        y = y * _sigmoid(og_ref[pl.ds(base, L), :].astype(F32))
        out_ref[pl.ds(base, L), :] = y.astype(out_ref.dtype)
        return carry

    lax.fori_loop(0, nc, fin_body, 0)


def _mlstm(proj, gates_rows, bg_rows, conv_w, conv_b, norm_g, B, S):
    T = B * S
    L = min(ML_CHUNK, S)
    assert S % L == 0 and L % LANES == 0
    q0 = 3 * NA_WIDTH // ML_QK_DIM
    k0 = q0 + ML_HEADS
    v0 = (3 * NA_WIDTH + 2 * ML_QK_WIDTH) // ML_V_DIM
    o0 = v0 + ML_HEADS
    kern = functools.partial(_mlstm_kernel, S=S, L=L)
    return pl.pallas_call(
        kern,
        grid=(B, ML_HEADS),
        in_specs=[
            pl.BlockSpec((S, ML_QK_DIM), lambda b, h: (b, q0 + h)),
            pl.BlockSpec((S, ML_QK_DIM), lambda b, h: (b, k0 + h)),
            pl.BlockSpec((S, ML_V_DIM), lambda b, h: (b, v0 + h)),
            pl.BlockSpec((S, ML_V_DIM), lambda b, h: (b, o0 + h)),
            pl.BlockSpec((None, None, 8, S), lambda b, h: (b, h, 0, 0)),
            pl.BlockSpec((None, 8, 1), lambda b, h: (h, 0, 0)),
            pl.BlockSpec((CONV_W, ML_QK_DIM), lambda b, h: (0, h)),
            pl.BlockSpec((CONV_W, ML_QK_DIM), lambda b, h: (0, ML_HEADS + h)),
            pl.BlockSpec((1, ML_QK_DIM), lambda b, h: (0, h)),
            pl.BlockSpec((1, ML_QK_DIM), lambda b, h: (0, ML_HEADS + h)),
            pl.BlockSpec((1, ML_V_DIM), lambda b, h: (0, h)),
        ],
        out_specs=pl.BlockSpec((S, ML_V_DIM), lambda b, h: (b, h)),
        out_shape=jax.ShapeDtypeStruct((T, ML_V_WIDTH), BF16),
        scratch_shapes=[
            pltpu.VMEM((S, ML_QK_DIM), BF16),
            pltpu.VMEM((ML_QK_DIM, S), BF16),
            pltpu.VMEM((8, S), F32),
            pltpu.VMEM((6, S, LANES), F32),
            pltpu.VMEM((S, ML_V_DIM), F32),
            pltpu.VMEM((S, ML_V_DIM), F32),
            pltpu.VMEM((2, ML_QK_DIM, ML_V_DIM), F32),
            pltpu.VMEM((2, ML_QK_DIM, LANES), F32),
        ],
        compiler_params=_cparams(("parallel", "arbitrary")),
        name="mlstm",
    )(proj, proj, proj, proj, gates_rows, bg_rows, conv_w, conv_w, conv_b, conv_b, norm_g)


def _out_proj_kernel(na_ref, ml_ref, x_ref, w_ref, nag_ref, fg_ref, wr_ref, xo_ref, h_ref, aff_ref):
    na = na_ref[...].astype(F32)
    ms = jnp.mean(na * na, axis=-1, keepdims=True)
    na_n = (na * lax.rsqrt(ms + EPS) * nag_ref[...]).astype(BF16)
    upd = jnp.dot(na_n, w_ref[0:NA_WIDTH, :], preferred_element_type=F32)
    upd = upd + jnp.dot(ml_ref[...], w_ref[NA_WIDTH:, :], preferred_element_type=F32)
    xn = x_ref[...] + upd
    xo_ref[...] = xn
    ms2 = jnp.mean(xn * xn, axis=-1, keepdims=True)
    h = xn * lax.rsqrt(ms2 + EPS) * fg_ref[...]
    h_ref[...] = h
    logits = jnp.dot(h.astype(BF16), wr_ref[...], preferred_element_type=F32)
    lane = lax.broadcasted_iota(jnp.int32, logits.shape, 1)
    logits = jnp.where(lane < N_EXPERTS, logits, NEG)
    m = jnp.max(logits, axis=-1, keepdims=True)
    e = jnp.exp(logits - m)
    aff_ref[...] = e / jnp.sum(e, axis=-1, keepdims=True)


def _out_proj(na_o, ml_o, x2d, w_out, na_g, ffn_g, w_router, tm=512):
    T, D = x2d.shape
    return pl.pallas_call(
        _out_proj_kernel,
        grid=(T // tm,),
        in_specs=[
            pl.BlockSpec((tm, NA_WIDTH), lambda i: (i, 0)),
            pl.BlockSpec((tm, ML_V_WIDTH), lambda i: (i, 0)),
            pl.BlockSpec((tm, D), lambda i: (i, 0)),
            pl.BlockSpec((NA_WIDTH + ML_V_WIDTH, D), lambda i: (0, 0)),
            pl.BlockSpec((1, NA_WIDTH), lambda i: (0, 0)),
            pl.BlockSpec((1, D), lambda i: (0, 0)),
            pl.BlockSpec((D, LANES), lambda i: (0, 0)),
        ],
        out_specs=[
            pl.BlockSpec((tm, D), lambda i: (i, 0)),
            pl.BlockSpec((tm, D), lambda i: (i, 0)),
            pl.BlockSpec((tm, LANES), lambda i: (i, 0)),
        ],
        out_shape=[
            jax.ShapeDtypeStruct((T, D), F32),
            jax.ShapeDtypeStruct((T, D), F32),
            jax.ShapeDtypeStruct((T, LANES), F32),
        ],
        compiler_params=_cparams(("parallel",)),
        name="out_proj",
    )(na_o, ml_o, x2d, w_out, na_g, ffn_g, w_router)


I32 = jnp.int32
SLOT_LO = 32
TOK_LO = 64
CHUNK = 256


def _route_kernel(aff_ref, tok_ref, dst_ref, gate_ref, off_ref, cnt_ref, a_scr, sel_scr, pos_scr, tmp_scr,
                  *, S, cap):
    b = pl.program_id(0)
    E = N_EXPERTS
    nchunk = S // CHUNK
    zrows = E * cap
    n_hi = cap // SLOT_LO

    TB = 512
    for c in range(S // TB):
        a_scr[:, c * TB:(c + 1) * TB] = aff_ref[c * TB:(c + 1) * TB, :].T[0:E, :]
    A = a_scr[...]

    def thr_body(i, thr):
        cand = thr | jnp.left_shift(jnp.int32(1), 30 - i)
        cand_f = lax.bitcast_convert_type(cand, F32)
        cnt = jnp.sum((a_scr[...] >= cand_f).astype(F32), axis=1, keepdims=True)
        return jnp.where(cnt >= cap, cand, thr)

    thr = lax.fori_loop(0, 31, thr_body, jnp.zeros((E, 1), I32))
    thr_f = lax.bitcast_convert_type(thr, F32)
    gt = A > thr_f
    eq = A == thr_f
    need = cap - jnp.sum(gt.astype(F32), axis=1, keepdims=True)
    idx = lax.broadcasted_iota(I32, (E, S), 1)

    nbits = S.bit_length() - 1

    def tie_body(i, jt):
        cand = jt | jnp.left_shift(jnp.int32(1), nbits - 1 - i)
        c = jnp.sum((eq & (idx < cand)).astype(F32), axis=1, keepdims=True)
        return jnp.where(c < need, cand, jt)

    jt = lax.fori_loop(0, nbits, tie_body, jnp.zeros((E, 1), I32))
    sel = (gt | (eq & (idx <= jt))).astype(F32)
    sel_scr[...] = sel

    r_i = lax.broadcasted_iota(I32, (CHUNK, CHUNK), 0)
    c_i = lax.broadcasted_iota(I32, (CHUNK, CHUNK), 1)
    strict_upper = (r_i < c_i).astype(BF16)
    e_r = lax.broadcasted_iota(I32, (E, E), 0)
    e_c = lax.broadcasted_iota(I32, (E, E), 1)
    strict_lower = (e_c < e_r).astype(BF16)
    run = jnp.zeros((E, 1), F32)
    run_t = jnp.zeros((1, 1), F32)
    for c in range(nchunk):
        sc = sel[:, c * CHUNK:(c + 1) * CHUNK]
        scb = sc.astype(BF16)
        pos_scr[:, c * CHUNK:(c + 1) * CHUNK] = jnp.dot(scb, strict_upper, preferred_element_type=F32) + run
        run = run + jnp.sum(sc, axis=1, keepdims=True)
        cnt_c = jnp.sum(sc, axis=0, keepdims=True)
        cnt8 = jnp.broadcast_to(cnt_c, (8, CHUNK)).astype(BF16)
        base_c = jnp.dot(cnt8, strict_upper, preferred_element_type=F32)[0:1] + run_t
        run_t = run_t + jnp.sum(cnt_c, axis=1, keepdims=True)
        rank_c = jnp.dot(strict_lower, scb, preferred_element_type=F32)
        zoff = (b * zrows).astype(F32)
        tmp_scr[:, c * CHUNK:(c + 1) * CHUNK] = base_c + rank_c + zoff
        off_ref[:, c * CHUNK:(c + 1) * CHUNK] = (base_c + zoff).astype(I32)
        cnt_ref[:, c * CHUNK:(c + 1) * CHUNK] = cnt_c.astype(I32)

    lo_id = lax.broadcasted_iota(I32, (SLOT_LO, S), 0).astype(F32)
    hi_id = lax.broadcasted_iota(I32, (n_hi, S), 0).astype(F32)
    t_row = lax.broadcasted_iota(I32, (1, S), 1)
    t_hi = (t_row // TOK_LO).astype(F32)
    t_lo = (t_row % TOK_LO).astype(F32)

    def split3(x):
        hi = x.astype(BF16).astype(F32)
        mid = (x - hi).astype(BF16).astype(F32)
        lo = (x - hi - mid).astype(BF16).astype(F32)
        return hi, mid, lo

    for e in range(E):
        sel_e = sel_scr[e:e + 1, :]
        pos_e = pos_scr[e:e + 1, :]
        p_hi = jnp.floor(pos_e * (1.0 / SLOT_LO))
        p_lo = pos_e - p_hi * SLOT_LO
        oh_lo = jnp.where((lo_id == p_lo) & (sel_e > 0), 1.0, 0.0).astype(BF16)
        oh_hi = jnp.where(hi_id == p_hi, 1.0, 0.0)
        d = tmp_scr[e:e + 1, :]
        d2 = jnp.floor(d * (1.0 / 1024.0))
        d1 = jnp.floor((d - d2 * 1024.0) * (1.0 / 32.0))
        d0 = d - d2 * 1024.0 - d1 * 32.0
        g_hi, g_mid, g_lo = split3(a_scr[e:e + 1, :])
        vals = (t_hi, t_lo, d2, d1, d0, g_hi, g_mid, g_lo)
        stack = jnp.concatenate([oh_hi * v for v in vals], axis=0).astype(BF16)
        r = lax.dot_general(stack, oh_lo, (((1,), (1,)), ((), ())), preferred_element_type=F32)
        rr = [r[k * n_hi:(k + 1) * n_hi] for k in range(8)]
        tok_ref[e] = (rr[0] * TOK_LO + rr[1]).astype(I32) + b * S
        dst_ref[e] = (rr[2] * 1024.0 + rr[3] * 32.0 + rr[4]).astype(I32)
        gate_ref[e] = rr[5] + rr[6] + rr[7]


def _route(aff, B, S):
    cap = CAPACITY_FACTOR * S // N_EXPERTS
    n_hi = cap // SLOT_LO
    E = N_EXPERTS
    kern = functools.partial(_route_kernel, S=S, cap=cap)
    lists = jax.ShapeDtypeStruct((B, E, n_hi, SLOT_LO), I32)
    return pl.pallas_call(
        kern,
        grid=(B,),
        in_specs=[pl.BlockSpec((S, LANES), lambda b: (b, 0))],
        out_specs=[
            pl.BlockSpec((None, E, n_hi, SLOT_LO), lambda b: (b, 0, 0, 0)),
            pl.BlockSpec((None, E, n_hi, SLOT_LO), lambda b: (b, 0, 0, 0)),
            pl.BlockSpec((None, E, n_hi, SLOT_LO), lambda b: (b, 0, 0, 0)),
            pl.BlockSpec((None, 1, S), lambda b: (b, 0, 0)),
            pl.BlockSpec((None, 1, S), lambda b: (b, 0, 0)),
        ],
        out_shape=[lists, lists, jax.ShapeDtypeStruct((B, E, n_hi, SLOT_LO), F32),
                   jax.ShapeDtypeStruct((B, 1, S), I32), jax.ShapeDtypeStruct((B, 1, S), I32)],
        scratch_shapes=[pltpu.VMEM((E, S), F32)] * 4,
        compiler_params=_cparams(("parallel",)),
        name="route",
    )(aff)


FFN_ROWS = 128
FFN_STEPS = 4
FFN_DOWN_STEPS = 2


def _ffn_kernel(tok_ref, dst_ref, h_hbm, wg_ref, wu_ref, wd_ref, gate_ref, z_hbm, xg, xb, act, yacc, sem_g, sem_s,
                *, tm):
    s = pl.program_id(2)
    q = pl.program_id(0) * pl.num_programs(1) + pl.program_id(1)
    nq = pl.num_programs(0) * pl.num_programs(1)
    tn = yacc.shape[1] // FFN_DOWN_STEPS
    last_step = FFN_STEPS + FFN_DOWN_STEPS - 1

    def gather_row(r):
        return pltpu.make_async_copy(h_hbm.at[pl.ds(tok_ref[0, r], 1), :], xg.at[pl.ds(r, 1), :], sem_g)

    def scatter_row(r):
        return pltpu.make_async_copy(yacc.at[pl.ds(r, 1), :], z_hbm.at[pl.ds(dst_ref[0, r], 1), :], sem_s)

    def for_rows(fn):
        def body(r, c):
            fn(r)
            return c
        lax.fori_loop(0, tm, body, 0, unroll=8)

    @pl.when(s == 0)
    def _():
        for_rows(lambda r: gather_row(r).start())
        for_rows(lambda r: gather_row(r).wait())

        def cast(j, c):
            rows = pl.ds(pl.multiple_of(j * FFN_ROWS, FFN_ROWS), FFN_ROWS)
            xb[rows, :] = xg[rows, :].astype(BF16)
            return c
        lax.fori_loop(0, tm // FFN_ROWS, cast, 0)

    @pl.when(s < FFN_STEPS)
    def _():
        x = xb[...]
        a = jnp.dot(x, wg_ref[...].astype(BF16), preferred_element_type=F32)
        u = jnp.dot(x, wu_ref[...].astype(BF16), preferred_element_type=F32)
        act[s] = (a * _sigmoid(a) * u).astype(BF16)

    @pl.when((s == FFN_STEPS) & (q > 0))
    def _():
        for_rows(lambda r: scatter_row(r).wait())

    @pl.when(s >= FFN_STEPS)
    def _():
        a_all = jnp.concatenate([act[f] for f in range(FFN_STEPS)], axis=1)
        y = jnp.dot(a_all, wd_ref[...].astype(BF16), preferred_element_type=F32)
        col = pl.multiple_of((s - FFN_STEPS) * tn, tn)
        yacc[:, pl.ds(col, tn)] = y * gate_ref[...]

    @pl.when(s == last_step)
    def _():
        for_rows(lambda r: scatter_row(r).start())

    @pl.when((s == last_step) & (q == nq - 1))
    def _():
        for_rows(lambda r: scatter_row(r).wait())


def _expert_ffn(h2, tok, dst, gate, w_gate, w_up, w_down, layer, tm=1024):
    E, M = tok.shape
    T, D = h2.shape
    F = w_gate.shape[-1]
    tm = min(tm, M)
    nm = M // tm
    tf = F // FFN_STEPS
    tn = D // FFN_DOWN_STEPS
    last = FFN_STEPS - 1
    tok3 = tok.reshape(E * nm, 1, tm)
    dst3 = dst.reshape(E * nm, 1, tm)
    smem_spec = pl.BlockSpec((None, 1, tm), lambda e, m, s: (e * nm + m, 0, 0), memory_space=pltpu.SMEM)
    kern = functools.partial(_ffn_kernel, tm=tm)
    return pl.pallas_call(
        kern,
        grid=(E, nm, FFN_STEPS + FFN_DOWN_STEPS),
        in_specs=[
            smem_spec,
            smem_spec,
            pl.BlockSpec(memory_space=pl.ANY),
            pl.BlockSpec((None, None, D, tf), lambda e, m, s: (layer, e, 0, jnp.minimum(s, last))),
            pl.BlockSpec((None, None, D, tf), lambda e, m, s: (layer, e, 0, jnp.minimum(s, last))),
            pl.BlockSpec((None, None, F, tn), lambda e, m, s: (layer, e, 0, jnp.maximum(s - FFN_STEPS, 0))),
            pl.BlockSpec((None, tm, 1), lambda e, m, s: (e, m, 0)),
        ],
        out_specs=pl.BlockSpec(memory_space=pl.ANY),
        out_shape=jax.ShapeDtypeStruct((E * M, D), F32),
        scratch_shapes=[
            pltpu.VMEM((tm, D), F32),
            pltpu.VMEM((tm, D), BF16),
            pltpu.VMEM((FFN_STEPS, tm, tf), BF16),
            pltpu.VMEM((tm, D), F32),
            pltpu.SemaphoreType.DMA(()),
            pltpu.SemaphoreType.DMA(()),
        ],
        compiler_params=_cparams(("arbitrary", "arbitrary", "arbitrary")),
        name="expert_ffn",
    )(tok3, dst3, h2, w_gate, w_up, w_down, gate)


COMB_TOK = 512
COMB_ROWS = 256
COMB_SLOTS = 6
COMB_AHEAD = 4


def _combine_kernel(start_ref, nch_ref, x_ref, off_ref, cnt_ref, g_ref, z_hbm, o_ref, zbuf, sem, st,
                    *, zrows, final):
    i = pl.program_id(0)
    nt = pl.num_programs(0)
    start = start_ref[i]
    n = nch_ref[i]
    i_next = jnp.minimum(i + 1, nt - 1)
    n_next = jnp.where(i + 1 < nt, nch_ref[i_next], 0)

    @pl.when(i == 0)
    def _():
        st[0] = 0
        st[1] = 0

    gc = st[0]
    pre = st[1]
    o_ref[...] = x_ref[...]
    off = off_ref[...]
    end = off + cnt_ref[...]

    def chunk_start(tile, k):
        return pl.multiple_of(jnp.minimum(start_ref[tile] + k * COMB_ROWS, zrows - COMB_ROWS), 8)

    def copy(tile, k, number):
        slot = lax.rem(number, COMB_SLOTS)
        return pltpu.make_async_copy(z_hbm.at[pl.ds(chunk_start(tile, k), COMB_ROWS), :], zbuf.at[slot],
                                     sem.at[slot])

    for j in range(COMB_AHEAD):
        @pl.when((j >= pre) & (j < n))
        def _():
            copy(i, j, gc + j).start()

    cross = n >= COMB_AHEAD

    def body(k, c):
        copy(i, k, gc + k).wait()
        ahead = k + COMB_AHEAD

        @pl.when(ahead < n)
        def _():
            copy(i, ahead, gc + ahead).start()

        @pl.when((ahead >= n) & cross & (ahead - n < n_next))
        def _():
            copy(i_next, ahead - n, gc + ahead).start()

        rows = chunk_start(i, k) + lax.broadcasted_iota(I32, (1, COMB_ROWS), 1)
        lo = jnp.maximum(off, start + k * COMB_ROWS)
        seg = jnp.where((rows >= lo) & (rows < end), 1.0, 0.0).astype(BF16)
        o_ref[...] += jnp.dot(seg, zbuf[lax.rem(gc + k, COMB_SLOTS)].astype(BF16), preferred_element_type=F32)
        return c

    lax.fori_loop(0, n, body, 0)
    st[0] = gc + n
    st[1] = jnp.where(cross, jnp.minimum(COMB_AHEAD, n_next), 0)

    if final:
        x = o_ref[...]
        ms = jnp.mean(x * x, axis=-1, keepdims=True)
        o_ref[...] = x * lax.rsqrt(ms + EPS) * g_ref[...]


def _combine(x2d, z, off, cnt, gain, final):
    T, D = x2d.shape
    zrows = z.shape[0]
    tt = min(COMB_TOK, T)
    nt = T // tt
    start = (off[::tt] // 8) * 8
    last = (off + cnt)[tt - 1::tt]
    nch = (last - start + COMB_ROWS - 1) // COMB_ROWS
    kern = functools.partial(_combine_kernel, zrows=zrows, final=final)
    grid_spec = pltpu.PrefetchScalarGridSpec(
        num_scalar_prefetch=2,
        grid=(nt,),
        in_specs=[
            pl.BlockSpec((tt, D), lambda i, s, n: (i, 0)),
            pl.BlockSpec((tt, 1), lambda i, s, n: (i, 0)),
            pl.BlockSpec((tt, 1), lambda i, s, n: (i, 0)),
            pl.BlockSpec((1, D), lambda i, s, n: (0, 0)),
            pl.BlockSpec(memory_space=pl.ANY),
        ],
        out_specs=pl.BlockSpec((tt, D), lambda i, s, n: (i, 0)),
        scratch_shapes=[pltpu.VMEM((COMB_SLOTS, COMB_ROWS, D), F32), pltpu.SemaphoreType.DMA((COMB_SLOTS,)),
                        pltpu.SMEM((2,), I32)],
    )
    return pl.pallas_call(
        kern,
        grid_spec=grid_spec,
        out_shape=jax.ShapeDtypeStruct((T, D), F32),
        compiler_params=_cparams(("arbitrary",)),
        name="combine",
    )(start.astype(I32), nch.astype(I32), x2d, off[:, None], cnt[:, None], gain, z)


def moe(x2d, h2, aff, w_gate, w_up, w_down, layer, B, S, gain, final):
    E = N_EXPERTS
    cap = CAPACITY_FACTOR * S // E
    tok, dst, gate, off, cnt = _route(aff, B, S)
    to_em = lambda a: jnp.swapaxes(a.reshape(B, E, cap), 0, 1).reshape(E, B * cap)
    z = _expert_ffn(h2, to_em(tok), to_em(dst), to_em(gate)[:, :, None], w_gate, w_up, w_down, layer)
    return _combine(x2d, z, off.reshape(B * S), cnt.reshape(B * S), gain, final)


def kernel(x, norm_mix_g, w_in, b_gates, conv_w, conv_b, na_rpb, na_norm_g, ml_norm_g, w_out,
           norm_ffn_g, w_router, w_gate, w_up, w_down, final_norm_g):
    B, S, D = x.shape
    T = B * S
    depth = w_in.shape[0]
    x2d = x.reshape(T, D)
    w_t = jnp.swapaxes(w_in, 1, 2)
    w_main_t = _cast_main_weights(w_t)

    for l in range(depth):
        proj, gates = _in_proj(x2d, norm_mix_g[l][None, :], w_main_t, w_t, l)

        tb = _na_bias_table(na_rpb[l])
        na_o = _na_attention(proj, tb, B, S)

        g4 = gates.reshape(B, S, 4, ML_HEADS)
        g_rows = jnp.pad(jnp.transpose(g4, (0, 3, 2, 1)), ((0, 0), (0, 0), (0, 4), (0, 0)))
        bg_rows = jnp.pad(b_gates[l].reshape(4, ML_HEADS).T, ((0, 0), (0, 4)))[:, :, None]
        ml_o = _mlstm(proj, g_rows, bg_rows.astype(F32), conv_w[l], conv_b[l][None, :],
                      ml_norm_g[l][None, :], B, S)

        w_r = jnp.pad(w_router[l], ((0, 0), (0, LANES - N_EXPERTS))).astype(BF16)
        x2d, h2, aff = _out_proj(na_o, ml_o, x2d, w_out[l].astype(BF16), na_norm_g[l][None, :],
                                 norm_ffn_g[l][None, :], w_r)

        x2d = moe(x2d, h2, aff, w_gate, w_up, w_down, l, B, S, final_norm_g[None, :], l == depth - 1)

    return x2d.reshape(B, S, D)
```
